```python
import jax, jax.numpy as jnp
from jax import lax
import numpy as np

D_MODEL = 1024
BATCH = 16
SEQ = 256
DEPTH = 4
DEC_BATCH = 8
DEC_SEQ = 2048
PAST_LEN = 256

GRID_W = 64
HEAD_DIM = 64
EPS = 1e-6
NA_HEADS = 8
NA_WIDTH = NA_HEADS * HEAD_DIM
NA_KR = 8
NA_KC = 16
GLA_HEADS = 4
GLA_DK = 64
GLA_DV = 128
GLA_KW = GLA_HEADS * GLA_DK
GLA_VW = GLA_HEADS * GLA_DV
GLA_RANK = 16
GLA_GATE_NORM = 16.0
GLA_CHUNK = 64
GQA_HEADS = 16
GQA_KV_HEADS = 4
GQA_GROUP = GQA_HEADS // GQA_KV_HEADS
C_QW = GQA_HEADS * HEAD_DIM
C_KW = GQA_KV_HEADS * HEAD_DIM
C_IN = C_QW + 2 * C_KW
ROPE_THETA = 10000.0
ROPE_AXIS_DIM = HEAD_DIM // 2
Q_BLOCK = 128
D_FF = 2816
CONV_W = 3
N_AB = (DEPTH + 1) // 2
N_C = DEPTH // 2
AB_IN = 3 * NA_WIDTH + 2 * GLA_KW + 2 * GLA_VW + 2 * GLA_RANK
AB_OUT = NA_WIDTH + GLA_VW
AB_SPLITS = (NA_WIDTH, 2 * NA_WIDTH, 3 * NA_WIDTH,
             3 * NA_WIDTH + GLA_KW, 3 * NA_WIDTH + 2 * GLA_KW,
             3 * NA_WIDTH + 2 * GLA_KW + GLA_VW, 3 * NA_WIDTH + 2 * GLA_KW + 2 * GLA_VW,
             3 * NA_WIDTH + 2 * GLA_KW + 2 * GLA_VW + GLA_RANK)

kernel_name = "hybrid_na_gla_gqa_prefix_dit_step"


def rms_norm(x, g):
    xf = x.astype(jnp.float32)
    xf = xf * lax.rsqrt(jnp.mean(xf * xf, axis=-1, keepdims=True) + EPS)
    return (xf * g.astype(jnp.float32)).astype(x.dtype)


def heads(a, nh):
    b, n, w = a.shape
    return a.reshape(b, n, nh, w // nh).transpose(0, 2, 1, 3)


def merge(a):
    b, h, n, d = a.shape
    return a.transpose(0, 2, 1, 3).reshape(b, n, h * d)


def modulation(cvec, w, bias):
    mod = (jax.nn.silu(cvec) @ w + bias).reshape(-1, 1, 6 * D_MODEL)
    return jnp.split(mod, 6, axis=-1)


def block_attention(q, k, v):
    b, kvh, g, nq, d = q.shape
    nb = nq // Q_BLOCK
    qb = jnp.moveaxis(q.reshape(b, kvh, g, nb, Q_BLOCK, d), 3, 0)
    scale = d ** -0.5

    def one(qblk):
        s = jnp.einsum('bhgqd,bhkd->bhgqk', qblk, k).astype(jnp.float32) * scale
        p = jax.nn.softmax(s, axis=-1).astype(v.dtype)
        return jnp.einsum('bhgqk,bhkd->bhgqd', p, v)

    o = lax.map(one, qb)
    return jnp.moveaxis(o, 0, 3).reshape(b, kvh, g, nq, d)


def grid_angles(n):
    t = jnp.arange(n)
    row = (t // GRID_W).astype(jnp.float32)
    col = (t % GRID_W).astype(jnp.float32)
    inv = ROPE_THETA ** (-jnp.arange(0, ROPE_AXIS_DIM, 2, dtype=jnp.float32) / ROPE_AXIS_DIM)
    return row[:, None] * inv, col[:, None] * inv


def rope_half(x, ang):
    x1, x2 = jnp.split(x, 2, axis=-1)
    cos, sin = jnp.cos(ang).astype(x.dtype), jnp.sin(ang).astype(x.dtype)
    return jnp.concatenate([x1 * cos - x2 * sin, x2 * cos + x1 * sin], axis=-1)


def rope_2d(x, ang_r, ang_c):
    xr, xc = jnp.split(x, 2, axis=-1)
    return jnp.concatenate([rope_half(xr, ang_r), rope_half(xc, ang_c)], axis=-1)


def neighbourhood_attention(q, k, v, k_ctx, v_ctx, rpb):
    b, h, n, d = q.shape
    rows = n // GRID_W
    kr = min(NA_KR, rows)
    r = jnp.arange(rows)
    rs = jnp.clip(r - kr // 2, 0, rows - kr)
    key_rows = rs[:, None] + jnp.arange(kr)[None, :]
    col = jnp.arange(GRID_W)
    cs = jnp.clip(col - NA_KC // 2, 0, GRID_W - NA_KC)
    col_ok = (col[None, :] >= cs[:, None]) & (col[None, :] < cs[:, None] + NA_KC)
    mask = jnp.broadcast_to(col_ok[:, None, :], (GRID_W, kr, GRID_W)).reshape(GRID_W, kr * GRID_W)
    qg = q.reshape(b, h, rows, GRID_W, d)
    kg = k.reshape(b, h, rows, GRID_W, d)[:, :, key_rows].reshape(b, h, rows, kr * GRID_W, d)
    vg = v.reshape(b, h, rows, GRID_W, d)[:, :, key_rows].reshape(b, h, rows, kr * GRID_W, d)
    dr = key_rows - r[:, None] + NA_KR - 1
    dc = jnp.clip(col[None, :] - col[:, None] + NA_KC - 1, 0, 2 * NA_KC - 2)
    bias = rpb.astype(jnp.float32)[:, dr[:, None, :, None], dc[None, :, None, :]]
    bias = bias.reshape(h, rows, GRID_W, kr * GRID_W)
    scale = d ** -0.5
    s_loc = jnp.einsum('bhrqd,bhrkd->bhrqk', qg, kg).astype(jnp.float32) * scale + bias
    s_loc = jnp.where(mask, s_loc, -jnp.inf)
    s_ctx = jnp.einsum('bhrqd,bhld->bhrql', qg, k_ctx).astype(jnp.float32) * scale
    p = jax.nn.softmax(jnp.concatenate([s_loc, s_ctx], axis=-1), axis=-1).astype(v.dtype)
    nk = kr * GRID_W
    o = (jnp.einsum('bhrqk,bhrkd->bhrqd', p[..., :nk], vg)
         + jnp.einsum('bhrql,bhld->bhrqd', p[..., nk:], v_ctx))
    return o.reshape(b, h, n, d)


def gla_scan(q, k, v, g, s0):
    b, h, t, dk = q.shape
    dv = v.shape[-1]
    nc = t // GLA_CHUNK

    def chunks(a):
        return jnp.moveaxis(a.reshape(b, h, nc, GLA_CHUNK, a.shape[-1]), 2, 0)

    causal = jnp.tril(jnp.ones((GLA_CHUNK, GLA_CHUNK), dtype=bool))

    def step(s, inp):
        qc, kc, vc, gc = inp
        bcum = jnp.cumsum(gc.astype(jnp.float32), axis=2)
        o_inter = jnp.einsum('bhid,bhde->bhie', qc * jnp.exp(bcum).astype(qc.dtype), s)
        diff = bcum[:, :, :, None, :] - bcum[:, :, None, :, :]
        decay = jnp.exp(jnp.where(causal[:, :, None], diff, -jnp.inf)).astype(qc.dtype)
        att = jnp.einsum('bhid,bhjd,bhijd->bhij', qc, kc, decay)
        o = o_inter + jnp.einsum('bhij,bhje->bhie', att, vc)
        blast = bcum[:, :, -1:, :]
        k_dec = kc * jnp.exp(blast - bcum).astype(kc.dtype)
        s_new = (jnp.exp(blast[:, :, 0, :])[..., None].astype(s.dtype) * s
                 + jnp.einsum('bhjd,bhje->bhde', k_dec, vc)).astype(s.dtype)
        return s_new, o

    s_fin, o = lax.scan(step, s0, (chunks(q), chunks(k), chunks(v), chunks(g)))
    return jnp.moveaxis(o, 0, 2).reshape(b, h, t, dv), s_fin


def gla_bidir(q, k, v, g_f, g_b, s0_f, s0_b):
    o_f, s_f = gla_scan(q, k, v, g_f, s0_f)
    flip = lambda a: jnp.flip(a, axis=2)
    o_b, s_b = gla_scan(flip(q), flip(k), flip(v), flip(g_b), s0_b)
    return o_f + flip(o_b), s_f, s_b


def ab_mixer(h, w_in, w_out, rpb, w2_f, b2_f, w2_b, b2_b, gla_g, ctx):
    b, n, _ = h.shape
    qa, ka, va, qg, kg, vg, rg, lr_f, lr_b = jnp.split(h @ w_in, AB_SPLITS, axis=-1)
    qa, ka, va = heads(qa, NA_HEADS), heads(ka, NA_HEADS), heads(va, NA_HEADS)
    qg = heads(qg, GLA_HEADS) * (GLA_DK ** -0.5)
    kg, vg = heads(kg, GLA_HEADS), heads(vg, GLA_HEADS)
    gf = heads(jax.nn.log_sigmoid(lr_f @ w2_f + b2_f) / GLA_GATE_NORM, GLA_HEADS)
    gb = heads(jax.nn.log_sigmoid(lr_b @ w2_b + b2_b) / GLA_GATE_NORM, GLA_HEADS)
    if ctx is None:
        oa = block_attention(qa[:, :, None], ka, va)[:, :, 0]
        zeros = jnp.zeros((b, GLA_HEADS, GLA_DK, GLA_DV), h.dtype)
        og, s_f, s_b = gla_bidir(qg, kg, vg, gf, gb, zeros, zeros)
        new = (ka, va, s_f, s_b)
    else:
        k_ctx, v_ctx, s0_f, s0_b = ctx
        oa = neighbourhood_attention(qa, ka, va, k_ctx, v_ctx, rpb)
        og, _, _ = gla_bidir(qg, kg, vg, gf, gb, s0_f, s0_b)
        new = None
    og = merge(rms_norm(og, gla_g.reshape(GLA_HEADS, 1, GLA_DV))) * jax.nn.silu(rg)
    out = jnp.concatenate([merge(oa), og], axis=-1) @ w_out
    return out, new


def c_mixer(h, w_in, w_out, qn_g, kn_g, ctx):
    b, n, _ = h.shape
    q, k, v = jnp.split(h @ w_in, (C_QW, C_QW + C_KW), axis=-1)
    q = rms_norm(heads(q, GQA_HEADS), qn_g)
    k = rms_norm(heads(k, GQA_KV_HEADS), kn_g)
    v = heads(v, GQA_KV_HEADS)
    if ctx is None:
        o = block_attention(q.reshape(b, GQA_KV_HEADS, GQA_GROUP, n, HEAD_DIM), k, v)
        new = (k, v)
    else:
        k_ctx, v_ctx = ctx
        ang_r, ang_c = grid_angles(n)
        q = rope_2d(q, ang_r, ang_c)
        k = rope_2d(k, ang_r, ang_c)
        k_all = jnp.concatenate([k, k_ctx], axis=2)
        v_all = jnp.concatenate([v, v_ctx], axis=2)
        o = block_attention(q.reshape(b, GQA_KV_HEADS, GQA_GROUP, n, HEAD_DIM), k_all, v_all)
        new = None
    return merge(o.reshape(b, GQA_HEADS, n, HEAD_DIM)) @ w_out, new


def conv_ffn(h, w_up, conv_w, conv_b, w_down):
    u = h @ w_up
    up = jnp.pad(u, ((0, 0), (1, 1), (0, 0)))
    u = up[:, :-2] * conv_w[0] + up[:, 1:-1] * conv_w[1] + up[:, 2:] * conv_w[2] + conv_b
    val, gate = jnp.split(u, 2, axis=-1)
    return (jax.nn.silu(gate) * val) @ w_down


def setup_inputs(seed: int = 0) -> dict:
    key = jax.random.key(seed)
    ks = jax.random.split(key, 40)
    nrm = lambda i, shape, s=1.0: jax.random.normal(ks[i], shape, jnp.float32) * s
    D = D_MODEL
    return {
        "x_prompt": nrm(0, (BATCH, SEQ, D)),
        "x_sample": nrm(1, (DEC_BATCH, DEC_SEQ, D)),
        "cache_na_k": nrm(2, (DEC_BATCH, N_AB, NA_HEADS, PAST_LEN, HEAD_DIM)),
        "cache_na_v": nrm(3, (DEC_BATCH, N_AB, NA_HEADS, PAST_LEN, HEAD_DIM)),
        "state_gla_fwd": nrm(4, (DEC_BATCH, N_AB, GLA_HEADS, GLA_DK, GLA_DV)),
        "state_gla_bwd": nrm(5, (DEC_BATCH, N_AB, GLA_HEADS, GLA_DK, GLA_DV)),
        "cache_gqa_k": nrm(6, (DEC_BATCH, N_C, GQA_KV_HEADS, PAST_LEN, HEAD_DIM)),
        "cache_gqa_v": nrm(7, (DEC_BATCH, N_C, GQA_KV_HEADS, PAST_LEN, HEAD_DIM)),
        "c": nrm(8, (DEC_BATCH, D)),
        "c_ctx": nrm(9, (D,)),
        "ada_w": nrm(10, (DEPTH, D, 6 * D), 0.5 * D ** -0.5),
        "ada_b": nrm(11, (DEPTH, 6 * D), 0.02),
        "norm_mix_g": 1.0 + nrm(12, (DEPTH, D), 0.1),
        "norm_ffn_g": 1.0 + nrm(13, (DEPTH, D), 0.1),
        "ab_w_in": nrm(14, (N_AB, D, AB_IN), D ** -0.5),
        "ab_w_out": nrm(15, (N_AB, AB_OUT, D), AB_OUT ** -0.5),
        "na_rpb": nrm(16, (N_AB, NA_HEADS, 2 * NA_KR - 1, 2 * NA_KC - 1), 0.2),
        "gla_w2_fwd": nrm(17, (N_AB, GLA_RANK, GLA_KW), GLA_RANK ** -0.5),
        "gla_b2_fwd": nrm(18, (N_AB, GLA_KW), 0.5),
        "gla_w2_bwd": nrm(19, (N_AB, GLA_RANK, GLA_KW), GLA_RANK ** -0.5),
        "gla_b2_bwd": nrm(20, (N_AB, GLA_KW), 0.5),
        "gla_norm_g": 1.0 + nrm(21, (N_AB, GLA_VW), 0.1),
        "gqa_w_in": nrm(22, (N_C, D, C_IN), D ** -0.5),
        "gqa_w_out": nrm(23, (N_C, C_QW, D), C_QW ** -0.5),
        "gqa_q_norm_g": 1.0 + nrm(24, (N_C, HEAD_DIM), 0.1),
        "gqa_k_norm_g": 1.0 + nrm(25, (N_C, HEAD_DIM), 0.1),
        "ffn_w_up": nrm(26, (DEPTH, D, 2 * D_FF), D ** -0.5),
        "ffn_conv_w": nrm(27, (DEPTH, CONV_W, 2 * D_FF), CONV_W ** -0.5),
        "ffn_conv_b": nrm(28, (DEPTH, 2 * D_FF), 0.02),
        "ffn_w_down": nrm(29, (DEPTH, D_FF, D), D_FF ** -0.5),
        "final_norm_g": 1.0 + nrm(30, (D,), 0.1),
    }


def reference(x_prompt, x_sample, cache_na_k, cache_na_v, state_gla_fwd, state_gla_bwd,
              cache_gqa_k, cache_gqa_v, c, c_ctx, ada_w, ada_b, norm_mix_g, norm_ffn_g,
              ab_w_in, ab_w_out, na_rpb, gla_w2_fwd, gla_b2_fwd, gla_w2_bwd, gla_b2_bwd, gla_norm_g,
              gqa_w_in, gqa_w_out, gqa_q_norm_g, gqa_k_norm_g,
              ffn_w_up, ffn_conv_w, ffn_conv_b, ffn_w_down, final_norm_g):
    xp, xs = x_prompt, x_sample
    na_k, na_v, gla_f, gla_b, gq_k, gq_v = [], [], [], [], [], []
    for i in range(DEPTH):
        j = i // 2
        sh1p, sc1p, g1p, sh2p, sc2p, g2p = modulation(c_ctx, ada_w[i], ada_b[i])
        sh1s, sc1s, g1s, sh2s, sc2s, g2s = modulation(c, ada_w[i], ada_b[i])
        hp = rms_norm(xp, norm_mix_g[i]) * (1.0 + sc1p) + sh1p
        hs = rms_norm(xs, norm_mix_g[i]) * (1.0 + sc1s) + sh1s
        if i % 2 == 0:
            prm = (ab_w_in[j], ab_w_out[j], na_rpb[j], gla_w2_fwd[j], gla_b2_fwd[j],
                   gla_w2_bwd[j], gla_b2_bwd[j], gla_norm_g[j])
            op, (ka, va, sf, sb) = ab_mixer(hp, *prm, ctx=None)
            os_, _ = ab_mixer(hs, *prm, ctx=(cache_na_k[:, j], cache_na_v[:, j],
                                             state_gla_fwd[:, j], state_gla_bwd[:, j]))
            na_k.append(ka)
            na_v.append(va)
            gla_f.append(sf)
            gla_b.append(sb)
        else:
            prm = (gqa_w_in[j], gqa_w_out[j], gqa_q_norm_g[j], gqa_k_norm_g[j])
            op, (kc, vc) = c_mixer(hp, *prm, ctx=None)
            os_, _ = c_mixer(hs, *prm, ctx=(cache_gqa_k[:, j], cache_gqa_v[:, j]))
            gq_k.append(kc)
            gq_v.append(vc)
        xp = xp + g1p * op
        xs = xs + g1s * os_
        ffn = (ffn_w_up[i], ffn_conv_w[i], ffn_conv_b[i], ffn_w_down[i])
        xp = xp + g2p * conv_ffn(rms_norm(xp, norm_ffn_g[i]) * (1.0 + sc2p) + sh2p, *ffn)
        xs = xs + g2s * conv_ffn(rms_norm(xs, norm_ffn_g[i]) * (1.0 + sc2s) + sh2s, *ffn)
    y_prompt = rms_norm(xp, final_norm_g)
    y_sample = rms_norm(xs, final_norm_g)
    new_na_k = jnp.stack(na_k, axis=1)
    new_na_v = jnp.stack(na_v, axis=1)
    new_gla_fwd = jnp.stack(gla_f, axis=1)
    new_gla_bwd = jnp.stack(gla_b, axis=1)
    new_gqa_k = jnp.stack(gq_k, axis=1)
    new_gqa_v = jnp.stack(gq_v, axis=1)
    return (y_prompt, y_sample, new_na_k, new_na_v, new_gla_fwd, new_gla_bwd, new_gqa_k, new_gqa_v)
```

```python
import functools

import numpy as np
import jax
import jax.numpy as jnp
from jax import lax
from jax.experimental import pallas as pl
from jax.experimental.pallas import tpu as pltpu

F32 = jnp.float32
BF16 = jnp.bfloat16

D = 1024
DEPTH = 4
HD = 64
GRID_W = 64
EPS = 1e-6
NA_HEADS = 8
NA_W = NA_HEADS * HD
NA_KR = 8
NA_KC = 16
GLA_HEADS = 4
GLA_DK = 64
GLA_DV = 128
GLA_KW = GLA_HEADS * GLA_DK
GLA_VW = GLA_HEADS * GLA_DV
GLA_RANK = 16
GLA_CHUNK = 64
GQA_HEADS = 16
GQA_KVH = 4
C_QW = GQA_HEADS * HD
C_KW = GQA_KVH * HD
ROPE_THETA = 10000.0
D_FF = 2816
AB_MAIN = 3 * NA_W + 2 * GLA_KW + 2 * GLA_VW

LANES = 128
TM = 256
HALO = 16
FF_CHUNK = 256
NA_QROWS = 8
NA_KROWS = NA_QROWS + NA_KR - 1
VMEM_LIMIT = 56 * 1024 * 1024


def _dot(a, b):
    return jnp.dot(a, b, preferred_element_type=F32)


def _dot_nt(a, b):
    return lax.dot_general(a, b, (((1,), (1,)), ((), ())), preferred_element_type=F32)


def _dot_tn(a, b):
    return lax.dot_general(a, b, (((0,), (0,)), ((), ())), preferred_element_type=F32)


def _split(a):
    hi = a.astype(BF16)
    lo = (a - hi.astype(F32)).astype(BF16)
    return hi, lo


def _dot_split(a, w_hi, w_lo):
    a_hi, a_lo = _split(a)
    return _dot(a_hi, w_hi) + _dot(a_lo, w_hi) + _dot(a_hi, w_lo)


def _norm_mod(x, g, sc, sh):
    xn = x * lax.rsqrt(jnp.mean(x * x, axis=-1, keepdims=True) + EPS)
    return (xn * g) * (1.0 + sc) + sh


def _silu(x):
    return x / (1.0 + jnp.exp(-x))


def _lane_lo(shape=(1, LANES)):
    return lax.broadcasted_iota(jnp.int32, shape, len(shape) - 1) < HD


def _stack_heads(x):
    lo = _lane_lo()
    zero = jnp.zeros_like(x)
    return jnp.concatenate([jnp.where(lo, x, zero), jnp.where(lo, zero, x)], axis=0)


def _params(sem):
    return pltpu.CompilerParams(dimension_semantics=sem, vmem_limit_bytes=VMEM_LIMIT)


def _mod_kernel(c_ref, w_ref, b_ref, o_ref):
    a = _silu(c_ref[...])
    w = w_ref[...]
    w_hi, w_lo = _split(w)
    o_ref[...] = _dot_split(a, w_hi, w_lo) + b_ref[...]


def _modulation(c_rows, ada_w, ada_b):
    nrow = c_rows.shape[0]
    tn = 1024
    return pl.pallas_call(
        _mod_kernel,
        out_shape=jax.ShapeDtypeStruct((DEPTH, nrow, 6 * D), F32),
        grid=(DEPTH, 6 * D // tn),
        in_specs=[pl.BlockSpec((nrow, D), lambda l, n: (0, 0)),
                  pl.BlockSpec((None, D, tn), lambda l, n: (l, 0, n)),
                  pl.BlockSpec((None, 1, tn), lambda l, n: (l, 0, n))],
        out_specs=pl.BlockSpec((None, nrow, tn), lambda l, n: (l, 0, n)),
        compiler_params=_params(("parallel", "parallel")),
        name="modulation",
    )(c_rows, ada_w, ada_b.reshape(DEPTH, 1, 6 * D))


class _Geom:
    def __init__(self, b, s, db, ds):
        assert s == TM and ds % TM == 0
        self.b, self.s, self.db, self.ds = b, s, db, ds
        self.np_rows = b * s
        self.ns_rows = db * ds
        self.n = self.np_rows + self.ns_rows
        self.npt = self.np_rows // TM
        self.tps = ds // TM
        self.ctx_row = db

    def mod_spec(self, k):
        npt, tps, ctx = self.npt, self.tps, self.ctx_row

        def imap(i):
            return (jnp.where(i < npt, ctx, (i - npt) // tps), 0, k)
        return pl.BlockSpec((None, 1, D), imap)


def _const_spec(shape):
    nd = len(shape)
    return pl.BlockSpec(shape, lambda *_: (0,) * nd)


def _ab_in_kernel(x_ref, g_ref, sc_ref, sh_ref, w_ref, w2hi_ref, w2lo_ref, b2_ref, o_ref, gate_ref):
    h = _norm_mod(x_ref[...], g_ref[...], sc_ref[...], sh_ref[...]).astype(BF16)
    cw = 512
    for j in range(AB_MAIN // cw):
        o_ref[:, j * cw:(j + 1) * cw] = _dot(h, w_ref[:, j * cw:(j + 1) * cw]).astype(BF16)
    lr = _dot(h, w_ref[:, AB_MAIN:AB_MAIN + LANES])
    z = _dot_split(lr, w2hi_ref[...], w2lo_ref[...]) + b2_ref[...]
    gate_ref[...] = (jnp.minimum(z, 0.0) - jnp.log(1.0 + jnp.exp(-jnp.abs(z)))) * (1.0 / 16.0)


def _ab_in_proj(geom, x, mod_l, g, w, w2hi, w2lo, b2):
    return pl.pallas_call(
        _ab_in_kernel,
        out_shape=(jax.ShapeDtypeStruct((geom.n, AB_MAIN), BF16),
                   jax.ShapeDtypeStruct((geom.n, 2 * GLA_KW), F32)),
        grid=(geom.n // TM,),
        in_specs=[pl.BlockSpec((TM, D), lambda i: (i, 0)),
                  _const_spec((1, D)),
                  geom.mod_spec(1), geom.mod_spec(0),
                  _const_spec(w.shape), _const_spec(w2hi.shape), _const_spec(w2lo.shape),
                  _const_spec(b2.shape)],
        out_specs=(pl.BlockSpec((TM, AB_MAIN), lambda i: (i, 0)),
                   pl.BlockSpec((TM, 2 * GLA_KW), lambda i: (i, 0))),
        compiler_params=_params(("parallel",)),
        name="ab_in_proj",
    )(x, g, mod_l, mod_l, w, w2hi, w2lo, b2)


def _flash_step(qs, kc, vc, m, l, acc):
    s = _dot_nt(qs, kc)
    m_new = jnp.maximum(m, jnp.max(s, axis=-1, keepdims=True))
    alpha = jnp.exp(m - m_new)
    p = jnp.exp(s - m_new)
    l = alpha * l + jnp.sum(p, axis=-1, keepdims=True)
    acc = alpha * acc + _dot(p.astype(BF16), vc)
    return m_new, l, acc


def _pair_attn_kernel(*refs, tk, has_ctx):
    if has_ctx:
        q_ref, k_ref, v_ref, kc_ref, vc_ref, _, o_ref = refs
    else:
        q_ref, k_ref, v_ref, o_ref = refs
    tq = q_ref.shape[0]
    qs = _stack_heads(q_ref[...])
    m = jnp.full((2 * tq, 1), -jnp.inf, F32)
    l = jnp.zeros((2 * tq, 1), F32)
    acc = jnp.zeros((2 * tq, LANES), F32)
    for c in range(k_ref.shape[0] // tk):
        m, l, acc = _flash_step(qs, k_ref[c * tk:(c + 1) * tk, :], v_ref[c * tk:(c + 1) * tk, :], m, l, acc)
    if has_ctx:
        m, l, acc = _flash_step(qs, kc_ref[...], vc_ref[...], m, l, acc)
    o = acc / l
    o_ref[...] = jnp.where(_lane_lo(), o[:tq], o[tq:]).astype(BF16)


def _na_kernel(q_ref, k_ref, v_ref, kc_ref, vc_ref, bias_ref, _, o_ref):
    rb = pl.program_id(1)
    nq = q_ref.shape[0]
    nk = NA_KROWS * GRID_W
    last = k_ref.shape[0] // GRID_W - NA_KROWS
    k0 = pl.multiple_of(jnp.clip(NA_QROWS * rb - NA_KR // 2, 0, last) * GRID_W, GRID_W)
    qs = _stack_heads(q_ref[...])
    kl = k_ref[pl.ds(k0, nk), :]
    vl = v_ref[pl.ds(k0, nk), :]
    s_loc = _dot_nt(qs, kl) + bias_ref[...]
    s_ctx = _dot_nt(qs, kc_ref[...])
    m = jnp.maximum(jnp.max(s_loc, axis=-1, keepdims=True), jnp.max(s_ctx, axis=-1, keepdims=True))
    p_loc = jnp.exp(s_loc - m)
    p_ctx = jnp.exp(s_ctx - m)
    l = jnp.sum(p_loc, axis=-1, keepdims=True) + jnp.sum(p_ctx, axis=-1, keepdims=True)
    o = (_dot(p_loc.astype(BF16), vl) + _dot(p_ctx.astype(BF16), vc_ref[...])) / l
    o_ref[...] = jnp.where(_lane_lo(), o[:nq], o[nq:]).astype(BF16)


def _na_bias_tiles(rpb, rows):
    tiles = []
    for r0 in (0, NA_QROWS, rows - NA_QROWS):
        kstart = int(np.clip(r0 - NA_KR // 2, 0, rows - NA_KROWS))
        r = r0 + np.arange(NA_QROWS)
        kr = kstart + np.arange(NA_KROWS)
        rs = np.clip(r - NA_KR // 2, 0, rows - NA_KR)
        row_ok = (kr[None, :] >= rs[:, None]) & (kr[None, :] < rs[:, None] + NA_KR)
        dr = np.clip(kr[None, :] - r[:, None] + NA_KR - 1, 0, 2 * NA_KR - 2)
        col = np.arange(GRID_W)
        cs = np.clip(col - NA_KC // 2, 0, GRID_W - NA_KC)
        col_ok = (col[None, :] >= cs[:, None]) & (col[None, :] < cs[:, None] + NA_KC)
        dc = np.clip(col[None, :] - col[:, None] + NA_KC - 1, 0, 2 * NA_KC - 2)
        bias = rpb.astype(F32)[:, dr[:, None, :, None], dc[None, :, None, :]]
        ok = row_ok[:, None, :, None] & col_ok[None, :, None, :]
        bias = jnp.where(ok[None], bias, -jnp.inf)
        tiles.append(bias.reshape(NA_HEADS // 2, 2 * NA_QROWS * GRID_W, NA_KROWS * GRID_W))
    return jnp.stack(tiles, axis=1)


def _na_attention(geom, proj, k_ctx, v_ctx, rpb):
    npairs = NA_HEADS // 2
    s, ds = geom.s, geom.ds
    oa = pl.pallas_call(
        functools.partial(_pair_attn_kernel, tk=s, has_ctx=False),
        out_shape=jax.ShapeDtypeStruct((geom.n, NA_W), BF16),
        grid=(geom.b, npairs),
        in_specs=[pl.BlockSpec((s, LANES), lambda b, p: (b, p)),
                  pl.BlockSpec((s, LANES), lambda b, p: (b, npairs + p)),
                  pl.BlockSpec((s, LANES), lambda b, p: (b, 2 * npairs + p))],
        out_specs=pl.BlockSpec((s, LANES), lambda b, p: (b, p)),
        compiler_params=_params(("parallel", "parallel")),
        name="na_prompt_attn",
    )(proj, proj, proj)

    rows = ds // GRID_W
    assert rows % NA_QROWS == 0 and rows >= NA_KROWS
    nrb = rows // NA_QROWS
    nq = NA_QROWS * GRID_W
    nk = NA_KROWS * GRID_W
    bias = _na_bias_tiles(rpb, rows)
    q0 = geom.np_rows // nq
    s0 = geom.np_rows // ds
    kc = k_ctx.reshape(geom.db, npairs, 2, -1, HD).transpose(0, 1, 3, 2, 4).reshape(geom.db, npairs, -1, LANES).astype(BF16)
    vc = v_ctx.reshape(geom.db, npairs, 2, -1, HD).transpose(0, 1, 3, 2, 4).reshape(geom.db, npairs, -1, LANES).astype(BF16)
    nctx = kc.shape[2]
    return pl.pallas_call(
        _na_kernel,
        out_shape=jax.ShapeDtypeStruct((geom.n, NA_W), BF16),
        grid=(npairs, nrb, geom.db),
        in_specs=[pl.BlockSpec((nq, LANES), lambda p, r, b: (q0 + nrb * b + r, p)),
                  pl.BlockSpec((ds, LANES), lambda p, r, b: (s0 + b, npairs + p)),
                  pl.BlockSpec((ds, LANES), lambda p, r, b: (s0 + b, 2 * npairs + p)),
                  pl.BlockSpec((None, None, nctx, LANES), lambda p, r, b: (b, p, 0, 0)),
                  pl.BlockSpec((None, None, nctx, LANES), lambda p, r, b: (b, p, 0, 0)),
                  pl.BlockSpec((None, None, 2 * nq, nk),
                               lambda p, r, b: (p, jnp.minimum(r, 1) + r // (nrb - 1), 0, 0)),
                  pl.BlockSpec(memory_space=pl.ANY)],
        out_specs=pl.BlockSpec((nq, LANES), lambda p, r, b: (q0 + nrb * b + r, p)),
        input_output_aliases={6: 0},
        compiler_params=_params(("parallel", "parallel", "parallel")),
        name="na_sample_attn",
    )(proj, proj, proj, kc, vc, bias, oa)


GLA_LEVELS = 6


def _gla_constants():
    c = GLA_CHUNK
    t = np.arange(c)
    out = {}
    for name in ("fwd", "bwd"):
        fwd = name == "fwd"
        sums = np.zeros((2 + GLA_LEVELS, c, c), np.float32)
        masks = np.zeros((GLA_LEVELS + 1, c, c), np.float32)
        i, j = t[:, None], t[None, :]
        sums[0] = (j <= i) if fwd else (j >= i)
        sums[1] = (j > i) if fwd else (j < i)
        for lvl in range(GLA_LEVELS):
            half = c >> (lvl + 1)
            mid = (t // (2 * half)) * (2 * half) + half
            upper = t >= mid
            same = (i // (2 * half)) == (j // (2 * half))
            if fwd:
                sums[2 + lvl] = np.where(upper[:, None], (j >= mid[:, None]) & (j <= i), (j > i) & (j < mid[:, None]))
                masks[lvl] = same & upper[:, None] & ~upper[None, :]
            else:
                sums[2 + lvl] = np.where(upper[:, None], (j >= mid[:, None]) & (j < i), (j >= i) & (j < mid[:, None]))
                masks[lvl] = same & ~upper[:, None] & upper[None, :]
        masks[GLA_LEVELS] = np.eye(c)
        out[name] = (jnp.asarray(sums.reshape(-1, c), BF16),
                     jnp.asarray(np.concatenate([masks, masks], axis=1).reshape(-1, c), F32))
    return out


def _gla_chunk(q, k, v, g, s_prev, sums, masks):
    c = GLA_CHUNK
    g_hi, g_lo = _split(g)
    a = _dot(sums, g_hi) + _dot(sums, g_lo)
    q_dec = q * jnp.exp(a[0:c])
    o_st = _dot(_stack_heads(q_dec).astype(BF16), s_prev.astype(BF16))
    att = jnp.zeros((2 * c, c), F32)
    for lvl in range(GLA_LEVELS + 1):
        if lvl < GLA_LEVELS:
            w = jnp.exp(a[(2 + lvl) * c:(3 + lvl) * c])
            ql, kl = q * w, k * w
        else:
            ql, kl = q, k
        p = _dot_nt(_stack_heads(ql).astype(BF16), kl.astype(BF16))
        att = att + p * masks[lvl * 2 * c:(lvl + 1) * 2 * c]
    r = _dot(att.astype(BF16), v)
    o = jnp.concatenate([o_st[0:c] + r[0:c, 0:GLA_DV], o_st[c:] + r[c:, GLA_DV:]], axis=1)
    k_dec = (k * jnp.exp(a[c:2 * c])).astype(BF16)
    u = _dot_tn(k_dec, v)
    ones = jnp.ones((c, LANES), BF16)
    total = _dot_tn(g_hi, ones) + _dot_tn(g_lo, ones)
    row_lo = lax.broadcasted_iota(jnp.int32, (LANES, GLA_DV), 0) < GLA_DK
    s_new = jnp.exp(total) * s_prev + jnp.where(row_lo, u[:, 0:GLA_DV], u[:, GLA_DV:])
    return o, s_new


def _gla_kernel(q_ref, k_ref, v_ref, rg_ref, gf_ref, gb_ref, s0f_ref, s0b_ref, gain_ref,
                sums_f_ref, masks_f_ref, sums_b_ref, masks_b_ref, _, y_ref, sf_ref, sb_ref,
                of_scr, ob_scr, st_scr):
    t = q_ref.shape[0]
    c = GLA_CHUNK
    nc = t // c
    st_scr[0] = jnp.concatenate([s0f_ref[0], s0f_ref[1]], axis=0)
    st_scr[1] = jnp.concatenate([s0b_ref[0], s0b_ref[1]], axis=0)

    def body(ci, carry):
        for d, (g_ref, o_scr, sums_ref, masks_ref) in enumerate(
                ((gf_ref, of_scr, sums_f_ref, masks_f_ref), (gb_ref, ob_scr, sums_b_ref, masks_b_ref))):
            cc = ci if d == 0 else nc - 1 - ci
            rows = pl.ds(pl.multiple_of(cc * c, c), c)
            o, s_new = _gla_chunk(q_ref[rows, :].astype(F32), k_ref[rows, :].astype(F32), v_ref[rows, :],
                                  g_ref[rows, :], st_scr[d], sums_ref[...], masks_ref[...])
            o_scr[rows, :] = o
            st_scr[d] = s_new
        return carry

    lax.fori_loop(0, nc, body, 0)

    for d, s_ref in enumerate((sf_ref, sb_ref)):
        s_ref[0] = st_scr[d, 0:GLA_DK]
        s_ref[1] = st_scr[d, GLA_DK:]

    rt = 256
    for r in range(t // rt):
        og = of_scr[r * rt:(r + 1) * rt, :] + ob_scr[r * rt:(r + 1) * rt, :]
        halves = []
        for hh in range(2):
            x = og[:, hh * GLA_DV:(hh + 1) * GLA_DV]
            halves.append(x * lax.rsqrt(jnp.mean(x * x, axis=-1, keepdims=True) + EPS))
        y = jnp.concatenate(halves, axis=1) * gain_ref[...] * _silu(rg_ref[r * rt:(r + 1) * rt, :].astype(F32))
        y_ref[r * rt:(r + 1) * rt, :] = y.astype(BF16)


def _gla_call(geom, proj, gates, s0f, s0b, gain, consts, prev, t, nb, row0, name):
    npairs = GLA_HEADS // 2
    qcol = 3 * NA_W // LANES
    kcol = qcol + GLA_KW // LANES
    vcol = (3 * NA_W + 2 * GLA_KW) // (2 * GLA_DV)
    rcol = vcol + GLA_VW // (2 * GLA_DV)
    sums_f, masks_f = consts["fwd"]
    sums_b, masks_b = consts["bwd"]
    st_spec = pl.BlockSpec((None, 2, GLA_DK, GLA_DV), lambda b, p: (b, p, 0, 0))
    in_specs = [pl.BlockSpec((t, LANES), lambda b, p: (row0 + b, qcol + p)),
                pl.BlockSpec((t, LANES), lambda b, p: (row0 + b, kcol + p)),
                pl.BlockSpec((t, 2 * GLA_DV), lambda b, p: (row0 + b, vcol + p)),
                pl.BlockSpec((t, 2 * GLA_DV), lambda b, p: (row0 + b, rcol + p)),
                pl.BlockSpec((t, LANES), lambda b, p: (row0 + b, p)),
                pl.BlockSpec((t, LANES), lambda b, p: (row0 + b, npairs + p)),
                st_spec, st_spec,
                pl.BlockSpec((1, 2 * GLA_DV), lambda b, p: (0, p)),
                _const_spec(sums_f.shape), _const_spec(masks_f.shape),
                _const_spec(sums_b.shape), _const_spec(masks_b.shape)]
    args = [proj, proj, proj, proj, gates, gates, s0f, s0b, gain, sums_f, masks_f, sums_b, masks_b]
    aliases = {}
    if prev is not None:
        in_specs.append(pl.BlockSpec(memory_space=pl.ANY))
        args.append(prev)
        aliases = {len(args) - 1: 0}
    kern = _gla_kernel if prev is not None else (lambda *r: _gla_kernel(*r[:13], None, *r[13:]))
    return pl.pallas_call(
        kern,
        out_shape=(jax.ShapeDtypeStruct((geom.n, GLA_VW), BF16),
                   jax.ShapeDtypeStruct((nb, GLA_HEADS, GLA_DK, GLA_DV), F32),
                   jax.ShapeDtypeStruct((nb, GLA_HEADS, GLA_DK, GLA_DV), F32)),
        grid=(nb, npairs),
        in_specs=in_specs,
        out_specs=(pl.BlockSpec((t, 2 * GLA_DV), lambda b, p: (row0 + b, p)), st_spec, st_spec),
        scratch_shapes=[pltpu.VMEM((t, 2 * GLA_DV), F32), pltpu.VMEM((t, 2 * GLA_DV), F32),
                        pltpu.VMEM((2, LANES, GLA_DV), F32)],
        input_output_aliases=aliases,
        compiler_params=_params(("parallel", "parallel")),
        name=name,
    )(*args)


def _out_proj_kernel(*refs, n_in):
    a_refs = refs[:n_in]
    w_ref, x_ref, gate_ref, o_ref = refs[n_in:]
    acc = None
    k0 = 0
    for a_ref in a_refs:
        kw = a_ref.shape[1]
        part = _dot(a_ref[...], w_ref[k0:k0 + kw, :])
        acc = part if acc is None else acc + part
        k0 += kw
    o_ref[...] = x_ref[...] + gate_ref[...] * acc


def _out_proj(geom, acts, w, x, mod_l, gate_k):
    in_specs = [pl.BlockSpec((TM, a.shape[1]), lambda i: (i, 0)) for a in acts]
    in_specs += [_const_spec(w.shape), pl.BlockSpec((TM, D), lambda i: (i, 0)), geom.mod_spec(gate_k)]
    return pl.pallas_call(
        functools.partial(_out_proj_kernel, n_in=len(acts)),
        out_shape=jax.ShapeDtypeStruct((geom.n, D), F32),
        grid=(geom.n // TM,),
        in_specs=in_specs,
        out_specs=pl.BlockSpec((TM, D), lambda i: (i, 0)),
        compiler_params=_params(("parallel",)),
        name="out_proj",
    )(*acts, w, x, mod_l)


C_OUT = C_QW + 4 * C_KW


def _c_in_kernel(x_ref, g_ref, sc_ref, sh_ref, w_ref, gain_ref, cos_ref, sin_ref, grp_ref, perm_ref, o_ref):
    h = _norm_mod(x_ref[...], g_ref[...], sc_ref[...], sh_ref[...]).astype(BF16)
    cw = 512
    n_norm = (C_QW + 2 * C_KW) // LANES
    for j in range(C_OUT // cw):
        y = _dot(h, w_ref[:, j * cw:(j + 1) * cw])
        for s in range(cw // LANES):
            blk = j * (cw // LANES) + s
            yb = y[:, s * LANES:(s + 1) * LANES]
            if blk < n_norm:
                sq_hi, sq_lo = _split(yb * yb)
                ms = _dot(sq_hi, grp_ref[...]) + _dot(sq_lo, grp_ref[...])
                yn = yb * lax.rsqrt(ms + EPS) * gain_ref[:, blk * LANES:(blk + 1) * LANES]
                partner = _dot(yn.astype(BF16), perm_ref[...])
                yb = yn * cos_ref[...] + partner * sin_ref[...]
            o_ref[:, blk * LANES:(blk + 1) * LANES] = yb.astype(BF16)


def _rope_tables(geom):
    tpos = np.arange(geom.ds)
    inv = ROPE_THETA ** (-np.arange(0, HD // 2, 2, dtype=np.float32) / (HD // 2))
    ang_r = (tpos // GRID_W).astype(np.float32)[:, None] * inv
    ang_c = (tpos % GRID_W).astype(np.float32)[:, None] * inv
    ang_r, ang_c = jnp.asarray(ang_r, F32), jnp.asarray(ang_c, F32)
    cos = jnp.concatenate([jnp.cos(ang_r)] * 2 + [jnp.cos(ang_c)] * 2, axis=1)
    sin = jnp.concatenate([-jnp.sin(ang_r), jnp.sin(ang_r), -jnp.sin(ang_c), jnp.sin(ang_c)], axis=1)
    cos = jnp.concatenate([jnp.tile(cos, (1, 2)), jnp.ones((TM, LANES), F32)], axis=0)
    sin = jnp.concatenate([jnp.tile(sin, (1, 2)), jnp.zeros((TM, LANES), F32)], axis=0)
    lane = np.arange(LANES)
    quarter = HD // 4
    partner = np.where(lane % (2 * quarter) < quarter, lane + quarter, lane - quarter)
    perm = np.zeros((LANES, LANES), np.float32)
    perm[partner, lane] = 1.0
    grp = (lane[:, None] // HD == lane[None, :] // HD).astype(np.float32) / HD
    return cos, sin, jnp.asarray(grp, BF16), jnp.asarray(perm, BF16)


def _c_in_proj(geom, x, mod_l, g, w, gain, tables):
    cos, sin, grp, perm = tables
    npt, tps = geom.npt, geom.tps

    def pos_map(i):
        return (jnp.where(i < npt, tps, (i - npt) % tps), 0)
    return pl.pallas_call(
        _c_in_kernel,
        out_shape=jax.ShapeDtypeStruct((geom.n, C_OUT), BF16),
        grid=(geom.n // TM,),
        in_specs=[pl.BlockSpec((TM, D), lambda i: (i, 0)),
                  _const_spec((1, D)),
                  geom.mod_spec(1), geom.mod_spec(0),
                  _const_spec(w.shape), _const_spec(gain.shape),
                  pl.BlockSpec((TM, LANES), pos_map), pl.BlockSpec((TM, LANES), pos_map),
                  _const_spec(grp.shape), _const_spec(perm.shape)],
        out_specs=pl.BlockSpec((TM, C_OUT), lambda i: (i, 0)),
        compiler_params=_params(("parallel",)),
        name="gqa_in_proj",
    )(x, g, mod_l, mod_l, w, gain, cos, sin, grp, perm)


def _gqa_attention(geom, proj, k_ctx, v_ctx):
    nqp = GQA_HEADS // 2
    kcol = C_QW // LANES
    vcol = kcol + 2 * C_KW // LANES
    s, ds = geom.s, geom.ds
    o = pl.pallas_call(
        functools.partial(_pair_attn_kernel, tk=s, has_ctx=False),
        out_shape=jax.ShapeDtypeStruct((geom.n, C_QW), BF16),
        grid=(geom.b, nqp),
        in_specs=[pl.BlockSpec((s, LANES), lambda b, m: (b, m)),
                  pl.BlockSpec((s, LANES), lambda b, m: (b, kcol + m // 2)),
                  pl.BlockSpec((s, LANES), lambda b, m: (b, vcol + m // 2))],
        out_specs=pl.BlockSpec((s, LANES), lambda b, m: (b, m)),
        compiler_params=_params(("parallel", "parallel")),
        name="gqa_prompt_attn",
    )(proj, proj, proj)

    tq = 512
    nqt = ds // tq
    q0 = geom.np_rows // tq
    s0 = geom.np_rows // ds
    kc = jnp.concatenate([k_ctx, k_ctx], axis=-1).astype(BF16)
    vc = jnp.concatenate([v_ctx, v_ctx], axis=-1).astype(BF16)
    nctx = kc.shape[2]
    return pl.pallas_call(
        functools.partial(_pair_attn_kernel, tk=512, has_ctx=True),
        out_shape=jax.ShapeDtypeStruct((geom.n, C_QW), BF16),
        grid=(geom.db, nqp, nqt),
        in_specs=[pl.BlockSpec((tq, LANES), lambda b, m, t: (q0 + nqt * b + t, m)),
                  pl.BlockSpec((ds, LANES), lambda b, m, t: (s0 + b, kcol + m // 2)),
                  pl.BlockSpec((ds, LANES), lambda b, m, t: (s0 + b, vcol + m // 2)),
                  pl.BlockSpec((None, None, nctx, LANES), lambda b, m, t: (b, m // 2, 0, 0)),
                  pl.BlockSpec((None, None, nctx, LANES), lambda b, m, t: (b, m // 2, 0, 0)),
                  pl.BlockSpec(memory_space=pl.ANY)],
        out_specs=pl.BlockSpec((tq, LANES), lambda b, m, t: (q0 + nqt * b + t, m)),
        input_output_aliases={5: 0},
        compiler_params=_params(("parallel", "parallel", "parallel")),
        name="gqa_sample_attn",
    )(proj, proj, proj, kc, vc, o)


def _ffn_kernel(x_ref, xp_ref, xn_ref, g_ref, sc_ref, sh_ref, gate_ref, wup_ref, cw_ref, cb_ref, wdn_ref,
                o_ref, acc_ref, *, npt, tps):
    i = pl.program_id(0)
    j = jnp.maximum(i - npt, 0) % tps
    is_prompt = i < npt
    has_prev = jnp.logical_and(jnp.logical_not(is_prompt), j > 0).astype(F32)
    has_next = jnp.logical_and(jnp.logical_not(is_prompt), j < tps - 1).astype(F32)
    g, sc, sh = g_ref[...], sc_ref[...], sh_ref[...]
    x = x_ref[...]
    h_ext = jnp.concatenate([_norm_mod(xp_ref[...], g, sc, sh) * has_prev,
                             _norm_mod(x, g, sc, sh),
                             _norm_mod(xn_ref[...], g, sc, sh) * has_next], axis=0).astype(BF16)
    rows = TM + 2 * HALO
    for c in range(D_FF // FF_CHUNK):
        u = _dot(h_ext, wup_ref[c])
        u_prev = pltpu.roll(u, 1, 0)[HALO:HALO + TM]
        u_next = pltpu.roll(u, rows - 1, 0)[HALO:HALO + TM]
        uc = cw_ref[0, c] * u_prev + cw_ref[1, c] * u[HALO:HALO + TM] + cw_ref[2, c] * u_next + cb_ref[c]
        a = (_silu(uc[:, FF_CHUNK:]) * uc[:, :FF_CHUNK]).astype(BF16)
        part = _dot(a, wdn_ref[c])
        if c == 0:
            acc_ref[...] = part
        else:
            acc_ref[...] += part
    o_ref[...] = x + gate_ref[...] * acc_ref[...]


def _conv_ffn(geom, x, mod_l, g, wup, cw, cb, wdn):
    nblk = geom.n // HALO
    per = TM // HALO
    return pl.pallas_call(
        functools.partial(_ffn_kernel, npt=geom.npt, tps=geom.tps),
        out_shape=jax.ShapeDtypeStruct((geom.n, D), F32),
        grid=(geom.n // TM,),
        in_specs=[pl.BlockSpec((TM, D), lambda i: (i, 0)),
                  pl.BlockSpec((HALO, D), lambda i: (jnp.maximum(i * per - 1, 0), 0)),
                  pl.BlockSpec((HALO, D), lambda i: (jnp.minimum((i + 1) * per, nblk - 1), 0)),
                  _const_spec((1, D)),
                  geom.mod_spec(4), geom.mod_spec(3), geom.mod_spec(5),
                  _const_spec(wup.shape), _const_spec(cw.shape), _const_spec(cb.shape), _const_spec(wdn.shape)],
        out_specs=pl.BlockSpec((TM, D), lambda i: (i, 0)),
        scratch_shapes=[pltpu.VMEM((TM, D), F32)],
        compiler_params=_params(("parallel",)),
        name="conv_ffn",
    )(x, x, x, g, mod_l, mod_l, mod_l, wup, cw, cb, wdn)


def _final_norm_kernel(x_ref, g_ref, o_ref):
    x = x_ref[...]
    o_ref[...] = x * lax.rsqrt(jnp.mean(x * x, axis=-1, keepdims=True) + EPS) * g_ref[...]


def _final_norm(x, g, row0, nrows):
    t0 = row0 // TM
    return pl.pallas_call(
        _final_norm_kernel,
        out_shape=jax.ShapeDtypeStruct((nrows, D), F32),
        grid=(nrows // TM,),
        in_specs=[pl.BlockSpec((TM, D), lambda i: (t0 + i, 0)), _const_spec((1, D))],
        out_specs=pl.BlockSpec((TM, D), lambda i: (i, 0)),
        compiler_params=_params(("parallel",)),
        name="final_norm",
    )(x, g)


def _prep_ab(w_in, w2_f, b2_f, w2_b, b2_b):
    scale = np.ones((w_in.shape[1],), np.float32)
    scale[0:NA_W] = HD ** -0.5
    scale[3 * NA_W:3 * NA_W + GLA_KW] = GLA_DK ** -0.5
    w = jnp.pad(w_in * scale, ((0, 0), (0, AB_MAIN + LANES - w_in.shape[1]))).astype(BF16)
    w2 = jnp.zeros((LANES, 2 * GLA_KW), F32)
    w2 = w2.at[0:GLA_RANK, 0:GLA_KW].set(w2_f).at[GLA_RANK:2 * GLA_RANK, GLA_KW:].set(w2_b)
    w2_hi, w2_lo = _split(w2)
    b2 = jnp.concatenate([b2_f, b2_b])[None, :]
    return w, w2_hi, w2_lo, b2


def _prep_c(w_in, qn_g, kn_g):
    wq, wk, wv = w_in[:, :C_QW], w_in[:, C_QW:C_QW + C_KW], w_in[:, C_QW + C_KW:]

    def dup(w):
        return jnp.broadcast_to(w.reshape(D, GQA_KVH, 1, HD), (D, GQA_KVH, 2, HD)).reshape(D, 2 * C_KW)
    w = jnp.concatenate([wq, dup(wk), dup(wv)], axis=1).astype(BF16)
    gain = jnp.concatenate([jnp.tile(qn_g * (HD ** -0.5), GQA_HEADS), jnp.tile(kn_g, 2 * GQA_KVH)])[None, :]
    return w, gain


def _prep_ffn(w_up, conv_w, conv_b, w_down):
    nch = D_FF // FF_CHUNK

    def pair(a):
        lead = a.shape[:-1]
        return a.reshape(*lead, 2, nch, FF_CHUNK).swapaxes(-3, -2).reshape(*lead, nch, 2 * FF_CHUNK)
    wup = pair(w_up).swapaxes(0, 1).astype(BF16)
    cw = pair(conv_w)[:, :, None, :]
    cb = pair(conv_b)[:, None, :]
    wdn = w_down.reshape(nch, FF_CHUNK, D).astype(BF16)
    return wup, cw, cb, wdn


def kernel(x_prompt, x_sample, cache_na_k, cache_na_v, state_gla_fwd, state_gla_bwd, cache_gqa_k, cache_gqa_v, c, c_ctx, ada_w, ada_b, norm_mix_g, norm_ffn_g, ab_w_in, ab_w_out, na_rpb, gla_w2_fwd, gla_b2_fwd, gla_w2_bwd, gla_b2_bwd, gla_norm_g, gqa_w_in, gqa_w_out, gqa_q_norm_g, gqa_k_norm_g, ffn_w_up, ffn_conv_w, ffn_conv_b, ffn_w_down, final_norm_g):
    b, s, _ = x_prompt.shape
    db, ds, _ = x_sample.shape
    geom = _Geom(b, s, db, ds)
    assert ds % GRID_W == 0 and geom.np_rows % ds == 0 and geom.np_rows % 512 == 0

    x = jnp.concatenate([x_prompt.reshape(-1, D), x_sample.reshape(-1, D)], axis=0)
    nrow = -(-(db + 1) // 8) * 8
    c_rows = jnp.zeros((nrow, D), F32).at[:db].set(c).at[db].set(c_ctx)
    mod = _modulation(c_rows, ada_w, ada_b).reshape(DEPTH, nrow, 1, 6 * D)

    gla_consts = _gla_constants()
    rope = _rope_tables(geom)
    zeros_state = jnp.zeros((b, GLA_HEADS, GLA_DK, GLA_DV), F32)
    na_k, na_v, gla_f, gla_b, gq_k, gq_v = [], [], [], [], [], []

    for i in range(DEPTH):
        j = i // 2
        mod_l = mod[i]
        if i % 2 == 0:
            w, w2_hi, w2_lo, b2 = _prep_ab(ab_w_in[j], gla_w2_fwd[j], gla_b2_fwd[j], gla_w2_bwd[j], gla_b2_bwd[j])
            proj, gates = _ab_in_proj(geom, x, mod_l, norm_mix_g[i][None, :], w, w2_hi, w2_lo, b2)
            oa = _na_attention(geom, proj, cache_na_k[:, j], cache_na_v[:, j], na_rpb[j])
            gain = gla_norm_g[j][None, :]
            yg, sf, sb = _gla_call(geom, proj, gates, zeros_state, zeros_state, gain, gla_consts, None,
                                   s, b, 0, "gla_prompt")
            yg, _, _ = _gla_call(geom, proj, gates, state_gla_fwd[:, j], state_gla_bwd[:, j], gain, gla_consts, yg,
                                 ds, db, geom.np_rows // ds, "gla_sample")
            x = _out_proj(geom, [oa, yg], ab_w_out[j].astype(BF16), x, mod_l, 2)
            pk = proj[:geom.np_rows].reshape(b, s, -1)
            na_k.append(pk[..., NA_W:2 * NA_W].reshape(b, s, NA_HEADS, HD).transpose(0, 2, 1, 3).astype(F32))
            na_v.append(pk[..., 2 * NA_W:3 * NA_W].reshape(b, s, NA_HEADS, HD).transpose(0, 2, 1, 3).astype(F32))
            gla_f.append(sf)
            gla_b.append(sb)
        else:
            w, gain = _prep_c(gqa_w_in[j], gqa_q_norm_g[j], gqa_k_norm_g[j])
            proj = _c_in_proj(geom, x, mod_l, norm_mix_g[i][None, :], w, gain, rope)
            o = _gqa_attention(geom, proj, cache_gqa_k[:, j], cache_gqa_v[:, j])
            x = _out_proj(geom, [o], gqa_w_out[j].astype(BF16), x, mod_l, 2)
            pk = proj[:geom.np_rows].reshape(b, s, -1)
            kd = pk[..., C_QW:C_QW + 2 * C_KW].reshape(b, s, GQA_KVH, 2, HD)[:, :, :, 0]
            vd = pk[..., C_QW + 2 * C_KW:].reshape(b, s, GQA_KVH, 2, HD)[:, :, :, 0]
            gq_k.append(kd.transpose(0, 2, 1, 3).astype(F32))
            gq_v.append(vd.transpose(0, 2, 1, 3).astype(F32))
        wup, cw, cb, wdn = _prep_ffn(ffn_w_up[i], ffn_conv_w[i], ffn_conv_b[i], ffn_w_down[i])
        x = _conv_ffn(geom, x, mod_l, norm_ffn_g[i][None, :], wup, cw, cb, wdn)

    y_prompt = _final_norm(x, final_norm_g[None, :], 0, geom.np_rows).reshape(b, s, D)
    y_sample = _final_norm(x, final_norm_g[None, :], geom.np_rows, geom.ns_rows).reshape(db, ds, D)
    return (y_prompt, y_sample,
            jnp.stack(na_k, axis=1), jnp.stack(na_v, axis=1),
            jnp.stack(gla_f, axis=1), jnp.stack(gla_b, axis=1),
            jnp.stack(gq_k, axis=1), jnp.stack(gq_v, axis=1))
```

```python
import functools

import numpy as np
import jax
import jax.numpy as jnp
from jax import lax
from jax.experimental import pallas as pl
from jax.experimental.pallas import tpu as pltpu

F32 = jnp.float32
BF16 = jnp.bfloat16

D = 1024
DEPTH = 4
HD = 64
GRID_W = 64
EPS = 1e-6
NA_HEADS = 8
NA_W = NA_HEADS * HD
NA_KR = 8
NA_KC = 16
GLA_HEADS = 4
GLA_DK = 64
GLA_DV = 128
GLA_KW = GLA_HEADS * GLA_DK
GLA_VW = GLA_HEADS * GLA_DV
GLA_RANK = 16
GLA_CHUNK = 64
GQA_HEADS = 16
GQA_KVH = 4
C_QW = GQA_HEADS * HD
C_KW = GQA_KVH * HD
ROPE_THETA = 10000.0
D_FF = 2816
AB_MAIN = 3 * NA_W + 2 * GLA_KW + 2 * GLA_VW

LANES = 128
TM = 256
TP = 512
LOG2E = 1.4426950408889634
HALO = 16
FF_CHUNK = 256
NA_QROWS = 8
NA_KROWS = NA_QROWS + NA_KR - 1
VMEM_LIMIT = 56 * 1024 * 1024


def _dot(a, b):
    return jnp.dot(a, b, preferred_element_type=F32)


def _dot_nt(a, b):
    return lax.dot_general(a, b, (((1,), (1,)), ((), ())), preferred_element_type=F32)


def _dot_tn(a, b):
    return lax.dot_general(a, b, (((0,), (0,)), ((), ())), preferred_element_type=F32)


def _split(a):
    hi = a.astype(BF16)
    lo = (a - hi.astype(F32)).astype(BF16)
    return hi, lo


def _dot_split(a, w_hi, w_lo):
    a_hi, a_lo = _split(a)
    return _dot(a_hi, w_hi) + _dot(a_lo, w_hi) + _dot(a_hi, w_lo)


def _norm_mod(x, g, sc, sh):
    xn = x * lax.rsqrt(jnp.mean(x * x, axis=-1, keepdims=True) + EPS)
    return (xn * g) * (1.0 + sc) + sh


def _silu(x):
    return x / (1.0 + jnp.exp(-x))


def _lane_lo(shape=(1, LANES)):
    return lax.broadcasted_iota(jnp.int32, shape, len(shape) - 1) < HD


def _stack_heads(x):
    lo = _lane_lo()
    zero = jnp.zeros_like(x)
    return jnp.concatenate([jnp.where(lo, x, zero), jnp.where(lo, zero, x)], axis=0)


def _params(sem):
    return pltpu.CompilerParams(dimension_semantics=sem, vmem_limit_bytes=VMEM_LIMIT)


def _mod_kernel(c_ref, w_ref, b_ref, o_ref):
    a = _silu(c_ref[...])
    w = w_ref[...]
    w_hi, w_lo = _split(w)
    o_ref[...] = _dot_split(a, w_hi, w_lo) + b_ref[...]


def _modulation(c_rows, ada_w, ada_b):
    nrow = c_rows.shape[0]
    tn = 1024
    return pl.pallas_call(
        _mod_kernel,
        out_shape=jax.ShapeDtypeStruct((DEPTH, nrow, 6 * D), F32),
        grid=(DEPTH, 6 * D // tn),
        in_specs=[pl.BlockSpec((nrow, D), lambda l, n: (0, 0)),
                  pl.BlockSpec((None, D, tn), lambda l, n: (l, 0, n)),
                  pl.BlockSpec((None, 1, tn), lambda l, n: (l, 0, n))],
        out_specs=pl.BlockSpec((None, nrow, tn), lambda l, n: (l, 0, n)),
        compiler_params=_params(("parallel", "parallel")),
        name="modulation",
    )(c_rows, ada_w, ada_b.reshape(DEPTH, 1, 6 * D))


class _Geom:
    def __init__(self, b, s, db, ds):
        assert s == TM and ds % TM == 0
        self.b, self.s, self.db, self.ds = b, s, db, ds
        self.np_rows = b * s
        self.ns_rows = db * ds
        self.n = self.np_rows + self.ns_rows
        self.npt = self.np_rows // TM
        self.tps = ds // TM
        self.ctx_row = db

    def mod_spec(self, k, tm=TM):
        assert self.np_rows % tm == 0 and self.ds % tm == 0
        npt, tps, ctx = self.np_rows // tm, self.ds // tm, self.ctx_row

        def imap(i):
            return (jnp.where(i < npt, ctx, (i - npt) // tps), 0, k)
        return pl.BlockSpec((None, 1, D), imap)


def _const_spec(shape):
    nd = len(shape)
    return pl.BlockSpec(shape, lambda *_: (0,) * nd)


def _ab_in_kernel(x_ref, g_ref, sc_ref, sh_ref, w_ref, w2hi_ref, w2lo_ref, b2_ref, o_ref, gate_ref):
    h = _norm_mod(x_ref[...], g_ref[...], sc_ref[...], sh_ref[...]).astype(BF16)
    cw = 512
    lr = _dot(h, w_ref[:, AB_MAIN:AB_MAIN + LANES])
    o_ref[:, 0:cw] = _dot(h, w_ref[:, 0:cw]).astype(BF16)
    z = _dot_split(lr, w2hi_ref[...], w2lo_ref[...]) + b2_ref[...]
    for j in range(1, AB_MAIN // cw):
        o_ref[:, j * cw:(j + 1) * cw] = _dot(h, w_ref[:, j * cw:(j + 1) * cw]).astype(BF16)
    gate_ref[...] = (jnp.minimum(z, 0.0) - jnp.log(1.0 + jnp.exp(-jnp.abs(z)))) * (1.0 / 16.0)


def _ab_in_proj(geom, x, mod_l, g, w, w2hi, w2lo, b2):
    return pl.pallas_call(
        _ab_in_kernel,
        out_shape=(jax.ShapeDtypeStruct((geom.n, AB_MAIN), BF16),
                   jax.ShapeDtypeStruct((geom.n, 2 * GLA_KW), F32)),
        grid=(geom.n // TP,),
        in_specs=[pl.BlockSpec((TP, D), lambda i: (i, 0)),
                  _const_spec((1, D)),
                  geom.mod_spec(1, TP), geom.mod_spec(0, TP),
                  _const_spec(w.shape), _const_spec(w2hi.shape), _const_spec(w2lo.shape),
                  _const_spec(b2.shape)],
        out_specs=(pl.BlockSpec((TP, AB_MAIN), lambda i: (i, 0)),
                   pl.BlockSpec((TP, 2 * GLA_KW), lambda i: (i, 0))),
        compiler_params=_params(("parallel",)),
        name="ab_in_proj",
    )(x, g, mod_l, mod_l, w, w2hi, w2lo, b2)


def _pair_attn_kernel(*refs, has_ctx, row_split):
    if has_ctx:
        q_ref, k_ref, v_ref, kc_ref, vc_ref, _, o_ref = refs
        kv = ((k_ref, v_ref), (kc_ref, vc_ref))
    else:
        q_ref, k_ref, v_ref, o_ref = refs
        kv = ((k_ref, v_ref),)
    q = q_ref[...]
    lo = _lane_lo()
    zero = jnp.zeros_like(q)
    q_heads = (jnp.where(lo, q, zero), jnp.where(lo, zero, q))
    rows = q.shape[0] // row_split
    blocks = [(hh, r) for r in range(row_split) for hh in range(2)]

    def scores(b):
        hh, r = blocks[b]
        qb = q_heads[hh][r * rows:(r + 1) * rows]
        return [_dot_nt(qb, kr[...]) for kr, _ in kv]

    def weighted_values(b, p_parts, l):
        outs[blocks[b]] = sum(_dot(p, vr[...]) for p, (_, vr) in zip(p_parts, kv)) / l

    outs = {}
    s_ahead = scores(0)
    pending = None
    for b in range(len(blocks)):
        s_parts = s_ahead
        if b + 1 < len(blocks):
            s_ahead = scores(b + 1)
        if pending is not None:
            weighted_values(b - 1, *pending)
        m = functools.reduce(jnp.maximum, [jnp.max(s, axis=-1, keepdims=True) for s in s_parts])
        p_parts = [jnp.exp2(s - m) for s in s_parts]
        l = sum(jnp.sum(p, axis=-1, keepdims=True) for p in p_parts)
        pending = ([p.astype(BF16) for p in p_parts], l)
    weighted_values(len(blocks) - 1, *pending)
    for r in range(row_split):
        o_ref[r * rows:(r + 1) * rows, :] = jnp.where(lo, outs[(0, r)], outs[(1, r)]).astype(BF16)


def _na_kernel(q_ref, k_ref, v_ref, kc_ref, vc_ref, bias_ref, _, o_ref):
    rb = pl.program_id(1)
    nq = q_ref.shape[0]
    nk = NA_KROWS * GRID_W
    last = k_ref.shape[0] // GRID_W - NA_KROWS
    k0 = pl.multiple_of(jnp.clip(NA_QROWS * rb - NA_KR // 2, 0, last) * GRID_W, GRID_W)
    q = q_ref[...]
    lo = _lane_lo()
    zero = jnp.zeros_like(q)
    q_heads = (jnp.where(lo, q, zero), jnp.where(lo, zero, q))
    kl = k_ref[pl.ds(k0, nk), :]
    vl = v_ref[pl.ds(k0, nk), :]
    scores = [(_dot_nt(qh, kl) + bias_ref[hh * nq:(hh + 1) * nq, :], _dot_nt(qh, kc_ref[...]))
              for hh, qh in enumerate(q_heads)]
    outs = []
    for s_loc, s_ctx in scores:
        m = jnp.maximum(jnp.max(s_loc, axis=-1, keepdims=True), jnp.max(s_ctx, axis=-1, keepdims=True))
        p_loc = jnp.exp2(s_loc - m)
        p_ctx = jnp.exp2(s_ctx - m)
        l = jnp.sum(p_loc, axis=-1, keepdims=True) + jnp.sum(p_ctx, axis=-1, keepdims=True)
        outs.append((_dot(p_loc.astype(BF16), vl) + _dot(p_ctx.astype(BF16), vc_ref[...])) / l)
    o_ref[...] = jnp.where(lo, outs[0], outs[1]).astype(BF16)


def _na_bias_tiles(rpb, rows):
    col = np.arange(GRID_W)
    cs = np.clip(col - NA_KC // 2, 0, GRID_W - NA_KC)
    col_ok = (col[None, :] >= cs[:, None]) & (col[None, :] < cs[:, None] + NA_KC)
    dc = np.clip(col[None, :] - col[:, None] + NA_KC - 1, 0, 2 * NA_KC - 2)
    by_col = jnp.where(col_ok, rpb.astype(F32)[:, :, dc] * LOG2E, -jnp.inf)
    masked = jnp.full((NA_HEADS, GRID_W, GRID_W), -jnp.inf, F32)
    tiles = []
    for r0 in (0, NA_QROWS, rows - NA_QROWS):
        kstart = int(np.clip(r0 - NA_KR // 2, 0, rows - NA_KROWS))
        strips = []
        for r in range(r0, r0 + NA_QROWS):
            rs = int(np.clip(r - NA_KR // 2, 0, rows - NA_KR))
            blocks = [by_col[:, kr - r + NA_KR - 1] if rs <= kr < rs + NA_KR else masked
                      for kr in range(kstart, kstart + NA_KROWS)]
            strips.append(jnp.concatenate(blocks, axis=2))
        tile = jnp.concatenate(strips, axis=1)
        tiles.append(tile.reshape(NA_HEADS // 2, 2 * NA_QROWS * GRID_W, NA_KROWS * GRID_W))
    return jnp.stack(tiles, axis=1)


def _na_attention(geom, proj, k_ctx, v_ctx, rpb):
    npairs = NA_HEADS // 2
    s, ds = geom.s, geom.ds
    oa = pl.pallas_call(
        functools.partial(_pair_attn_kernel, has_ctx=False, row_split=1),
        out_shape=jax.ShapeDtypeStruct((geom.n, NA_W), BF16),
        grid=(geom.b, npairs),
        in_specs=[pl.BlockSpec((s, LANES), lambda b, p: (b, p)),
                  pl.BlockSpec((s, LANES), lambda b, p: (b, npairs + p)),
                  pl.BlockSpec((s, LANES), lambda b, p: (b, 2 * npairs + p))],
        out_specs=pl.BlockSpec((s, LANES), lambda b, p: (b, p)),
        compiler_params=_params(("parallel", "parallel")),
        name="na_prompt_attn",
    )(proj, proj, proj)

    rows = ds // GRID_W
    assert rows % NA_QROWS == 0 and rows >= NA_KROWS
    nrb = rows // NA_QROWS
    nq = NA_QROWS * GRID_W
    nk = NA_KROWS * GRID_W
    bias = _na_bias_tiles(rpb, rows)
    q0 = geom.np_rows // nq
    s0 = geom.np_rows // ds
    kc = k_ctx.reshape(geom.db, npairs, 2, -1, HD).transpose(0, 1, 3, 2, 4).reshape(geom.db, npairs, -1, LANES).astype(BF16)
    vc = v_ctx.reshape(geom.db, npairs, 2, -1, HD).transpose(0, 1, 3, 2, 4).reshape(geom.db, npairs, -1, LANES).astype(BF16)
    nctx = kc.shape[2]
    return pl.pallas_call(
        _na_kernel,
        out_shape=jax.ShapeDtypeStruct((geom.n, NA_W), BF16),
        grid=(npairs, nrb, geom.db),
        in_specs=[pl.BlockSpec((nq, LANES), lambda p, r, b: (q0 + nrb * b + r, p)),
                  pl.BlockSpec((ds, LANES), lambda p, r, b: (s0 + b, npairs + p)),
                  pl.BlockSpec((ds, LANES), lambda p, r, b: (s0 + b, 2 * npairs + p)),
                  pl.BlockSpec((None, None, nctx, LANES), lambda p, r, b: (b, p, 0, 0)),
                  pl.BlockSpec((None, None, nctx, LANES), lambda p, r, b: (b, p, 0, 0)),
                  pl.BlockSpec((None, None, 2 * nq, nk),
                               lambda p, r, b: (p, jnp.minimum(r, 1) + r // (nrb - 1), 0, 0)),
                  pl.BlockSpec(memory_space=pl.ANY)],
        out_specs=pl.BlockSpec((nq, LANES), lambda p, r, b: (q0 + nrb * b + r, p)),
        input_output_aliases={6: 0},
        compiler_params=_params(("parallel", "parallel", "parallel")),
        name="na_sample_attn",
    )(proj, proj, proj, kc, vc, bias, oa)


GLA_LEVELS = 6


def _gla_constants():
    c = GLA_CHUNK
    t = np.arange(c)
    out = {}
    for name in ("fwd", "bwd"):
        fwd = name == "fwd"
        sums = np.zeros((2 + GLA_LEVELS, c, c), np.float32)
        masks = np.zeros((GLA_LEVELS + 1, c, c), np.float32)
        i, j = t[:, None], t[None, :]
        sums[0] = (j <= i) if fwd else (j >= i)
        sums[1] = (j > i) if fwd else (j < i)
        for lvl in range(GLA_LEVELS):
            half = c >> (lvl + 1)
            mid = (t // (2 * half)) * (2 * half) + half
            upper = t >= mid
            same = (i // (2 * half)) == (j // (2 * half))
            if fwd:
                sums[2 + lvl] = np.where(upper[:, None], (j >= mid[:, None]) & (j <= i), (j > i) & (j < mid[:, None]))
                masks[lvl] = same & upper[:, None] & ~upper[None, :]
            else:
                sums[2 + lvl] = np.where(upper[:, None], (j >= mid[:, None]) & (j < i), (j >= i) & (j < mid[:, None]))
                masks[lvl] = same & ~upper[:, None] & upper[None, :]
        masks[GLA_LEVELS] = np.eye(c)
        out[name] = (jnp.asarray(sums.reshape(-1, c), BF16),
                     jnp.asarray(np.concatenate([masks, masks], axis=1).reshape(-1, c), F32))
    return out


def _gla_chunk(q, k, v, g, s_prev, sums, masks):
    c = GLA_CHUNK
    g_hi, g_lo = _split(g)
    a = _dot(sums, g_hi) + _dot(sums, g_lo)
    q_dec = q * jnp.exp(a[0:c])
    o_st = _dot(_stack_heads(q_dec).astype(BF16), s_prev.astype(BF16))
    att = jnp.zeros((2 * c, c), F32)
    for lvl in range(GLA_LEVELS + 1):
        if lvl < GLA_LEVELS:
            w = jnp.exp(a[(2 + lvl) * c:(3 + lvl) * c])
            ql, kl = q * w, k * w
        else:
            ql, kl = q, k
        p = _dot_nt(_stack_heads(ql).astype(BF16), kl.astype(BF16))
        att = att + p * masks[lvl * 2 * c:(lvl + 1) * 2 * c]
    r = _dot(att.astype(BF16), v)
    o = jnp.concatenate([o_st[0:c] + r[0:c, 0:GLA_DV], o_st[c:] + r[c:, GLA_DV:]], axis=1)
    k_dec = (k * jnp.exp(a[c:2 * c])).astype(BF16)
    u = _dot_tn(k_dec, v)
    ones = jnp.ones((c, LANES), BF16)
    total = _dot_tn(g_hi, ones) + _dot_tn(g_lo, ones)
    row_lo = lax.broadcasted_iota(jnp.int32, (LANES, GLA_DV), 0) < GLA_DK
    s_new = jnp.exp(total) * s_prev + jnp.where(row_lo, u[:, 0:GLA_DV], u[:, GLA_DV:])
    return o, s_new


def _gla_kernel(q_ref, k_ref, v_ref, rg_ref, gf_ref, gb_ref, s0f_ref, s0b_ref, gain_ref,
                sums_f_ref, masks_f_ref, sums_b_ref, masks_b_ref, _, y_ref, sf_ref, sb_ref,
                of_scr, ob_scr, st_scr):
    t = q_ref.shape[0]
    c = GLA_CHUNK
    nc = t // c
    st_scr[0] = jnp.concatenate([s0f_ref[0], s0f_ref[1]], axis=0)
    st_scr[1] = jnp.concatenate([s0b_ref[0], s0b_ref[1]], axis=0)

    def body(ci, carry):
        for d, (g_ref, o_scr, sums_ref, masks_ref) in enumerate(
                ((gf_ref, of_scr, sums_f_ref, masks_f_ref), (gb_ref, ob_scr, sums_b_ref, masks_b_ref))):
            cc = ci if d == 0 else nc - 1 - ci
            rows = pl.ds(pl.multiple_of(cc * c, c), c)
            o, s_new = _gla_chunk(q_ref[rows, :].astype(F32), k_ref[rows, :].astype(F32), v_ref[rows, :],
                                  g_ref[rows, :], st_scr[d], sums_ref[...], masks_ref[...])
            o_scr[rows, :] = o
            st_scr[d] = s_new
        return carry

    lax.fori_loop(0, nc, body, 0, unroll=4)

    for d, s_ref in enumerate((sf_ref, sb_ref)):
        s_ref[0] = st_scr[d, 0:GLA_DK]
        s_ref[1] = st_scr[d, GLA_DK:]

    rt = 256
    for r in range(t // rt):
        og = of_scr[r * rt:(r + 1) * rt, :] + ob_scr[r * rt:(r + 1) * rt, :]
        halves = []
        for hh in range(2):
            x = og[:, hh * GLA_DV:(hh + 1) * GLA_DV]
            halves.append(x * lax.rsqrt(jnp.mean(x * x, axis=-1, keepdims=True) + EPS))
        y = jnp.concatenate(halves, axis=1) * gain_ref[...] * _silu(rg_ref[r * rt:(r + 1) * rt, :].astype(F32))
        y_ref[r * rt:(r + 1) * rt, :] = y.astype(BF16)


def _gla_call(geom, proj, gates, s0f, s0b, gain, consts, prev, t, nb, row0, name):
    npairs = GLA_HEADS // 2
    qcol = 3 * NA_W // LANES
    kcol = qcol + GLA_KW // LANES
    vcol = (3 * NA_W + 2 * GLA_KW) // (2 * GLA_DV)
    rcol = vcol + GLA_VW // (2 * GLA_DV)
    sums_f, masks_f = consts["fwd"]
    sums_b, masks_b = consts["bwd"]
    st_spec = pl.BlockSpec((None, 2, GLA_DK, GLA_DV), lambda b, p: (b, p, 0, 0))
    in_specs = [pl.BlockSpec((t, LANES), lambda b, p: (row0 + b, qcol + p)),
                pl.BlockSpec((t, LANES), lambda b, p: (row0 + b, kcol + p)),
                pl.BlockSpec((t, 2 * GLA_DV), lambda b, p: (row0 + b, vcol + p)),
                pl.BlockSpec((t, 2 * GLA_DV), lambda b, p: (row0 + b, rcol + p)),
                pl.BlockSpec((t, LANES), lambda b, p: (row0 + b, p)),
                pl.BlockSpec((t, LANES), lambda b, p: (row0 + b, npairs + p)),
                st_spec, st_spec,
                pl.BlockSpec((1, 2 * GLA_DV), lambda b, p: (0, p)),
                _const_spec(sums_f.shape), _const_spec(masks_f.shape),
                _const_spec(sums_b.shape), _const_spec(masks_b.shape)]
    args = [proj, proj, proj, proj, gates, gates, s0f, s0b, gain, sums_f, masks_f, sums_b, masks_b]
    aliases = {}
    if prev is not None:
        in_specs.append(pl.BlockSpec(memory_space=pl.ANY))
        args.append(prev)
        aliases = {len(args) - 1: 0}
    kern = _gla_kernel if prev is not None else (lambda *r: _gla_kernel(*r[:13], None, *r[13:]))
    return pl.pallas_call(
        kern,
        out_shape=(jax.ShapeDtypeStruct((geom.n, GLA_VW), BF16),
                   jax.ShapeDtypeStruct((nb, GLA_HEADS, GLA_DK, GLA_DV), F32),
                   jax.ShapeDtypeStruct((nb, GLA_HEADS, GLA_DK, GLA_DV), F32)),
        grid=(nb, npairs),
        in_specs=in_specs,
        out_specs=(pl.BlockSpec((t, 2 * GLA_DV), lambda b, p: (row0 + b, p)), st_spec, st_spec),
        scratch_shapes=[pltpu.VMEM((t, 2 * GLA_DV), F32), pltpu.VMEM((t, 2 * GLA_DV), F32),
                        pltpu.VMEM((2, LANES, GLA_DV), F32)],
        input_output_aliases=aliases,
        compiler_params=_params(("parallel", "parallel")),
        name=name,
    )(*args)


def _out_proj_kernel(*refs, n_in):
    a_refs = refs[:n_in]
    w_ref, x_ref, gate_ref, o_ref = refs[n_in:]
    acc = None
    k0 = 0
    for a_ref in a_refs:
        kw = a_ref.shape[1]
        part = _dot(a_ref[...], w_ref[k0:k0 + kw, :])
        acc = part if acc is None else acc + part
        k0 += kw
    o_ref[...] = x_ref[...] + gate_ref[...] * acc


def _out_proj(geom, acts, w, x, mod_l, gate_k):
    in_specs = [pl.BlockSpec((TP, a.shape[1]), lambda i: (i, 0)) for a in acts]
    in_specs += [_const_spec(w.shape), pl.BlockSpec((TP, D), lambda i: (i, 0)), geom.mod_spec(gate_k, TP)]
    return pl.pallas_call(
        functools.partial(_out_proj_kernel, n_in=len(acts)),
        out_shape=jax.ShapeDtypeStruct((geom.n, D), F32),
        grid=(geom.n // TP,),
        in_specs=in_specs,
        out_specs=pl.BlockSpec((TP, D), lambda i: (i, 0)),
        compiler_params=_params(("parallel",)),
        name="out_proj",
    )(*acts, w, x, mod_l)


C_OUT = C_QW + 4 * C_KW


def _c_in_kernel(x_ref, g_ref, sc_ref, sh_ref, w_ref, gain_ref, cos_ref, sin_ref, grp_ref, o_ref):
    h = _norm_mod(x_ref[...], g_ref[...], sc_ref[...], sh_ref[...]).astype(BF16)
    cw = 512
    nch = C_OUT // cw
    n_norm = (C_QW + 2 * C_KW) // LANES
    quarter = HD // 4
    first_half = lax.broadcasted_iota(jnp.int32, (1, LANES), 1) % (2 * quarter) < quarter
    y_ahead = _dot(h, w_ref[:, 0:cw])
    for j in range(nch):
        y = y_ahead
        if j + 1 < nch:
            y_ahead = _dot(h, w_ref[:, (j + 1) * cw:(j + 2) * cw])
        for s in range(cw // LANES):
            blk = j * (cw // LANES) + s
            yb = y[:, s * LANES:(s + 1) * LANES]
            if blk < n_norm:
                ms = _dot((yb * yb).astype(BF16), grp_ref[...])
                yn = yb * lax.rsqrt(ms + EPS) * gain_ref[:, blk * LANES:(blk + 1) * LANES]
                partner = jnp.where(first_half, pltpu.roll(yn, LANES - quarter, 1), pltpu.roll(yn, quarter, 1))
                yb = yn * cos_ref[...] + partner * sin_ref[...]
            o_ref[:, blk * LANES:(blk + 1) * LANES] = yb.astype(BF16)


def _rope_tables(geom):
    tpos = np.arange(geom.ds)
    inv = ROPE_THETA ** (-np.arange(0, HD // 2, 2, dtype=np.float32) / (HD // 2))
    ang_r = (tpos // GRID_W).astype(np.float32)[:, None] * inv
    ang_c = (tpos % GRID_W).astype(np.float32)[:, None] * inv
    ang_r, ang_c = jnp.asarray(ang_r, F32), jnp.asarray(ang_c, F32)
    cos = jnp.concatenate([jnp.cos(ang_r)] * 2 + [jnp.cos(ang_c)] * 2, axis=1)
    sin = jnp.concatenate([-jnp.sin(ang_r), jnp.sin(ang_r), -jnp.sin(ang_c), jnp.sin(ang_c)], axis=1)
    cos = jnp.concatenate([jnp.tile(cos, (1, 2)), jnp.ones((TP, LANES), F32)], axis=0)
    sin = jnp.concatenate([jnp.tile(sin, (1, 2)), jnp.zeros((TP, LANES), F32)], axis=0)
    lane = np.arange(LANES)
    grp = (lane[:, None] // HD == lane[None, :] // HD).astype(np.float32) / HD
    return cos, sin, jnp.asarray(grp, BF16)


def _c_in_proj(geom, x, mod_l, g, w, gain, tables):
    cos, sin, grp = tables
    npt, tps = geom.np_rows // TP, geom.ds // TP

    def pos_map(i):
        return (jnp.where(i < npt, tps, (i - npt) % tps), 0)
    return pl.pallas_call(
        _c_in_kernel,
        out_shape=jax.ShapeDtypeStruct((geom.n, C_OUT), BF16),
        grid=(geom.n // TP,),
        in_specs=[pl.BlockSpec((TP, D), lambda i: (i, 0)),
                  _const_spec((1, D)),
                  geom.mod_spec(1, TP), geom.mod_spec(0, TP),
                  _const_spec(w.shape), _const_spec(gain.shape),
                  pl.BlockSpec((TP, LANES), pos_map), pl.BlockSpec((TP, LANES), pos_map),
                  _const_spec(grp.shape)],
        out_specs=pl.BlockSpec((TP, C_OUT), lambda i: (i, 0)),
        compiler_params=_params(("parallel",)),
        name="gqa_in_proj",
    )(x, g, mod_l, mod_l, w, gain, cos, sin, grp)


def _gqa_attention(geom, proj, k_ctx, v_ctx):
    nqp = GQA_HEADS // 2
    kcol = C_QW // LANES
    vcol = kcol + 2 * C_KW // LANES
    s, ds = geom.s, geom.ds
    o = pl.pallas_call(
        functools.partial(_pair_attn_kernel, has_ctx=False, row_split=1),
        out_shape=jax.ShapeDtypeStruct((geom.n, C_QW), BF16),
        grid=(geom.b, nqp),
        in_specs=[pl.BlockSpec((s, LANES), lambda b, m: (b, m)),
                  pl.BlockSpec((s, LANES), lambda b, m: (b, kcol + m // 2)),
                  pl.BlockSpec((s, LANES), lambda b, m: (b, vcol + m // 2))],
        out_specs=pl.BlockSpec((s, LANES), lambda b, m: (b, m)),
        compiler_params=_params(("parallel", "parallel")),
        name="gqa_prompt_attn",
    )(proj, proj, proj)

    tq = 512
    nqt = ds // tq
    q0 = geom.np_rows // tq
    s0 = geom.np_rows // ds
    kc = jnp.concatenate([k_ctx, k_ctx], axis=-1).astype(BF16)
    vc = jnp.concatenate([v_ctx, v_ctx], axis=-1).astype(BF16)
    nctx = kc.shape[2]
    return pl.pallas_call(
        functools.partial(_pair_attn_kernel, has_ctx=True, row_split=2),
        out_shape=jax.ShapeDtypeStruct((geom.n, C_QW), BF16),
        grid=(geom.db, nqp, nqt),
        in_specs=[pl.BlockSpec((tq, LANES), lambda b, m, t: (q0 + nqt * b + t, m)),
                  pl.BlockSpec((ds, LANES), lambda b, m, t: (s0 + b, kcol + m // 2)),
                  pl.BlockSpec((ds, LANES), lambda b, m, t: (s0 + b, vcol + m // 2)),
                  pl.BlockSpec((None, None, nctx, LANES), lambda b, m, t: (b, m // 2, 0, 0)),
                  pl.BlockSpec((None, None, nctx, LANES), lambda b, m, t: (b, m // 2, 0, 0)),
                  pl.BlockSpec(memory_space=pl.ANY)],
        out_specs=pl.BlockSpec((tq, LANES), lambda b, m, t: (q0 + nqt * b + t, m)),
        input_output_aliases={5: 0},
        compiler_params=_params(("parallel", "parallel", "parallel")),
        name="gqa_sample_attn",
    )(proj, proj, proj, kc, vc, o)


def _ffn_kernel(x_ref, xp_ref, xn_ref, g_ref, sc_ref, sh_ref, gate_ref, wup_ref, cw_ref, cb_ref, wdn_ref,
                o_ref, acc_ref, *, npt, tps):
    i = pl.program_id(0)
    j = jnp.maximum(i - npt, 0) % tps
    is_prompt = i < npt
    has_prev = jnp.logical_and(jnp.logical_not(is_prompt), j > 0).astype(F32)
    has_next = jnp.logical_and(jnp.logical_not(is_prompt), j < tps - 1).astype(F32)
    g, sc, sh = g_ref[...], sc_ref[...], sh_ref[...]
    x = x_ref[...]
    h_ext = jnp.concatenate([_norm_mod(xp_ref[...], g, sc, sh) * has_prev,
                             _norm_mod(x, g, sc, sh),
                             _norm_mod(xn_ref[...], g, sc, sh) * has_next], axis=0).astype(BF16)
    rows = TM + 2 * HALO
    nch = D_FF // FF_CHUNK

    def down(c, a):
        part = _dot(a, wdn_ref[c])
        if c == 0:
            acc_ref[...] = part
        else:
            acc_ref[...] += part

    u_ahead = _dot(h_ext, wup_ref[0])
    a_prev = None
    for c in range(nch):
        u = u_ahead
        if c + 1 < nch:
            u_ahead = _dot(h_ext, wup_ref[c + 1])
        if a_prev is not None:
            down(c - 1, a_prev)
        u_prev = pltpu.roll(u, 1, 0)[HALO:HALO + TM]
        u_next = pltpu.roll(u, rows - 1, 0)[HALO:HALO + TM]
        uc = cw_ref[0, c] * u_prev + cw_ref[1, c] * u[HALO:HALO + TM] + cw_ref[2, c] * u_next + cb_ref[c]
        a_prev = (_silu(uc[:, FF_CHUNK:]) * uc[:, :FF_CHUNK]).astype(BF16)
    down(nch - 1, a_prev)
    o_ref[...] = x + gate_ref[...] * acc_ref[...]


def _conv_ffn(geom, x, mod_l, g, wup, cw, cb, wdn):
    nblk = geom.n // HALO
    per = TM // HALO
    return pl.pallas_call(
        functools.partial(_ffn_kernel, npt=geom.npt, tps=geom.tps),
        out_shape=jax.ShapeDtypeStruct((geom.n, D), F32),
        grid=(geom.n // TM,),
        in_specs=[pl.BlockSpec((TM, D), lambda i: (i, 0)),
                  pl.BlockSpec((HALO, D), lambda i: (jnp.maximum(i * per - 1, 0), 0)),
                  pl.BlockSpec((HALO, D), lambda i: (jnp.minimum((i + 1) * per, nblk - 1), 0)),
                  _const_spec((1, D)),
                  geom.mod_spec(4), geom.mod_spec(3), geom.mod_spec(5),
                  _const_spec(wup.shape), _const_spec(cw.shape), _const_spec(cb.shape), _const_spec(wdn.shape)],
        out_specs=pl.BlockSpec((TM, D), lambda i: (i, 0)),
        scratch_shapes=[pltpu.VMEM((TM, D), F32)],
        compiler_params=_params(("parallel",)),
        name="conv_ffn",
    )(x, x, x, g, mod_l, mod_l, mod_l, wup, cw, cb, wdn)


def _final_norm_kernel(x_ref, g_ref, o_ref):
    x = x_ref[...]
    o_ref[...] = x * lax.rsqrt(jnp.mean(x * x, axis=-1, keepdims=True) + EPS) * g_ref[...]


def _final_norm(x, g, row0, nrows):
    t0 = row0 // TM
    return pl.pallas_call(
        _final_norm_kernel,
        out_shape=jax.ShapeDtypeStruct((nrows, D), F32),
        grid=(nrows // TM,),
        in_specs=[pl.BlockSpec((TM, D), lambda i: (t0 + i, 0)), _const_spec((1, D))],
        out_specs=pl.BlockSpec((TM, D), lambda i: (i, 0)),
        compiler_params=_params(("parallel",)),
        name="final_norm",
    )(x, g)


def _prep_ab(w_in, w2_f, b2_f, w2_b, b2_b):
    scale = np.ones((w_in.shape[1],), np.float32)
    scale[0:NA_W] = HD ** -0.5 * LOG2E
    scale[3 * NA_W:3 * NA_W + GLA_KW] = GLA_DK ** -0.5
    w = jnp.pad(w_in * scale, ((0, 0), (0, AB_MAIN + LANES - w_in.shape[1]))).astype(BF16)
    w2 = jnp.zeros((LANES, 2 * GLA_KW), F32)
    w2 = w2.at[0:GLA_RANK, 0:GLA_KW].set(w2_f).at[GLA_RANK:2 * GLA_RANK, GLA_KW:].set(w2_b)
    w2_hi, w2_lo = _split(w2)
    b2 = jnp.concatenate([b2_f, b2_b])[None, :]
    return w, w2_hi, w2_lo, b2


def _prep_c(w_in, qn_g, kn_g):
    wq, wk, wv = w_in[:, :C_QW], w_in[:, C_QW:C_QW + C_KW], w_in[:, C_QW + C_KW:]

    def dup(w):
        return jnp.broadcast_to(w.reshape(D, GQA_KVH, 1, HD), (D, GQA_KVH, 2, HD)).reshape(D, 2 * C_KW)
    w = jnp.concatenate([wq, dup(wk), dup(wv)], axis=1).astype(BF16)
    gain = jnp.concatenate([jnp.tile(qn_g * (HD ** -0.5 * LOG2E), GQA_HEADS), jnp.tile(kn_g, 2 * GQA_KVH)])[None, :]
    return w, gain


def _prep_ffn(w_up, conv_w, conv_b, w_down):
    nch = D_FF // FF_CHUNK

    def pair(a):
        lead = a.shape[:-1]
        return a.reshape(*lead, 2, nch, FF_CHUNK).swapaxes(-3, -2).reshape(*lead, nch, 2 * FF_CHUNK)
    wup = pair(w_up).swapaxes(0, 1).astype(BF16)
    cw = pair(conv_w)[:, :, None, :]
    cb = pair(conv_b)[:, None, :]
    wdn = w_down.reshape(nch, FF_CHUNK, D).astype(BF16)
    return wup, cw, cb, wdn


def kernel(x_prompt, x_sample, cache_na_k, cache_na_v, state_gla_fwd, state_gla_bwd, cache_gqa_k, cache_gqa_v, c, c_ctx, ada_w, ada_b, norm_mix_g, norm_ffn_g, ab_w_in, ab_w_out, na_rpb, gla_w2_fwd, gla_b2_fwd, gla_w2_bwd, gla_b2_bwd, gla_norm_g, gqa_w_in, gqa_w_out, gqa_q_norm_g, gqa_k_norm_g, ffn_w_up, ffn_conv_w, ffn_conv_b, ffn_w_down, final_norm_g):
    b, s, _ = x_prompt.shape
    db, ds, _ = x_sample.shape
    geom = _Geom(b, s, db, ds)
    assert ds % GRID_W == 0 and geom.np_rows % ds == 0 and geom.np_rows % 512 == 0

    x = jnp.concatenate([x_prompt.reshape(-1, D), x_sample.reshape(-1, D)], axis=0)
    nrow = -(-(db + 1) // 8) * 8
    c_rows = jnp.zeros((nrow, D), F32).at[:db].set(c).at[db].set(c_ctx)
    mod = _modulation(c_rows, ada_w, ada_b).reshape(DEPTH, nrow, 1, 6 * D)

    gla_consts = _gla_constants()
    rope = _rope_tables(geom)
    zeros_state = jnp.zeros((b, GLA_HEADS, GLA_DK, GLA_DV), F32)
    na_k, na_v, gla_f, gla_b, gq_k, gq_v = [], [], [], [], [], []

    for i in range(DEPTH):
        j = i // 2
        mod_l = mod[i]
        if i % 2 == 0:
            w, w2_hi, w2_lo, b2 = _prep_ab(ab_w_in[j], gla_w2_fwd[j], gla_b2_fwd[j], gla_w2_bwd[j], gla_b2_bwd[j])
            proj, gates = _ab_in_proj(geom, x, mod_l, norm_mix_g[i][None, :], w, w2_hi, w2_lo, b2)
            oa = _na_attention(geom, proj, cache_na_k[:, j], cache_na_v[:, j], na_rpb[j])
            gain = gla_norm_g[j][None, :]
            yg, sf, sb = _gla_call(geom, proj, gates, zeros_state, zeros_state, gain, gla_consts, None,
                                   s, b, 0, "gla_prompt")
            yg, _, _ = _gla_call(geom, proj, gates, state_gla_fwd[:, j], state_gla_bwd[:, j], gain, gla_consts, yg,
                                 ds, db, geom.np_rows // ds, "gla_sample")
            x = _out_proj(geom, [oa, yg], ab_w_out[j].astype(BF16), x, mod_l, 2)
            pk = proj[:geom.np_rows].reshape(b, s, -1)
            na_k.append(pk[..., NA_W:2 * NA_W].reshape(b, s, NA_HEADS, HD).transpose(0, 2, 1, 3).astype(F32))
            na_v.append(pk[..., 2 * NA_W:3 * NA_W].reshape(b, s, NA_HEADS, HD).transpose(0, 2, 1, 3).astype(F32))
            gla_f.append(sf)
            gla_b.append(sb)
        else:
            w, gain = _prep_c(gqa_w_in[j], gqa_q_norm_g[j], gqa_k_norm_g[j])
            proj = _c_in_proj(geom, x, mod_l, norm_mix_g[i][None, :], w, gain, rope)
            o = _gqa_attention(geom, proj, cache_gqa_k[:, j], cache_gqa_v[:, j])
            x = _out_proj(geom, [o], gqa_w_out[j].astype(BF16), x, mod_l, 2)
            pk = proj[:geom.np_rows].reshape(b, s, -1)
            kd = pk[..., C_QW:C_QW + 2 * C_KW].reshape(b, s, GQA_KVH, 2, HD)[:, :, :, 0]
            vd = pk[..., C_QW + 2 * C_KW:].reshape(b, s, GQA_KVH, 2, HD)[:, :, :, 0]
            gq_k.append(kd.transpose(0, 2, 1, 3).astype(F32))
            gq_v.append(vd.transpose(0, 2, 1, 3).astype(F32))
        wup, cw, cb, wdn = _prep_ffn(ffn_w_up[i], ffn_conv_w[i], ffn_conv_b[i], ffn_w_down[i])
        x = _conv_ffn(geom, x, mod_l, norm_ffn_g[i][None, :], wup, cw, cb, wdn)

    y_prompt = _final_norm(x, final_norm_g[None, :], 0, geom.np_rows).reshape(b, s, D)
    y_sample = _final_norm(x, final_norm_g[None, :], geom.np_rows, geom.ns_rows).reshape(db, ds, D)
    return (y_prompt, y_sample,
            jnp.stack(na_k, axis=1), jnp.stack(na_v, axis=1),
            jnp.stack(gla_f, axis=1), jnp.stack(gla_b, axis=1),
            jnp.stack(gq_k, axis=1), jnp.stack(gq_v, axis=1))
```

```python
import functools

import numpy as np
import jax
import jax.numpy as jnp
from jax import lax
from jax.experimental import pallas as pl
from jax.experimental.pallas import tpu as pltpu

F32 = jnp.float32
BF16 = jnp.bfloat16

D = 1024
DEPTH = 4
HD = 64
GRID_W = 64
EPS = 1e-6
NA_HEADS = 8
NA_W = NA_HEADS * HD
NA_KR = 8
NA_KC = 16
GLA_HEADS = 4
GLA_DK = 64
GLA_DV = 128
GLA_KW = GLA_HEADS * GLA_DK
GLA_VW = GLA_HEADS * GLA_DV
GLA_RANK = 16
GLA_CHUNK = 64
GQA_HEADS = 16
GQA_KVH = 4
C_QW = GQA_HEADS * HD
C_KW = GQA_KVH * HD
ROPE_THETA = 10000.0
D_FF = 2816
AB_MAIN = 3 * NA_W + 2 * GLA_KW + 2 * GLA_VW

LANES = 128
SUBLANES = 8
TM = 256
TP = 512
LOG2E = 1.4426950408889634
HALO = 16
FF_CHUNK = 256
NA_QROWS = 8
NA_KROWS = NA_QROWS + NA_KR - 1
VMEM_LIMIT = 56 * 1024 * 1024


def _dot(a, b):
    return jnp.dot(a, b, preferred_element_type=F32)


def _dot_nt(a, b):
    return lax.dot_general(a, b, (((1,), (1,)), ((), ())), preferred_element_type=F32)


def _dot_tn(a, b):
    return lax.dot_general(a, b, (((0,), (0,)), ((), ())), preferred_element_type=F32)


def _split(a):
    hi = a.astype(BF16)
    lo = (a - hi.astype(F32)).astype(BF16)
    return hi, lo


def _dot_split(a, w_hi, w_lo):
    a_hi, a_lo = _split(a)
    return _dot(a_hi, w_hi) + _dot(a_lo, w_hi) + _dot(a_hi, w_lo)


def _norm_mod(x, g, sc, sh):
    xn = x * lax.rsqrt(jnp.mean(x * x, axis=-1, keepdims=True) + EPS)
    return (xn * g) * (1.0 + sc) + sh


def _silu(x):
    return x / (1.0 + jnp.exp(-x))


def _lane_lo(shape=(1, LANES)):
    return lax.broadcasted_iota(jnp.int32, shape, len(shape) - 1) < HD


def _stack_heads(x):
    lo = _lane_lo()
    zero = jnp.zeros_like(x)
    return jnp.concatenate([jnp.where(lo, x, zero), jnp.where(lo, zero, x)], axis=0)


def _params(sem):
    return pltpu.CompilerParams(dimension_semantics=sem, vmem_limit_bytes=VMEM_LIMIT)


def _mod_kernel(c_ref, w_ref, b_ref, o_ref):
    a = _silu(c_ref[...])
    w = w_ref[...]
    w_hi, w_lo = _split(w)
    o_ref[...] = _dot_split(a, w_hi, w_lo) + b_ref[...]


def _modulation(c_rows, ada_w, ada_b):
    nrow = c_rows.shape[0]
    tn = 1024
    return pl.pallas_call(
        _mod_kernel,
        out_shape=jax.ShapeDtypeStruct((DEPTH, nrow, 6 * D), F32),
        grid=(DEPTH, 6 * D // tn),
        in_specs=[pl.BlockSpec((nrow, D), lambda l, n: (0, 0)),
                  pl.BlockSpec((None, D, tn), lambda l, n: (l, 0, n)),
                  pl.BlockSpec((None, 1, tn), lambda l, n: (l, 0, n))],
        out_specs=pl.BlockSpec((None, nrow, tn), lambda l, n: (l, 0, n)),
        compiler_params=_params(("parallel", "parallel")),
        name="modulation",
    )(c_rows, ada_w, ada_b.reshape(DEPTH, 1, 6 * D))


class _Geom:
    def __init__(self, b, s, db, ds):
        assert s == TM and ds % TM == 0
        self.b, self.s, self.db, self.ds = b, s, db, ds
        self.np_rows = b * s
        self.ns_rows = db * ds
        self.n = self.np_rows + self.ns_rows
        self.npt = self.np_rows // TM
        self.tps = ds // TM
        self.ctx_row = db

    def mod_spec(self, k, tm=TM):
        assert self.np_rows % tm == 0 and self.ds % tm == 0
        npt, tps, ctx = self.np_rows // tm, self.ds // tm, self.ctx_row

        def imap(i):
            return (jnp.where(i < npt, ctx, (i - npt) // tps), 0, k)
        return pl.BlockSpec((None, 1, D), imap)

    def split_specs(self, tm, width):
        npt = self.np_rows // tm
        return [pl.BlockSpec((tm, width), lambda i: (jnp.minimum(i, npt - 1), 0)),
                pl.BlockSpec((tm, width), lambda i: (jnp.maximum(i - npt, 0), 0))]


def _read_split(p_ref, s_ref, npt):
    return jnp.where(pl.program_id(0) < npt, p_ref[...], s_ref[...])


def _const_spec(shape):
    nd = len(shape)
    return pl.BlockSpec(shape, lambda *_: (0,) * nd)


def _ab_in_kernel(*refs, split_npt):
    nx = 2 if split_npt else 1
    g_ref, sc_ref, sh_ref, w_ref, w2hi_ref, w2lo_ref, b2_ref, o_ref, gate_ref = refs[nx:]
    x = _read_split(refs[0], refs[1], split_npt) if split_npt else refs[0][...]
    h = _norm_mod(x, g_ref[...], sc_ref[...], sh_ref[...]).astype(BF16)
    cw = 512
    lr = _dot(h, w_ref[:, AB_MAIN:AB_MAIN + LANES])
    o_ref[:, 0:cw] = _dot(h, w_ref[:, 0:cw]).astype(BF16)
    z = _dot_split(lr, w2hi_ref[...], w2lo_ref[...]) + b2_ref[...]
    for j in range(1, AB_MAIN // cw):
        o_ref[:, j * cw:(j + 1) * cw] = _dot(h, w_ref[:, j * cw:(j + 1) * cw]).astype(BF16)
    gate_ref[...] = (jnp.minimum(z, 0.0) - jnp.log(1.0 + jnp.exp(-jnp.abs(z)))) * (1.0 / 16.0)


def _ab_in_proj(geom, xs, mod_l, g, w, w2hi, w2lo, b2):
    split = len(xs) == 2
    x_specs = geom.split_specs(TP, D) if split else [pl.BlockSpec((TP, D), lambda i: (i, 0))]
    return pl.pallas_call(
        functools.partial(_ab_in_kernel, split_npt=geom.np_rows // TP if split else 0),
        out_shape=(jax.ShapeDtypeStruct((geom.n, AB_MAIN), BF16),
                   jax.ShapeDtypeStruct((geom.n, 2 * GLA_KW), F32)),
        grid=(geom.n // TP,),
        in_specs=x_specs + [_const_spec((1, D)),
                            geom.mod_spec(1, TP), geom.mod_spec(0, TP),
                            _const_spec(w.shape), _const_spec(w2hi.shape), _const_spec(w2lo.shape),
                            _const_spec(b2.shape)],
        out_specs=(pl.BlockSpec((TP, AB_MAIN), lambda i: (i, 0)),
                   pl.BlockSpec((TP, 2 * GLA_KW), lambda i: (i, 0))),
        compiler_params=_params(("parallel",)),
        name="ab_in_proj",
    )(*xs, g, mod_l, mod_l, w, w2hi, w2lo, b2)


def _pair_attn_kernel(*refs, has_ctx, row_split):
    if has_ctx:
        q_ref, k_ref, v_ref, kc_ref, vc_ref, o_ref = refs
        kv = ((k_ref, v_ref), (kc_ref, vc_ref))
    else:
        q_ref, k_ref, v_ref, o_ref = refs
        kv = ((k_ref, v_ref),)
    q = q_ref[...]
    lo = _lane_lo()
    zero = jnp.zeros_like(q)
    q_heads = (jnp.where(lo, q, zero), jnp.where(lo, zero, q))
    rows = q.shape[0] // row_split
    blocks = [(hh, r) for r in range(row_split) for hh in range(2)]

    def scores(b):
        hh, r = blocks[b]
        qb = q_heads[hh][r * rows:(r + 1) * rows]
        return [_dot_nt(qb, kr[...]) for kr, _ in kv]

    def weighted_values(b, p_parts, l):
        outs[blocks[b]] = sum(_dot(p, vr[...]) for p, (_, vr) in zip(p_parts, kv)) / l

    outs = {}
    s_ahead = scores(0)
    pending = None
    for b in range(len(blocks)):
        s_parts = s_ahead
        if b + 1 < len(blocks):
            s_ahead = scores(b + 1)
        if pending is not None:
            weighted_values(b - 1, *pending)
        m = functools.reduce(jnp.maximum, [jnp.max(s, axis=-1, keepdims=True) for s in s_parts])
        p_parts = [jnp.exp2(s - m) for s in s_parts]
        l = sum(jnp.sum(p, axis=-1, keepdims=True) for p in p_parts)
        pending = ([p.astype(BF16) for p in p_parts], l)
    weighted_values(len(blocks) - 1, *pending)
    for r in range(row_split):
        o_ref[r * rows:(r + 1) * rows, :] = jnp.where(lo, outs[(0, r)], outs[(1, r)]).astype(BF16)


def _na_kernel(q_ref, k_ref, v_ref, kc_ref, vc_ref, bias_ref, o_ref):
    rb = pl.program_id(1)
    nq = q_ref.shape[0]
    nk = NA_KROWS * GRID_W
    last = k_ref.shape[0] // GRID_W - NA_KROWS
    k0 = pl.multiple_of(jnp.clip(NA_QROWS * rb - NA_KR // 2, 0, last) * GRID_W, GRID_W)
    q = q_ref[...]
    lo = _lane_lo()
    zero = jnp.zeros_like(q)
    q_heads = (jnp.where(lo, q, zero), jnp.where(lo, zero, q))
    kl = k_ref[pl.ds(k0, nk), :]
    vl = v_ref[pl.ds(k0, nk), :]
    scores = [(_dot_nt(qh, kl) + bias_ref[hh * nq:(hh + 1) * nq, :], _dot_nt(qh, kc_ref[...]))
              for hh, qh in enumerate(q_heads)]
    outs = []
    for s_loc, s_ctx in scores:
        m = jnp.maximum(jnp.max(s_loc, axis=-1, keepdims=True), jnp.max(s_ctx, axis=-1, keepdims=True))
        p_loc = jnp.exp2(s_loc - m)
        p_ctx = jnp.exp2(s_ctx - m)
        l = jnp.sum(p_loc, axis=-1, keepdims=True) + jnp.sum(p_ctx, axis=-1, keepdims=True)
        outs.append((_dot(p_loc.astype(BF16), vl) + _dot(p_ctx.astype(BF16), vc_ref[...])) / l)
    o_ref[...] = jnp.where(lo, outs[0], outs[1]).astype(BF16)


def _na_bias_tiles(rpb, rows):
    hi = lax.Precision.HIGHEST
    col = np.arange(GRID_W)
    cs = np.clip(col - NA_KC // 2, 0, GRID_W - NA_KC)
    col_ok = (col[None, :] >= cs[:, None]) & (col[None, :] < cs[:, None] + NA_KC)
    dc = np.clip(col[None, :] - col[:, None] + NA_KC - 1, 0, 2 * NA_KC - 2)
    pick_dc = (np.arange(2 * NA_KC - 1)[:, None, None] == dc[None]).astype(np.float32)
    by_col = jnp.einsum('hdj,jqc->hdqc', rpb.astype(F32) * LOG2E, pick_dc, precision=hi)
    by_col = by_col.reshape(NA_HEADS // 2, 2, 2 * NA_KR - 1, GRID_W, GRID_W)
    pick_dr = np.zeros((3, NA_QROWS, NA_KROWS, 2 * NA_KR - 1), np.float32)
    row_mask = np.full((3, NA_QROWS, NA_KROWS), -np.inf, np.float32)
    for t, r0 in enumerate((0, NA_QROWS, rows - NA_QROWS)):
        kstart = int(np.clip(r0 - NA_KR // 2, 0, rows - NA_KROWS))
        for qr in range(NA_QROWS):
            r = r0 + qr
            rs = int(np.clip(r - NA_KR // 2, 0, rows - NA_KR))
            for kk in range(NA_KROWS):
                kr = kstart + kk
                if rs <= kr < rs + NA_KR:
                    pick_dr[t, qr, kk, kr - r + NA_KR - 1] = 1.0
                    row_mask[t, qr, kk] = 0.0
    col_mask = np.where(col_ok, 0.0, -np.inf).astype(np.float32)
    tiles = jnp.einsum('trkd,phdqc->pthrqkc', pick_dr, by_col, precision=hi)
    tiles = tiles + row_mask[None, :, None, :, None, :, None] + col_mask[None, None, None, None, :, None, :]
    return tiles.reshape(NA_HEADS // 2, 3, 2 * NA_QROWS * GRID_W, NA_KROWS * GRID_W)


def _na_attention(geom, proj, k_ctx, v_ctx, rpb):
    npairs = NA_HEADS // 2
    s, ds = geom.s, geom.ds
    oa_p = pl.pallas_call(
        functools.partial(_pair_attn_kernel, has_ctx=False, row_split=1),
        out_shape=jax.ShapeDtypeStruct((geom.np_rows, NA_W), BF16),
        grid=(geom.b, npairs),
        in_specs=[pl.BlockSpec((s, LANES), lambda b, p: (b, p)),
                  pl.BlockSpec((s, LANES), lambda b, p: (b, npairs + p)),
                  pl.BlockSpec((s, LANES), lambda b, p: (b, 2 * npairs + p))],
        out_specs=pl.BlockSpec((s, LANES), lambda b, p: (b, p)),
        compiler_params=_params(("parallel", "parallel")),
        name="na_prompt_attn",
    )(proj, proj, proj)

    rows = ds // GRID_W
    assert rows % NA_QROWS == 0 and rows >= NA_KROWS
    nrb = rows // NA_QROWS
    nq = NA_QROWS * GRID_W
    nk = NA_KROWS * GRID_W
    bias = _na_bias_tiles(rpb, rows)
    q0 = geom.np_rows // nq
    s0 = geom.np_rows // ds
    kc = k_ctx.reshape(geom.db, npairs, 2, -1, HD).transpose(0, 1, 3, 2, 4).reshape(geom.db, npairs, -1, LANES).astype(BF16)
    vc = v_ctx.reshape(geom.db, npairs, 2, -1, HD).transpose(0, 1, 3, 2, 4).reshape(geom.db, npairs, -1, LANES).astype(BF16)
    nctx = kc.shape[2]
    oa_s = pl.pallas_call(
        _na_kernel,
        out_shape=jax.ShapeDtypeStruct((geom.ns_rows, NA_W), BF16),
        grid=(npairs, nrb, geom.db),
        in_specs=[pl.BlockSpec((nq, LANES), lambda p, r, b: (q0 + nrb * b + r, p)),
                  pl.BlockSpec((ds, LANES), lambda p, r, b: (s0 + b, npairs + p)),
                  pl.BlockSpec((ds, LANES), lambda p, r, b: (s0 + b, 2 * npairs + p)),
                  pl.BlockSpec((None, None, nctx, LANES), lambda p, r, b: (b, p, 0, 0)),
                  pl.BlockSpec((None, None, nctx, LANES), lambda p, r, b: (b, p, 0, 0)),
                  pl.BlockSpec((None, None, 2 * nq, nk),
                               lambda p, r, b: (p, jnp.minimum(r, 1) + r // (nrb - 1), 0, 0))],
        out_specs=pl.BlockSpec((nq, LANES), lambda p, r, b: (nrb * b + r, p)),
        compiler_params=_params(("parallel", "parallel", "parallel")),
        name="na_sample_attn",
    )(proj, proj, proj, kc, vc, bias)
    return oa_p, oa_s


GLA_LEVELS = 6


def _gla_constants():
    c = GLA_CHUNK
    t = np.arange(c)
    i, j = t[:, None], t[None, :]
    out = {}
    for fwd in (True, False):
        tri = (j <= i) if fwd else (j >= i)
        masks = np.zeros((GLA_LEVELS + 1, c, c), np.float32)
        for lvl in range(GLA_LEVELS):
            half = c >> (lvl + 1)
            upper = (t % (2 * half)) >= half
            same = (i // (2 * half)) == (j // (2 * half))
            masks[lvl] = (same & upper[:, None] & ~upper[None, :]) if fwd else (same & ~upper[:, None] & upper[None, :])
        masks[GLA_LEVELS] = np.eye(c)
        out[fwd] = (jnp.asarray(tri.astype(np.float32), BF16),
                    jnp.asarray(np.concatenate([masks, masks], axis=1).reshape(-1, c), F32))
    return out


def _row_of_block(x, block, row):
    c = x.shape[0]
    if block >= SUBLANES:
        return jnp.concatenate(
            [jnp.broadcast_to(x[b * block + row:b * block + row + 1, :], (block, x.shape[1])) for b in range(c // block)],
            axis=0)
    per = SUBLANES // block
    sub = lax.broadcasted_iota(jnp.int32, x.shape, 0) % SUBLANES
    out = _row_of_block(x, SUBLANES, row)
    for p in range(1, per):
        out = jnp.where(sub >= p * block, _row_of_block(x, SUBLANES, p * block + row), out)
    return out


def _gla_cumulative(g, tri):
    g_hi, g_lo = _split(g)
    return _dot(tri, g_hi) + _dot(tri, g_lo)


def _gla_intra(q, k, g, b, masks, fwd):
    c = GLA_CHUNK
    t = lax.broadcasted_iota(jnp.int32, b.shape, 0)
    att = jnp.zeros((2 * c, c), F32)
    for lvl in range(GLA_LEVELS + 1):
        half = c >> (lvl + 1)
        if half >= 2:
            ref = _row_of_block(b, 2 * half, half - 1 if fwd else half)
            past_mid = ((t % (2 * half)) >= half) == fwd
            w = jnp.exp(jnp.where(past_mid, b - ref, ref - b))
            ql, kl = q * w, k * w
        elif half == 1:
            w = jnp.exp(jnp.where((t % 2 == 1) == fwd, g, 0.0))
            ql, kl = q * w, k * w
        else:
            ql, kl = q, k
        p = _dot_nt(_stack_heads(ql).astype(BF16), kl.astype(BF16))
        att = att + p * masks[lvl * 2 * c:(lvl + 1) * 2 * c]
    return att.astype(BF16)


def _gla_outputs(q, k, v, b, att, st, fwd):
    c = GLA_CHUNK
    edge = b[c - 1:c] if fwd else b[0:1]
    q_dec = q * jnp.exp(b)
    o_st = _dot_nt(_stack_heads(q_dec).astype(BF16), st.astype(BF16))
    r = _dot(att, v)
    o = jnp.concatenate([o_st[0:c] + r[0:c, 0:GLA_DV], o_st[c:] + r[c:, GLA_DV:]], axis=1)
    k_dec = (k * jnp.exp(edge - b)).astype(BF16)
    u = _dot_tn(v, k_dec)
    st_new = jnp.exp(edge) * st + jnp.where(_lane_lo(), u[0:GLA_DV], u[GLA_DV:])
    return o, st_new


GLA_GROUP = 4


def _gla_kernel(q_ref, k_ref, v_ref, rg_ref, gf_ref, gb_ref, s0f_ref, s0b_ref, gain_ref,
                tri_f_ref, masks_f_ref, tri_b_ref, masks_b_ref, y_ref, sf_ref, sb_ref,
                of_scr, ob_scr, st_scr):
    t = q_ref.shape[0]
    c = GLA_CHUNK
    nc = t // c
    assert nc % GLA_GROUP == 0
    for d, s0_ref in enumerate((s0f_ref, s0b_ref)):
        st_scr[d] = jnp.concatenate([s0_ref[0], s0_ref[1]], axis=0).T
    per_dir = ((gf_ref, of_scr, tri_f_ref, masks_f_ref), (gb_ref, ob_scr, tri_b_ref, masks_b_ref))

    def body(gi, carry):
        streams = []
        for u in range(GLA_GROUP):
            for d in range(2):
                ci = gi * GLA_GROUP + u
                cc = ci if d == 0 else nc - 1 - ci
                streams.append((d, pl.ds(pl.multiple_of(cc * c, c), c)))
        qkg = [(q_ref[rows, :].astype(F32), k_ref[rows, :].astype(F32), per_dir[d][0][rows, :])
               for d, rows in streams]
        cums = [_gla_cumulative(g, per_dir[d][2][...]) for (d, _), (_, _, g) in zip(streams, qkg)]
        atts = [_gla_intra(q, k, g, b, per_dir[d][3][...], d == 0)
                for (d, _), (q, k, g), b in zip(streams, qkg, cums)]
        for (d, rows), (q, k, _), b, att in zip(streams, qkg, cums, atts):
            o, st_new = _gla_outputs(q, k, v_ref[rows, :], b, att, st_scr[d], d == 0)
            per_dir[d][1][rows, :] = o
            st_scr[d] = st_new
        return carry

    lax.fori_loop(0, nc // GLA_GROUP, body, 0)

    for d, s_ref in enumerate((sf_ref, sb_ref)):
        s_pair = st_scr[d].T
        s_ref[0] = s_pair[0:GLA_DK]
        s_ref[1] = s_pair[GLA_DK:]

    rt = 256
    for r in range(t // rt):
        og = of_scr[r * rt:(r + 1) * rt, :] + ob_scr[r * rt:(r + 1) * rt, :]
        halves = []
        for hh in range(2):
            x = og[:, hh * GLA_DV:(hh + 1) * GLA_DV]
            halves.append(x * lax.rsqrt(jnp.mean(x * x, axis=-1, keepdims=True) + EPS))
        y = jnp.concatenate(halves, axis=1) * gain_ref[...] * _silu(rg_ref[r * rt:(r + 1) * rt, :].astype(F32))
        y_ref[r * rt:(r + 1) * rt, :] = y.astype(BF16)


def _gla_call(proj, gates, s0f, s0b, gain, consts, t, nb, row0, name):
    npairs = GLA_HEADS // 2
    qcol = 3 * NA_W // LANES
    kcol = qcol + GLA_KW // LANES
    vcol = (3 * NA_W + 2 * GLA_KW) // (2 * GLA_DV)
    rcol = vcol + GLA_VW // (2 * GLA_DV)
    tri_f, masks_f = consts[True]
    tri_b, masks_b = consts[False]
    st_spec = pl.BlockSpec((None, 2, GLA_DK, GLA_DV), lambda b, p: (b, p, 0, 0))
    return pl.pallas_call(
        _gla_kernel,
        out_shape=(jax.ShapeDtypeStruct((nb * t, GLA_VW), BF16),
                   jax.ShapeDtypeStruct((nb, GLA_HEADS, GLA_DK, GLA_DV), F32),
                   jax.ShapeDtypeStruct((nb, GLA_HEADS, GLA_DK, GLA_DV), F32)),
        grid=(nb, npairs),
        in_specs=[pl.BlockSpec((t, LANES), lambda b, p: (row0 + b, qcol + p)),
                  pl.BlockSpec((t, LANES), lambda b, p: (row0 + b, kcol + p)),
                  pl.BlockSpec((t, 2 * GLA_DV), lambda b, p: (row0 + b, vcol + p)),
                  pl.BlockSpec((t, 2 * GLA_DV), lambda b, p: (row0 + b, rcol + p)),
                  pl.BlockSpec((t, LANES), lambda b, p: (row0 + b, p)),
                  pl.BlockSpec((t, LANES), lambda b, p: (row0 + b, npairs + p)),
                  st_spec, st_spec,
                  pl.BlockSpec((1, 2 * GLA_DV), lambda b, p: (0, p)),
                  _const_spec(tri_f.shape), _const_spec(masks_f.shape),
                  _const_spec(tri_b.shape), _const_spec(masks_b.shape)],
        out_specs=(pl.BlockSpec((t, 2 * GLA_DV), lambda b, p: (b, p)), st_spec, st_spec),
        scratch_shapes=[pltpu.VMEM((t, 2 * GLA_DV), F32), pltpu.VMEM((t, 2 * GLA_DV), F32),
                        pltpu.VMEM((2, GLA_DV, LANES), F32)],
        compiler_params=_params(("parallel", "parallel")),
        name=name,
    )(proj, proj, proj, proj, gates, gates, s0f, s0b, gain, tri_f, masks_f, tri_b, masks_b)


def _out_proj_kernel(*refs, n_act, split_npt, x_split):
    npt = split_npt
    nx = 2 if x_split else 1
    act_refs = refs[:2 * n_act]
    w_ref = refs[2 * n_act]
    x_refs = refs[2 * n_act + 1:2 * n_act + 1 + nx]
    gate_ref, o_ref = refs[2 * n_act + 1 + nx:]
    acc = None
    k0 = 0
    for a in range(n_act):
        act = _read_split(act_refs[2 * a], act_refs[2 * a + 1], npt)
        kw = act.shape[1]
        part = _dot(act, w_ref[k0:k0 + kw, :])
        acc = part if acc is None else acc + part
        k0 += kw
    x = _read_split(x_refs[0], x_refs[1], npt) if x_split else x_refs[0][...]
    o_ref[...] = x + gate_ref[...] * acc


def _out_proj(geom, act_pairs, w, xs, mod_l, gate_k):
    x_split = len(xs) == 2
    in_specs, args = [], []
    for a_p, a_s in act_pairs:
        in_specs += geom.split_specs(TP, a_p.shape[1])
        args += [a_p, a_s]
    in_specs.append(_const_spec(w.shape))
    in_specs += geom.split_specs(TP, D) if x_split else [pl.BlockSpec((TP, D), lambda i: (i, 0))]
    in_specs.append(geom.mod_spec(gate_k, TP))
    return pl.pallas_call(
        functools.partial(_out_proj_kernel, n_act=len(act_pairs), split_npt=geom.np_rows // TP, x_split=x_split),
        out_shape=jax.ShapeDtypeStruct((geom.n, D), F32),
        grid=(geom.n // TP,),
        in_specs=in_specs,
        out_specs=pl.BlockSpec((TP, D), lambda i: (i, 0)),
        compiler_params=_params(("parallel",)),
        name="out_proj",
    )(*args, w, *xs, mod_l)


C_OUT = C_QW + 4 * C_KW


def _c_in_kernel(x_ref, g_ref, sc_ref, sh_ref, w_ref, gain_ref, cos_ref, sin_ref, grp_ref, o_ref):
    h = _norm_mod(x_ref[...], g_ref[...], sc_ref[...], sh_ref[...]).astype(BF16)
    cw = 512
    nch = C_OUT // cw
    n_norm = (C_QW + 2 * C_KW) // LANES
    quarter = HD // 4
    first_half = lax.broadcasted_iota(jnp.int32, (1, LANES), 1) % (2 * quarter) < quarter
    y_ahead = _dot(h, w_ref[:, 0:cw])
    for j in range(nch):
        y = y_ahead
        if j + 1 < nch:
            y_ahead = _dot(h, w_ref[:, (j + 1) * cw:(j + 2) * cw])
        for s in range(cw // LANES):
            blk = j * (cw // LANES) + s
            yb = y[:, s * LANES:(s + 1) * LANES]
            if blk < n_norm:
                ms = _dot((yb * yb).astype(BF16), grp_ref[...])
                yn = yb * lax.rsqrt(ms + EPS) * gain_ref[:, blk * LANES:(blk + 1) * LANES]
                partner = jnp.where(first_half, pltpu.roll(yn, LANES - quarter, 1), pltpu.roll(yn, quarter, 1))
                yb = yn * cos_ref[...] + partner * sin_ref[...]
            o_ref[:, blk * LANES:(blk + 1) * LANES] = yb.astype(BF16)


def _rope_tables(geom):
    tpos = np.arange(geom.ds)
    inv = ROPE_THETA ** (-np.arange(0, HD // 2, 2, dtype=np.float32) / (HD // 2))
    ang_r = (tpos // GRID_W).astype(np.float32)[:, None] * inv
    ang_c = (tpos % GRID_W).astype(np.float32)[:, None] * inv
    ang_r, ang_c = jnp.asarray(ang_r, F32), jnp.asarray(ang_c, F32)
    cos = jnp.concatenate([jnp.cos(ang_r)] * 2 + [jnp.cos(ang_c)] * 2, axis=1)
    sin = jnp.concatenate([-jnp.sin(ang_r), jnp.sin(ang_r), -jnp.sin(ang_c), jnp.sin(ang_c)], axis=1)
    cos = jnp.concatenate([jnp.tile(cos, (1, 2)), jnp.ones((TP, LANES), F32)], axis=0)
    sin = jnp.concatenate([jnp.tile(sin, (1, 2)), jnp.zeros((TP, LANES), F32)], axis=0)
    lane = np.arange(LANES)
    grp = (lane[:, None] // HD == lane[None, :] // HD).astype(np.float32) / HD
    return cos, sin, jnp.asarray(grp, BF16)


def _c_in_proj(geom, x, mod_l, g, w, gain, tables):
    cos, sin, grp = tables
    npt, tps = geom.np_rows // TP, geom.ds // TP

    def pos_map(i):
        return (jnp.where(i < npt, tps, (i - npt) % tps), 0)
    return pl.pallas_call(
        _c_in_kernel,
        out_shape=jax.ShapeDtypeStruct((geom.n, C_OUT), BF16),
        grid=(geom.n // TP,),
        in_specs=[pl.BlockSpec((TP, D), lambda i: (i, 0)),
                  _const_spec((1, D)),
                  geom.mod_spec(1, TP), geom.mod_spec(0, TP),
                  _const_spec(w.shape), _const_spec(gain.shape),
                  pl.BlockSpec((TP, LANES), pos_map), pl.BlockSpec((TP, LANES), pos_map),
                  _const_spec(grp.shape)],
        out_specs=pl.BlockSpec((TP, C_OUT), lambda i: (i, 0)),
        compiler_params=_params(("parallel",)),
        name="gqa_in_proj",
    )(x, g, mod_l, mod_l, w, gain, cos, sin, grp)


def _gqa_attention(geom, proj, k_ctx, v_ctx):
    nqp = GQA_HEADS // 2
    kcol = C_QW // LANES
    vcol = kcol + 2 * C_KW // LANES
    s, ds = geom.s, geom.ds
    o_p = pl.pallas_call(
        functools.partial(_pair_attn_kernel, has_ctx=False, row_split=1),
        out_shape=jax.ShapeDtypeStruct((geom.np_rows, C_QW), BF16),
        grid=(geom.b, nqp),
        in_specs=[pl.BlockSpec((s, LANES), lambda b, m: (b, m)),
                  pl.BlockSpec((s, LANES), lambda b, m: (b, kcol + m // 2)),
                  pl.BlockSpec((s, LANES), lambda b, m: (b, vcol + m // 2))],
        out_specs=pl.BlockSpec((s, LANES), lambda b, m: (b, m)),
        compiler_params=_params(("parallel", "parallel")),
        name="gqa_prompt_attn",
    )(proj, proj, proj)

    tq = 512
    nqt = ds // tq
    q0 = geom.np_rows // tq
    s0 = geom.np_rows // ds
    kc = jnp.concatenate([k_ctx, k_ctx], axis=-1).astype(BF16)
    vc = jnp.concatenate([v_ctx, v_ctx], axis=-1).astype(BF16)
    nctx = kc.shape[2]
    o_s = pl.pallas_call(
        functools.partial(_pair_attn_kernel, has_ctx=True, row_split=2),
        out_shape=jax.ShapeDtypeStruct((geom.ns_rows, C_QW), BF16),
        grid=(geom.db, nqp, nqt),
        in_specs=[pl.BlockSpec((tq, LANES), lambda b, m, t: (q0 + nqt * b + t, m)),
                  pl.BlockSpec((ds, LANES), lambda b, m, t: (s0 + b, kcol + m // 2)),
                  pl.BlockSpec((ds, LANES), lambda b, m, t: (s0 + b, vcol + m // 2)),
                  pl.BlockSpec((None, None, nctx, LANES), lambda b, m, t: (b, m // 2, 0, 0)),
                  pl.BlockSpec((None, None, nctx, LANES), lambda b, m, t: (b, m // 2, 0, 0))],
        out_specs=pl.BlockSpec((tq, LANES), lambda b, m, t: (nqt * b + t, m)),
        compiler_params=_params(("parallel", "parallel", "parallel")),
        name="gqa_sample_attn",
    )(proj, proj, proj, kc, vc)
    return o_p, o_s


def _ffn_kernel(x_ref, xp_ref, xn_ref, g_ref, sc_ref, sh_ref, gate_ref, wup_ref, cw_ref, cb_ref, wdn_ref,
                o_ref, acc_ref, *, npt, tps):
    i = pl.program_id(0)
    j = jnp.maximum(i - npt, 0) % tps
    is_prompt = i < npt
    has_prev = jnp.logical_and(jnp.logical_not(is_prompt), j > 0).astype(F32)
    has_next = jnp.logical_and(jnp.logical_not(is_prompt), j < tps - 1).astype(F32)
    g, sc, sh = g_ref[...], sc_ref[...], sh_ref[...]
    x = x_ref[...]
    h_ext = jnp.concatenate([_norm_mod(xp_ref[...], g, sc, sh) * has_prev,
                             _norm_mod(x, g, sc, sh),
                             _norm_mod(xn_ref[...], g, sc, sh) * has_next], axis=0).astype(BF16)
    rows = TM + 2 * HALO
    nch = D_FF // FF_CHUNK

    def up(c):
        return [_dot(h_ext, wup_ref[:, off + c * FF_CHUNK:off + (c + 1) * FF_CHUNK]) for off in (0, D_FF)]

    def conv(u, off, c):
        cols = slice(off + c * FF_CHUNK, off + (c + 1) * FF_CHUNK)
        u_prev = pltpu.roll(u, 1, 0)[HALO:HALO + TM]
        u_next = pltpu.roll(u, rows - 1, 0)[HALO:HALO + TM]
        return (cw_ref[0:1, cols] * u_prev + cw_ref[1:2, cols] * u[HALO:HALO + TM]
                + cw_ref[2:3, cols] * u_next + cb_ref[:, cols])

    def down(c, a):
        part = _dot(a, wdn_ref[c * FF_CHUNK:(c + 1) * FF_CHUNK, :])
        if c == 0:
            acc_ref[...] = part
        else:
            acc_ref[...] += part

    u_ahead = up(0)
    a_prev = None
    for c in range(nch):
        u_val, u_gate = u_ahead
        if c + 1 < nch:
            u_ahead = up(c + 1)
        if a_prev is not None:
            down(c - 1, a_prev)
        a_prev = (_silu(conv(u_gate, D_FF, c)) * conv(u_val, 0, c)).astype(BF16)
    down(nch - 1, a_prev)
    o_ref[...] = x + gate_ref[...] * acc_ref[...]


def _conv_ffn(geom, x, mod_l, g, wup, cw, cb, wdn):
    nblk = geom.n // HALO
    per = TM // HALO
    return pl.pallas_call(
        functools.partial(_ffn_kernel, npt=geom.npt, tps=geom.tps),
        out_shape=jax.ShapeDtypeStruct((geom.n, D), F32),
        grid=(geom.n // TM,),
        in_specs=[pl.BlockSpec((TM, D), lambda i: (i, 0)),
                  pl.BlockSpec((HALO, D), lambda i: (jnp.maximum(i * per - 1, 0), 0)),
                  pl.BlockSpec((HALO, D), lambda i: (jnp.minimum((i + 1) * per, nblk - 1), 0)),
                  _const_spec((1, D)),
                  geom.mod_spec(4), geom.mod_spec(3), geom.mod_spec(5),
                  _const_spec(wup.shape), _const_spec(cw.shape), _const_spec(cb.shape), _const_spec(wdn.shape)],
        out_specs=pl.BlockSpec((TM, D), lambda i: (i, 0)),
        scratch_shapes=[pltpu.VMEM((TM, D), F32)],
        compiler_params=_params(("parallel",)),
        name="conv_ffn",
    )(x, x, x, g, mod_l, mod_l, mod_l, wup, cw, cb, wdn)


def _final_norm_kernel(x_ref, g_ref, o_ref):
    x = x_ref[...]
    o_ref[...] = x * lax.rsqrt(jnp.mean(x * x, axis=-1, keepdims=True) + EPS) * g_ref[...]


def _final_norm(x, g, row0, nrows):
    t0 = row0 // TM
    return pl.pallas_call(
        _final_norm_kernel,
        out_shape=jax.ShapeDtypeStruct((nrows, D), F32),
        grid=(nrows // TM,),
        in_specs=[pl.BlockSpec((TM, D), lambda i: (t0 + i, 0)), _const_spec((1, D))],
        out_specs=pl.BlockSpec((TM, D), lambda i: (i, 0)),
        compiler_params=_params(("parallel",)),
        name="final_norm",
    )(x, g)


def _prep_ab(w_in, w2_f, b2_f, w2_b, b2_b):
    scale = np.ones((w_in.shape[1],), np.float32)
    scale[0:NA_W] = HD ** -0.5 * LOG2E
    scale[3 * NA_W:3 * NA_W + GLA_KW] = GLA_DK ** -0.5
    w = jnp.pad(w_in * scale, ((0, 0), (0, AB_MAIN + LANES - w_in.shape[1]))).astype(BF16)
    w2 = jnp.zeros((LANES, 2 * GLA_KW), F32)
    w2 = w2.at[0:GLA_RANK, 0:GLA_KW].set(w2_f).at[GLA_RANK:2 * GLA_RANK, GLA_KW:].set(w2_b)
    w2_hi, w2_lo = _split(w2)
    b2 = jnp.concatenate([b2_f, b2_b])[None, :]
    return w, w2_hi, w2_lo, b2


def _prep_c(w_in, qn_g, kn_g):
    wq, wk, wv = w_in[:, :C_QW], w_in[:, C_QW:C_QW + C_KW], w_in[:, C_QW + C_KW:]

    def dup(w):
        return jnp.broadcast_to(w.reshape(D, GQA_KVH, 1, HD), (D, GQA_KVH, 2, HD)).reshape(D, 2 * C_KW)
    w = jnp.concatenate([wq, dup(wk), dup(wv)], axis=1).astype(BF16)
    gain = jnp.concatenate([jnp.tile(qn_g * (HD ** -0.5 * LOG2E), GQA_HEADS), jnp.tile(kn_g, 2 * GQA_KVH)])[None, :]
    return w, gain


def kernel(x_prompt, x_sample, cache_na_k, cache_na_v, state_gla_fwd, state_gla_bwd, cache_gqa_k, cache_gqa_v, c, c_ctx, ada_w, ada_b, norm_mix_g, norm_ffn_g, ab_w_in, ab_w_out, na_rpb, gla_w2_fwd, gla_b2_fwd, gla_w2_bwd, gla_b2_bwd, gla_norm_g, gqa_w_in, gqa_w_out, gqa_q_norm_g, gqa_k_norm_g, ffn_w_up, ffn_conv_w, ffn_conv_b, ffn_w_down, final_norm_g):
    b, s, _ = x_prompt.shape
    db, ds, _ = x_sample.shape
    geom = _Geom(b, s, db, ds)
    assert ds % GRID_W == 0 and geom.np_rows % ds == 0 and geom.np_rows % TP == 0

    xs = [x_prompt.reshape(-1, D), x_sample.reshape(-1, D)]
    nrow = -(-(db + 1) // SUBLANES) * SUBLANES
    c_rows = jnp.zeros((nrow, D), F32).at[:db].set(c).at[db].set(c_ctx)
    mod = _modulation(c_rows, ada_w, ada_b).reshape(DEPTH, nrow, 1, 6 * D)

    gla_consts = _gla_constants()
    rope = _rope_tables(geom)
    zeros_state = jnp.zeros((b, GLA_HEADS, GLA_DK, GLA_DV), F32)
    na_k, na_v, gla_f, gla_b, gq_k, gq_v = [], [], [], [], [], []

    for i in range(DEPTH):
        j = i // 2
        mod_l = mod[i]
        if i % 2 == 0:
            w, w2_hi, w2_lo, b2 = _prep_ab(ab_w_in[j], gla_w2_fwd[j], gla_b2_fwd[j], gla_w2_bwd[j], gla_b2_bwd[j])
            proj, gates = _ab_in_proj(geom, xs, mod_l, norm_mix_g[i][None, :], w, w2_hi, w2_lo, b2)
            oa = _na_attention(geom, proj, cache_na_k[:, j], cache_na_v[:, j], na_rpb[j])
            gain = gla_norm_g[j][None, :]
            yg_p, sf, sb = _gla_call(proj, gates, zeros_state, zeros_state, gain, gla_consts, s, b, 0, "gla_prompt")
            yg_s, _, _ = _gla_call(proj, gates, state_gla_fwd[:, j], state_gla_bwd[:, j], gain, gla_consts,
                                   ds, db, geom.np_rows // ds, "gla_sample")
            x = _out_proj(geom, [oa, (yg_p, yg_s)], ab_w_out[j].astype(BF16), xs, mod_l, 2)
            pk = proj[:geom.np_rows].reshape(b, s, -1)
            na_k.append(pk[..., NA_W:2 * NA_W].reshape(b, s, NA_HEADS, HD).transpose(0, 2, 1, 3).astype(F32))
            na_v.append(pk[..., 2 * NA_W:3 * NA_W].reshape(b, s, NA_HEADS, HD).transpose(0, 2, 1, 3).astype(F32))
            gla_f.append(sf)
            gla_b.append(sb)
        else:
            w, gain = _prep_c(gqa_w_in[j], gqa_q_norm_g[j], gqa_k_norm_g[j])
            proj = _c_in_proj(geom, xs[0], mod_l, norm_mix_g[i][None, :], w, gain, rope)
            o = _gqa_attention(geom, proj, cache_gqa_k[:, j], cache_gqa_v[:, j])
            x = _out_proj(geom, [o], gqa_w_out[j].astype(BF16), xs, mod_l, 2)
            pk = proj[:geom.np_rows].reshape(b, s, -1)
            kd = pk[..., C_QW:C_QW + 2 * C_KW].reshape(b, s, GQA_KVH, 2, HD)[:, :, :, 0]
            vd = pk[..., C_QW + 2 * C_KW:].reshape(b, s, GQA_KVH, 2, HD)[:, :, :, 0]
            gq_k.append(kd.transpose(0, 2, 1, 3).astype(F32))
            gq_v.append(vd.transpose(0, 2, 1, 3).astype(F32))
        x = _conv_ffn(geom, x, mod_l, norm_ffn_g[i][None, :], ffn_w_up[i].astype(BF16), ffn_conv_w[i],
                      ffn_conv_b[i][None, :], ffn_w_down[i].astype(BF16))
        xs = [x]

    y_prompt = _final_norm(x, final_norm_g[None, :], 0, geom.np_rows).reshape(b, s, D)
    y_sample = _final_norm(x, final_norm_g[None, :], geom.np_rows, geom.ns_rows).reshape(db, ds, D)
    return (y_prompt, y_sample,
            jnp.stack(na_k, axis=1), jnp.stack(na_v, axis=1),
            jnp.stack(gla_f, axis=1), jnp.stack(gla_b, axis=1),
            jnp.stack(gq_k, axis=1), jnp.stack(gq_v, axis=1))
```

```python
import functools

import numpy as np
import jax
import jax.numpy as jnp
from jax import lax
from jax.experimental import pallas as pl
from jax.experimental.pallas import tpu as pltpu

F32 = jnp.float32
BF16 = jnp.bfloat16

D = 1024
DEPTH = 4
HD = 64
GRID_W = 64
EPS = 1e-6
NA_HEADS = 8
NA_W = NA_HEADS * HD
NA_KR = 8
NA_KC = 16
GLA_HEADS = 4
GLA_DK = 64
GLA_DV = 128
GLA_KW = GLA_HEADS * GLA_DK
GLA_VW = GLA_HEADS * GLA_DV
GLA_RANK = 16
GLA_CHUNK = 64
GQA_HEADS = 16
GQA_KVH = 4
C_QW = GQA_HEADS * HD
C_KW = GQA_KVH * HD
ROPE_THETA = 10000.0
D_FF = 2816
AB_MAIN = 3 * NA_W + 2 * GLA_KW + 2 * GLA_VW

LANES = 128
SUBLANES = 8
TM = 256
TP = 512
LOG2E = 1.4426950408889634
HALO = 16
FF_CHUNK = 256
NA_QROWS = 8
NA_KROWS = NA_QROWS + NA_KR - 1
VMEM_LIMIT = 56 * 1024 * 1024


def _dot(a, b):
    return jnp.dot(a, b, preferred_element_type=F32)


def _dot_nt(a, b):
    return lax.dot_general(a, b, (((1,), (1,)), ((), ())), preferred_element_type=F32)


def _dot_tn(a, b):
    return lax.dot_general(a, b, (((0,), (0,)), ((), ())), preferred_element_type=F32)


def _split(a):
    hi = a.astype(BF16)
    lo = (a - hi.astype(F32)).astype(BF16)
    return hi, lo


def _dot_split(a, w_hi, w_lo):
    a_hi, a_lo = _split(a)
    return _dot(a_hi, w_hi) + _dot(a_lo, w_hi) + _dot(a_hi, w_lo)


def _norm_mod(x, g, sc, sh):
    xn = x * lax.rsqrt(jnp.mean(x * x, axis=-1, keepdims=True) + EPS)
    return (xn * g) * (1.0 + sc) + sh


def _silu(x):
    return x / (1.0 + jnp.exp(-x))


def _lane_lo(shape=(1, LANES)):
    return lax.broadcasted_iota(jnp.int32, shape, len(shape) - 1) < HD


def _stack_heads(x):
    lo = _lane_lo()
    zero = jnp.zeros_like(x)
    return jnp.concatenate([jnp.where(lo, x, zero), jnp.where(lo, zero, x)], axis=0)


def _params(sem):
    return pltpu.CompilerParams(dimension_semantics=sem, vmem_limit_bytes=VMEM_LIMIT)


def _mod_kernel(c_ref, w_ref, b_ref, o_ref):
    a = _silu(c_ref[...])
    w = w_ref[...]
    w_hi, w_lo = _split(w)
    o_ref[...] = _dot_split(a, w_hi, w_lo) + b_ref[...]


def _modulation(c_rows, ada_w, ada_b):
    nrow = c_rows.shape[0]
    tn = 1024
    return pl.pallas_call(
        _mod_kernel,
        out_shape=jax.ShapeDtypeStruct((DEPTH, nrow, 6 * D), F32),
        grid=(DEPTH, 6 * D // tn),
        in_specs=[pl.BlockSpec((nrow, D), lambda l, n: (0, 0)),
                  pl.BlockSpec((None, D, tn), lambda l, n: (l, 0, n)),
                  pl.BlockSpec((None, 1, tn), lambda l, n: (l, 0, n))],
        out_specs=pl.BlockSpec((None, nrow, tn), lambda l, n: (l, 0, n)),
        compiler_params=_params(("parallel", "parallel")),
        name="modulation",
    )(c_rows, ada_w, ada_b.reshape(DEPTH, 1, 6 * D))


class _Geom:
    def __init__(self, b, s, db, ds):
        assert s == TM and ds % TM == 0
        self.b, self.s, self.db, self.ds = b, s, db, ds
        self.np_rows = b * s
        self.ns_rows = db * ds
        self.n = self.np_rows + self.ns_rows
        self.npt = self.np_rows // TM
        self.tps = ds // TM
        self.ctx_row = db

    def mod_spec(self, k, tm=TM):
        assert self.np_rows % tm == 0 and self.ds % tm == 0
        npt, tps, ctx = self.np_rows // tm, self.ds // tm, self.ctx_row

        def imap(i):
            return (jnp.where(i < npt, ctx, (i - npt) // tps), 0, k)
        return pl.BlockSpec((None, 1, D), imap)

    def split_specs(self, tm, width):
        npt = self.np_rows // tm
        return [pl.BlockSpec((tm, width), lambda i: (jnp.minimum(i, npt - 1), 0)),
                pl.BlockSpec((tm, width), lambda i: (jnp.maximum(i - npt, 0), 0))]


def _read_split(p_ref, s_ref, npt):
    return jnp.where(pl.program_id(0) < npt, p_ref[...], s_ref[...])


def _const_spec(shape):
    nd = len(shape)
    return pl.BlockSpec(shape, lambda *_: (0,) * nd)


def _ab_in_kernel(*refs, split_npt):
    nx = 2 if split_npt else 1
    g_ref, sc_ref, sh_ref, w_ref, w2hi_ref, w2lo_ref, b2_ref, o_ref, gate_ref = refs[nx:]
    x = _read_split(refs[0], refs[1], split_npt) if split_npt else refs[0][...]
    h = _norm_mod(x, g_ref[...], sc_ref[...], sh_ref[...]).astype(BF16)
    cw = 512
    lr = _dot(h, w_ref[:, AB_MAIN:AB_MAIN + LANES])
    o_ref[:, 0:cw] = _dot(h, w_ref[:, 0:cw]).astype(BF16)
    z = _dot_split(lr, w2hi_ref[...], w2lo_ref[...]) + b2_ref[...]
    for j in range(1, AB_MAIN // cw):
        o_ref[:, j * cw:(j + 1) * cw] = _dot(h, w_ref[:, j * cw:(j + 1) * cw]).astype(BF16)
    gate_ref[...] = (jnp.minimum(z, 0.0) - jnp.log(1.0 + jnp.exp(-jnp.abs(z)))) * (1.0 / 16.0)


def _ab_in_proj(geom, xs, mod_l, g, w, w2hi, w2lo, b2):
    split = len(xs) == 2
    x_specs = geom.split_specs(TP, D) if split else [pl.BlockSpec((TP, D), lambda i: (i, 0))]
    return pl.pallas_call(
        functools.partial(_ab_in_kernel, split_npt=geom.np_rows // TP if split else 0),
        out_shape=(jax.ShapeDtypeStruct((geom.n, AB_MAIN), BF16),
                   jax.ShapeDtypeStruct((geom.n, 2 * GLA_KW), F32)),
        grid=(geom.n // TP,),
        in_specs=x_specs + [_const_spec((1, D)),
                            geom.mod_spec(1, TP), geom.mod_spec(0, TP),
                            _const_spec(w.shape), _const_spec(w2hi.shape), _const_spec(w2lo.shape),
                            _const_spec(b2.shape)],
        out_specs=(pl.BlockSpec((TP, AB_MAIN), lambda i: (i, 0)),
                   pl.BlockSpec((TP, 2 * GLA_KW), lambda i: (i, 0))),
        compiler_params=_params(("parallel",)),
        name="ab_in_proj",
    )(*xs, g, mod_l, mod_l, w, w2hi, w2lo, b2)


def _softmax_units(n_units, scores, values, emit):
    def finish(u, p_parts, l):
        emit(u, sum(_dot(p, v) for p, v in zip(p_parts, values(u))) / l)

    s_ahead = scores(0)
    pending = None
    for u in range(n_units):
        s_parts = s_ahead
        if u + 1 < n_units:
            s_ahead = scores(u + 1)
        if pending is not None:
            finish(u - 1, *pending)
        m = functools.reduce(jnp.maximum, [jnp.max(s, axis=-1, keepdims=True) for s in s_parts])
        p_parts = [jnp.exp2(s - m) for s in s_parts]
        l = sum(jnp.sum(p, axis=-1, keepdims=True) for p in p_parts)
        pending = ([p.astype(BF16) for p in p_parts], l)
    finish(n_units - 1, *pending)


def _head_of_pair(x, hh):
    lo = _lane_lo()
    zero = jnp.zeros_like(x)
    return jnp.where(lo, x, zero) if hh == 0 else jnp.where(lo, zero, x)


def _pair_attn_kernel(*refs, has_ctx, row_split, kv_share):
    if has_ctx:
        q_ref, k_ref, v_ref, kc_ref, vc_ref, o_ref = refs
    else:
        q_ref, k_ref, v_ref, o_ref = refs
    npairs = q_ref.shape[1] // LANES
    rows = q_ref.shape[0] // row_split
    units = [(m, r, hh) for m in range(npairs) for r in range(row_split) for hh in range(2)]
    outs = {}

    def kv_cols(m):
        c0 = (m // kv_share) * LANES
        return slice(c0, c0 + LANES)

    def scores(u):
        m, r, hh = units[u]
        qb = _head_of_pair(q_ref[r * rows:(r + 1) * rows, m * LANES:(m + 1) * LANES], hh)
        parts = [_dot_nt(qb, k_ref[:, kv_cols(m)])]
        if has_ctx:
            parts.append(_dot_nt(qb, kc_ref[...]))
        return parts

    def values(u):
        m = units[u][0]
        return [v_ref[:, kv_cols(m)]] + ([vc_ref[...]] if has_ctx else [])

    def emit(u, out):
        outs[units[u]] = out

    _softmax_units(len(units), scores, values, emit)
    lo = _lane_lo()
    for m in range(npairs):
        for r in range(row_split):
            o_ref[r * rows:(r + 1) * rows, m * LANES:(m + 1) * LANES] = jnp.where(
                lo, outs[(m, r, 0)], outs[(m, r, 1)]).astype(BF16)


def _na_block_rows(rb, rows):
    r0 = rb * NA_QROWS
    return r0, int(np.clip(r0 - NA_KR // 2, 0, rows - NA_KROWS))


def _na_kernel(q_ref, k_ref, v_ref, kc_ref, vc_ref, tab_even_ref, tab_odd_ref, rowmask_ref, o_ref):
    rows = q_ref.shape[0] // GRID_W
    nrb = rows // NA_QROWS
    nq = NA_QROWS * GRID_W
    nk = NA_KROWS * GRID_W
    units = [(rb, hh) for rb in range(nrb) for hh in range(2)]
    outs = {}

    def bias(rb, hh):
        r0, kstart = _na_block_rows(rb, rows)
        strips = []
        for qr in range(NA_QROWS):
            first = kstart - (r0 + qr) + 2 * (NA_KR - 1)
            tab_ref, blk = (tab_even_ref, first) if first % 2 == 0 else (tab_odd_ref, first - 1)
            strips.append(tab_ref[hh, :, blk * GRID_W:blk * GRID_W + nk] + rowmask_ref[rb, qr:qr + 1, :])
        return jnp.concatenate(strips, axis=0)

    def keys_at(rb):
        k0 = _na_block_rows(rb, rows)[1] * GRID_W
        return slice(k0, k0 + nk)

    def scores(u):
        rb, hh = units[u]
        qb = _head_of_pair(q_ref[rb * nq:(rb + 1) * nq, :], hh)
        return [_dot_nt(qb, k_ref[keys_at(rb), :]) + bias(rb, hh), _dot_nt(qb, kc_ref[...])]

    def values(u):
        return [v_ref[keys_at(units[u][0]), :], vc_ref[...]]

    def emit(u, out):
        outs[units[u]] = out

    _softmax_units(len(units), scores, values, emit)
    lo = _lane_lo()
    for rb in range(nrb):
        o_ref[rb * nq:(rb + 1) * nq, :] = jnp.where(lo, outs[(rb, 0)], outs[(rb, 1)]).astype(BF16)


NA_TAB_BLOCKS = 30


def _na_bias_tables(rpb, rows):
    ndr = 2 * NA_KR - 1
    col = np.arange(GRID_W)
    cs = np.clip(col - NA_KC // 2, 0, GRID_W - NA_KC)
    col_ok = (col[None, :] >= cs[:, None]) & (col[None, :] < cs[:, None] + NA_KC)
    dc = np.clip(col[None, :] - col[:, None] + NA_KC - 1, 0, 2 * NA_KC - 2)
    pick_dc = (np.arange(2 * NA_KC - 1)[:, None, None] == dc[None]).astype(np.float32)
    by_col = jnp.einsum('hdj,jqc->hqdc', rpb.astype(F32) * LOG2E, pick_dc, precision=lax.Precision.HIGHEST)
    by_col = jnp.where(col_ok[None, :, None, :], by_col, -jnp.inf)
    flat = by_col.reshape(NA_HEADS // 2, 2, GRID_W, ndr * GRID_W)
    lead = NA_KR - 1
    tail = NA_TAB_BLOCKS - ndr - lead

    def table(shift):
        return jnp.pad(flat, ((0, 0), (0, 0), (0, 0), ((lead - shift) * GRID_W, (tail + shift) * GRID_W)))
    nrb = rows // NA_QROWS
    row_mask = np.full((nrb, NA_QROWS, NA_KROWS, GRID_W), -np.inf, np.float32)
    for rb in range(nrb):
        r0, kstart = _na_block_rows(rb, rows)
        for qr in range(NA_QROWS):
            rs = int(np.clip(r0 + qr - NA_KR // 2, 0, rows - NA_KR))
            for kk in range(NA_KROWS):
                if rs <= kstart + kk < rs + NA_KR:
                    row_mask[rb, qr, kk] = 0.0
    return table(0), table(1), jnp.asarray(row_mask.reshape(nrb, NA_QROWS, NA_KROWS * GRID_W))


def _na_attention(geom, proj, k_ctx, v_ctx, rpb):
    npairs = NA_HEADS // 2
    s, ds = geom.s, geom.ds
    oa_p = pl.pallas_call(
        functools.partial(_pair_attn_kernel, has_ctx=False, row_split=1, kv_share=1),
        out_shape=jax.ShapeDtypeStruct((geom.np_rows, NA_W), BF16),
        grid=(geom.b,),
        in_specs=[pl.BlockSpec((s, NA_W), lambda b: (b, 0)),
                  pl.BlockSpec((s, NA_W), lambda b: (b, 1)),
                  pl.BlockSpec((s, NA_W), lambda b: (b, 2))],
        out_specs=pl.BlockSpec((s, NA_W), lambda b: (b, 0)),
        compiler_params=_params(("parallel",)),
        name="na_prompt_attn",
    )(proj, proj, proj)

    rows = ds // GRID_W
    assert rows % NA_QROWS == 0 and rows >= NA_KROWS
    tab_even, tab_odd, row_mask = _na_bias_tables(rpb, rows)
    s0 = geom.np_rows // ds
    kc = k_ctx.reshape(geom.db, npairs, 2, -1, HD).transpose(0, 1, 3, 2, 4).reshape(geom.db, npairs, -1, LANES).astype(BF16)
    vc = v_ctx.reshape(geom.db, npairs, 2, -1, HD).transpose(0, 1, 3, 2, 4).reshape(geom.db, npairs, -1, LANES).astype(BF16)
    nctx = kc.shape[2]
    tab_spec = pl.BlockSpec((None, 2, GRID_W, NA_TAB_BLOCKS * GRID_W), lambda p, b: (p, 0, 0, 0))
    oa_s = pl.pallas_call(
        _na_kernel,
        out_shape=jax.ShapeDtypeStruct((geom.ns_rows, NA_W), BF16),
        grid=(npairs, geom.db),
        in_specs=[pl.BlockSpec((ds, LANES), lambda p, b: (s0 + b, p)),
                  pl.BlockSpec((ds, LANES), lambda p, b: (s0 + b, npairs + p)),
                  pl.BlockSpec((ds, LANES), lambda p, b: (s0 + b, 2 * npairs + p)),
                  pl.BlockSpec((None, None, nctx, LANES), lambda p, b: (b, p, 0, 0)),
                  pl.BlockSpec((None, None, nctx, LANES), lambda p, b: (b, p, 0, 0)),
                  tab_spec, tab_spec, _const_spec(row_mask.shape)],
        out_specs=pl.BlockSpec((ds, LANES), lambda p, b: (b, p)),
        compiler_params=_params(("parallel", "parallel")),
        name="na_sample_attn",
    )(proj, proj, proj, kc, vc, tab_even, tab_odd, row_mask)
    return oa_p, oa_s


GLA_LEVELS = 6


def _gla_constants():
    c = GLA_CHUNK
    t = np.arange(c)
    i, j = t[:, None], t[None, :]
    out = {}
    for fwd in (True, False):
        tri = (j <= i) if fwd else (j >= i)
        masks = np.zeros((GLA_LEVELS + 1, c, c), np.float32)
        for lvl in range(GLA_LEVELS):
            half = c >> (lvl + 1)
            upper = (t % (2 * half)) >= half
            same = (i // (2 * half)) == (j // (2 * half))
            masks[lvl] = (same & upper[:, None] & ~upper[None, :]) if fwd else (same & ~upper[:, None] & upper[None, :])
        masks[GLA_LEVELS] = np.eye(c)
        out[fwd] = (jnp.asarray(tri.astype(np.float32), BF16),
                    jnp.asarray(np.concatenate([masks, masks], axis=1).reshape(-1, c), F32))
    return out


def _row_of_block(x, block, row):
    c = x.shape[0]
    if block >= SUBLANES:
        return jnp.concatenate(
            [jnp.broadcast_to(x[b * block + row:b * block + row + 1, :], (block, x.shape[1])) for b in range(c // block)],
            axis=0)
    per = SUBLANES // block
    sub = lax.broadcasted_iota(jnp.int32, x.shape, 0) % SUBLANES
    out = _row_of_block(x, SUBLANES, row)
    for p in range(1, per):
        out = jnp.where(sub >= p * block, _row_of_block(x, SUBLANES, p * block + row), out)
    return out


def _gla_cumulative(g, tri):
    g_hi, g_lo = _split(g)
    return _dot(tri, g_hi) + _dot(tri, g_lo)


def _gla_intra(q, k, g, b, masks, fwd):
    c = GLA_CHUNK
    t = lax.broadcasted_iota(jnp.int32, b.shape, 0)
    att = jnp.zeros((2 * c, c), F32)
    for lvl in range(GLA_LEVELS + 1):
        half = c >> (lvl + 1)
        if half >= 2:
            ref = _row_of_block(b, 2 * half, half - 1 if fwd else half)
            past_mid = ((t % (2 * half)) >= half) == fwd
            w = jnp.exp(jnp.where(past_mid, b - ref, ref - b))
            ql, kl = q * w, k * w
        elif half == 1:
            w = jnp.exp(jnp.where((t % 2 == 1) == fwd, g, 0.0))
            ql, kl = q * w, k * w
        else:
            ql, kl = q, k
        p = _dot_nt(_stack_heads(ql).astype(BF16), kl.astype(BF16))
        att = att + p * masks[lvl * 2 * c:(lvl + 1) * 2 * c]
    return att.astype(BF16)


def _gla_outputs(q, k, v, b, att, st, fwd):
    c = GLA_CHUNK
    edge = b[c - 1:c] if fwd else b[0:1]
    q_dec = q * jnp.exp(b)
    o_st = _dot_nt(_stack_heads(q_dec).astype(BF16), st.astype(BF16))
    r = _dot(att, v)
    o = jnp.concatenate([o_st[0:c] + r[0:c, 0:GLA_DV], o_st[c:] + r[c:, GLA_DV:]], axis=1)
    k_dec = (k * jnp.exp(edge - b)).astype(BF16)
    u = _dot_tn(v, k_dec)
    st_new = jnp.exp(edge) * st + jnp.where(_lane_lo(), u[0:GLA_DV], u[GLA_DV:])
    return o, st_new


GLA_GROUP = 4


def _gla_kernel(q_ref, k_ref, v_ref, rg_ref, gf_ref, gb_ref, s0f_ref, s0b_ref, gain_ref,
                tri_f_ref, masks_f_ref, tri_b_ref, masks_b_ref, y_ref, sf_ref, sb_ref,
                of_scr, ob_scr, st_scr):
    t = q_ref.shape[0]
    c = GLA_CHUNK
    nc = t // c
    assert nc % GLA_GROUP == 0
    for d, s0_ref in enumerate((s0f_ref, s0b_ref)):
        st_scr[d] = jnp.concatenate([s0_ref[0], s0_ref[1]], axis=0).T
    per_dir = ((gf_ref, of_scr, tri_f_ref, masks_f_ref), (gb_ref, ob_scr, tri_b_ref, masks_b_ref))

    def body(gi, carry):
        streams = []
        for u in range(GLA_GROUP):
            for d in range(2):
                ci = gi * GLA_GROUP + u
                cc = ci if d == 0 else nc - 1 - ci
                streams.append((d, pl.ds(pl.multiple_of(cc * c, c), c)))
        qkg = [(q_ref[rows, :].astype(F32), k_ref[rows, :].astype(F32), per_dir[d][0][rows, :])
               for d, rows in streams]
        cums = [_gla_cumulative(g, per_dir[d][2][...]) for (d, _), (_, _, g) in zip(streams, qkg)]
        atts = [_gla_intra(q, k, g, b, per_dir[d][3][...], d == 0)
                for (d, _), (q, k, g), b in zip(streams, qkg, cums)]
        for (d, rows), (q, k, _), b, att in zip(streams, qkg, cums, atts):
            o, st_new = _gla_outputs(q, k, v_ref[rows, :], b, att, st_scr[d], d == 0)
            per_dir[d][1][rows, :] = o
            st_scr[d] = st_new
        return carry

    lax.fori_loop(0, nc // GLA_GROUP, body, 0)

    for d, s_ref in enumerate((sf_ref, sb_ref)):
        s_pair = st_scr[d].T
        s_ref[0] = s_pair[0:GLA_DK]
        s_ref[1] = s_pair[GLA_DK:]

    rt = 256
    for r in range(t // rt):
        og = of_scr[r * rt:(r + 1) * rt, :] + ob_scr[r * rt:(r + 1) * rt, :]
        halves = []
        for hh in range(2):
            x = og[:, hh * GLA_DV:(hh + 1) * GLA_DV]
            halves.append(x * lax.rsqrt(jnp.mean(x * x, axis=-1, keepdims=True) + EPS))
        y = jnp.concatenate(halves, axis=1) * gain_ref[...] * _silu(rg_ref[r * rt:(r + 1) * rt, :].astype(F32))
        y_ref[r * rt:(r + 1) * rt, :] = y.astype(BF16)


def _gla_call(proj, gates, s0f, s0b, gain, consts, t, nb, row0, name):
    npairs = GLA_HEADS // 2
    qcol = 3 * NA_W // LANES
    kcol = qcol + GLA_KW // LANES
    vcol = (3 * NA_W + 2 * GLA_KW) // (2 * GLA_DV)
    rcol = vcol + GLA_VW // (2 * GLA_DV)
    tri_f, masks_f = consts[True]
    tri_b, masks_b = consts[False]
    st_spec = pl.BlockSpec((None, 2, GLA_DK, GLA_DV), lambda b, p: (b, p, 0, 0))
    return pl.pallas_call(
        _gla_kernel,
        out_shape=(jax.ShapeDtypeStruct((nb * t, GLA_VW), BF16),
                   jax.ShapeDtypeStruct((nb, GLA_HEADS, GLA_DK, GLA_DV), F32),
                   jax.ShapeDtypeStruct((nb, GLA_HEADS, GLA_DK, GLA_DV), F32)),
        grid=(nb, npairs),
        in_specs=[pl.BlockSpec((t, LANES), lambda b, p: (row0 + b, qcol + p)),
                  pl.BlockSpec((t, LANES), lambda b, p: (row0 + b, kcol + p)),
                  pl.BlockSpec((t, 2 * GLA_DV), lambda b, p: (row0 + b, vcol + p)),
                  pl.BlockSpec((t, 2 * GLA_DV), lambda b, p: (row0 + b, rcol + p)),
                  pl.BlockSpec((t, LANES), lambda b, p: (row0 + b, p)),
                  pl.BlockSpec((t, LANES), lambda b, p: (row0 + b, npairs + p)),
                  st_spec, st_spec,
                  pl.BlockSpec((1, 2 * GLA_DV), lambda b, p: (0, p)),
                  _const_spec(tri_f.shape), _const_spec(masks_f.shape),
                  _const_spec(tri_b.shape), _const_spec(masks_b.shape)],
        out_specs=(pl.BlockSpec((t, 2 * GLA_DV), lambda b, p: (b, p)), st_spec, st_spec),
        scratch_shapes=[pltpu.VMEM((t, 2 * GLA_DV), F32), pltpu.VMEM((t, 2 * GLA_DV), F32),
                        pltpu.VMEM((2, GLA_DV, LANES), F32)],
        compiler_params=_params(("parallel", "parallel")),
        name=name,
    )(proj, proj, proj, proj, gates, gates, s0f, s0b, gain, tri_f, masks_f, tri_b, masks_b)


def _out_proj_kernel(*refs, n_act, split_npt, x_split):
    npt = split_npt
    nx = 2 if x_split else 1
    act_refs = refs[:2 * n_act]
    w_ref = refs[2 * n_act]
    x_refs = refs[2 * n_act + 1:2 * n_act + 1 + nx]
    gate_ref, o_ref = refs[2 * n_act + 1 + nx:]
    acc = None
    k0 = 0
    for a in range(n_act):
        act = _read_split(act_refs[2 * a], act_refs[2 * a + 1], npt)
        kw = act.shape[1]
        part = _dot(act, w_ref[k0:k0 + kw, :])
        acc = part if acc is None else acc + part
        k0 += kw
    x = _read_split(x_refs[0], x_refs[1], npt) if x_split else x_refs[0][...]
    o_ref[...] = x + gate_ref[...] * acc


def _out_proj(geom, act_pairs, w, xs, mod_l, gate_k):
    x_split = len(xs) == 2
    in_specs, args = [], []
    for a_p, a_s in act_pairs:
        in_specs += geom.split_specs(TP, a_p.shape[1])
        args += [a_p, a_s]
    in_specs.append(_const_spec(w.shape))
    in_specs += geom.split_specs(TP, D) if x_split else [pl.BlockSpec((TP, D), lambda i: (i, 0))]
    in_specs.append(geom.mod_spec(gate_k, TP))
    return pl.pallas_call(
        functools.partial(_out_proj_kernel, n_act=len(act_pairs), split_npt=geom.np_rows // TP, x_split=x_split),
        out_shape=jax.ShapeDtypeStruct((geom.n, D), F32),
        grid=(geom.n // TP,),
        in_specs=in_specs,
        out_specs=pl.BlockSpec((TP, D), lambda i: (i, 0)),
        compiler_params=_params(("parallel",)),
        name="out_proj",
    )(*args, w, *xs, mod_l)


C_OUT = C_QW + 4 * C_KW


def _c_in_kernel(x_ref, g_ref, sc_ref, sh_ref, w_ref, gain_ref, cos_ref, sin_ref, grp_ref, o_ref):
    h = _norm_mod(x_ref[...], g_ref[...], sc_ref[...], sh_ref[...]).astype(BF16)
    cw = 512
    nch = C_OUT // cw
    n_norm = (C_QW + 2 * C_KW) // LANES
    quarter = HD // 4
    first_half = lax.broadcasted_iota(jnp.int32, (1, LANES), 1) % (2 * quarter) < quarter
    y_ahead = _dot(h, w_ref[:, 0:cw])
    for j in range(nch):
        y = y_ahead
        if j + 1 < nch:
            y_ahead = _dot(h, w_ref[:, (j + 1) * cw:(j + 2) * cw])
        for s in range(cw // LANES):
            blk = j * (cw // LANES) + s
            yb = y[:, s * LANES:(s + 1) * LANES]
            if blk < n_norm:
                ms = _dot((yb * yb).astype(BF16), grp_ref[...])
                yn = yb * lax.rsqrt(ms + EPS) * gain_ref[:, blk * LANES:(blk + 1) * LANES]
                partner = jnp.where(first_half, pltpu.roll(yn, LANES - quarter, 1), pltpu.roll(yn, quarter, 1))
                yb = yn * cos_ref[...] + partner * sin_ref[...]
            o_ref[:, blk * LANES:(blk + 1) * LANES] = yb.astype(BF16)


def _rope_tables(geom):
    tpos = np.arange(geom.ds)
    inv = ROPE_THETA ** (-np.arange(0, HD // 2, 2, dtype=np.float32) / (HD // 2))
    ang_r = (tpos // GRID_W).astype(np.float32)[:, None] * inv
    ang_c = (tpos % GRID_W).astype(np.float32)[:, None] * inv
    ang_r, ang_c = jnp.asarray(ang_r, F32), jnp.asarray(ang_c, F32)
    cos = jnp.concatenate([jnp.cos(ang_r)] * 2 + [jnp.cos(ang_c)] * 2, axis=1)
    sin = jnp.concatenate([-jnp.sin(ang_r), jnp.sin(ang_r), -jnp.sin(ang_c), jnp.sin(ang_c)], axis=1)
    cos = jnp.concatenate([jnp.tile(cos, (1, 2)), jnp.ones((TP, LANES), F32)], axis=0)
    sin = jnp.concatenate([jnp.tile(sin, (1, 2)), jnp.zeros((TP, LANES), F32)], axis=0)
    lane = np.arange(LANES)
    grp = (lane[:, None] // HD == lane[None, :] // HD).astype(np.float32) / HD
    return cos, sin, jnp.asarray(grp, BF16)


def _c_in_proj(geom, x, mod_l, g, w, gain, tables):
    cos, sin, grp = tables
    npt, tps = geom.np_rows // TP, geom.ds // TP

    def pos_map(i):
        return (jnp.where(i < npt, tps, (i - npt) % tps), 0)
    return pl.pallas_call(
        _c_in_kernel,
        out_shape=jax.ShapeDtypeStruct((geom.n, C_OUT), BF16),
        grid=(geom.n // TP,),
        in_specs=[pl.BlockSpec((TP, D), lambda i: (i, 0)),
                  _const_spec((1, D)),
                  geom.mod_spec(1, TP), geom.mod_spec(0, TP),
                  _const_spec(w.shape), _const_spec(gain.shape),
                  pl.BlockSpec((TP, LANES), pos_map), pl.BlockSpec((TP, LANES), pos_map),
                  _const_spec(grp.shape)],
        out_specs=pl.BlockSpec((TP, C_OUT), lambda i: (i, 0)),
        compiler_params=_params(("parallel",)),
        name="gqa_in_proj",
    )(x, g, mod_l, mod_l, w, gain, cos, sin, grp)


def _gqa_attention(geom, proj, k_ctx, v_ctx):
    nqp = GQA_HEADS // 2
    kcol = C_QW // LANES
    vcol = kcol + 2 * C_KW // LANES
    s, ds = geom.s, geom.ds
    kvw = 2 * C_KW
    o_p = pl.pallas_call(
        functools.partial(_pair_attn_kernel, has_ctx=False, row_split=1, kv_share=2),
        out_shape=jax.ShapeDtypeStruct((geom.np_rows, C_QW), BF16),
        grid=(geom.b,),
        in_specs=[pl.BlockSpec((s, C_QW), lambda b: (b, 0)),
                  pl.BlockSpec((s, kvw), lambda b: (b, C_QW // kvw)),
                  pl.BlockSpec((s, kvw), lambda b: (b, C_QW // kvw + 1))],
        out_specs=pl.BlockSpec((s, C_QW), lambda b: (b, 0)),
        compiler_params=_params(("parallel",)),
        name="gqa_prompt_attn",
    )(proj, proj, proj)

    tq = 512
    nqt = ds // tq
    q0 = geom.np_rows // tq
    s0 = geom.np_rows // ds
    kc = jnp.concatenate([k_ctx, k_ctx], axis=-1).astype(BF16)
    vc = jnp.concatenate([v_ctx, v_ctx], axis=-1).astype(BF16)
    nctx = kc.shape[2]
    o_s = pl.pallas_call(
        functools.partial(_pair_attn_kernel, has_ctx=True, row_split=2, kv_share=1),
        out_shape=jax.ShapeDtypeStruct((geom.ns_rows, C_QW), BF16),
        grid=(geom.db, nqp, nqt),
        in_specs=[pl.BlockSpec((tq, LANES), lambda b, m, t: (q0 + nqt * b + t, m)),
                  pl.BlockSpec((ds, LANES), lambda b, m, t: (s0 + b, kcol + m // 2)),
                  pl.BlockSpec((ds, LANES), lambda b, m, t: (s0 + b, vcol + m // 2)),
                  pl.BlockSpec((None, None, nctx, LANES), lambda b, m, t: (b, m // 2, 0, 0)),
                  pl.BlockSpec((None, None, nctx, LANES), lambda b, m, t: (b, m // 2, 0, 0))],
        out_specs=pl.BlockSpec((tq, LANES), lambda b, m, t: (nqt * b + t, m)),
        compiler_params=_params(("parallel", "parallel", "parallel")),
        name="gqa_sample_attn",
    )(proj, proj, proj, kc, vc)
    return o_p, o_s


def _ffn_kernel(x_ref, xp_ref, xn_ref, g_ref, sc_ref, sh_ref, gate_ref, wup_ref, cw_ref, cb_ref, wdn_ref,
                o_ref, acc_ref, *, npt, tps):
    i = pl.program_id(0)
    j = jnp.maximum(i - npt, 0) % tps
    is_prompt = i < npt
    has_prev = jnp.logical_and(jnp.logical_not(is_prompt), j > 0).astype(F32)
    has_next = jnp.logical_and(jnp.logical_not(is_prompt), j < tps - 1).astype(F32)
    g, sc, sh = g_ref[...], sc_ref[...], sh_ref[...]
    x = x_ref[...]
    h_ext = jnp.concatenate([_norm_mod(xp_ref[...], g, sc, sh) * has_prev,
                             _norm_mod(x, g, sc, sh),
                             _norm_mod(xn_ref[...], g, sc, sh) * has_next], axis=0).astype(BF16)
    rows = TM + 2 * HALO
    nch = D_FF // FF_CHUNK

    def up(c):
        return [_dot(h_ext, wup_ref[:, off + c * FF_CHUNK:off + (c + 1) * FF_CHUNK]) for off in (0, D_FF)]

    def conv(u, off, c):
        cols = slice(off + c * FF_CHUNK, off + (c + 1) * FF_CHUNK)
        u_prev = pltpu.roll(u, 1, 0)[HALO:HALO + TM]
        u_next = pltpu.roll(u, rows - 1, 0)[HALO:HALO + TM]
        return (cw_ref[0:1, cols] * u_prev + cw_ref[1:2, cols] * u[HALO:HALO + TM]
                + cw_ref[2:3, cols] * u_next + cb_ref[:, cols])

    def down(c, a):
        part = _dot(a, wdn_ref[c * FF_CHUNK:(c + 1) * FF_CHUNK, :])
        if c == 0:
            acc_ref[...] = part
        else:
            acc_ref[...] += part

    u_ahead = up(0)
    a_prev = None
    for c in range(nch):
        u_val, u_gate = u_ahead
        if c + 1 < nch:
            u_ahead = up(c + 1)
        if a_prev is not None:
            down(c - 1, a_prev)
        a_prev = (_silu(conv(u_gate, D_FF, c)) * conv(u_val, 0, c)).astype(BF16)
    down(nch - 1, a_prev)
    o_ref[...] = x + gate_ref[...] * acc_ref[...]


def _conv_ffn(geom, x, mod_l, g, wup, cw, cb, wdn):
    nblk = geom.n // HALO
    per = TM // HALO
    return pl.pallas_call(
        functools.partial(_ffn_kernel, npt=geom.npt, tps=geom.tps),
        out_shape=jax.ShapeDtypeStruct((geom.n, D), F32),
        grid=(geom.n // TM,),
        in_specs=[pl.BlockSpec((TM, D), lambda i: (i, 0)),
                  pl.BlockSpec((HALO, D), lambda i: (jnp.maximum(i * per - 1, 0), 0)),
                  pl.BlockSpec((HALO, D), lambda i: (jnp.minimum((i + 1) * per, nblk - 1), 0)),
                  _const_spec((1, D)),
                  geom.mod_spec(4), geom.mod_spec(3), geom.mod_spec(5),
                  _const_spec(wup.shape), _const_spec(cw.shape), _const_spec(cb.shape), _const_spec(wdn.shape)],
        out_specs=pl.BlockSpec((TM, D), lambda i: (i, 0)),
        scratch_shapes=[pltpu.VMEM((TM, D), F32)],
        compiler_params=_params(("parallel",)),
        name="conv_ffn",
    )(x, x, x, g, mod_l, mod_l, mod_l, wup, cw, cb, wdn)


def _final_norm_kernel(x_ref, g_ref, o_ref):
    x = x_ref[...]
    o_ref[...] = x * lax.rsqrt(jnp.mean(x * x, axis=-1, keepdims=True) + EPS) * g_ref[...]


def _final_norm(x, g, row0, nrows):
    t0 = row0 // TM
    return pl.pallas_call(
        _final_norm_kernel,
        out_shape=jax.ShapeDtypeStruct((nrows, D), F32),
        grid=(nrows // TM,),
        in_specs=[pl.BlockSpec((TM, D), lambda i: (t0 + i, 0)), _const_spec((1, D))],
        out_specs=pl.BlockSpec((TM, D), lambda i: (i, 0)),
        compiler_params=_params(("parallel",)),
        name="final_norm",
    )(x, g)


def _prep_ab(w_in, w2_f, b2_f, w2_b, b2_b):
    scale = np.ones((w_in.shape[1],), np.float32)
    scale[0:NA_W] = HD ** -0.5 * LOG2E
    scale[3 * NA_W:3 * NA_W + GLA_KW] = GLA_DK ** -0.5
    w = jnp.pad(w_in * scale, ((0, 0), (0, AB_MAIN + LANES - w_in.shape[1]))).astype(BF16)
    w2 = jnp.zeros((LANES, 2 * GLA_KW), F32)
    w2 = w2.at[0:GLA_RANK, 0:GLA_KW].set(w2_f).at[GLA_RANK:2 * GLA_RANK, GLA_KW:].set(w2_b)
    w2_hi, w2_lo = _split(w2)
    b2 = jnp.concatenate([b2_f, b2_b])[None, :]
    return w, w2_hi, w2_lo, b2


def _prep_c(w_in, qn_g, kn_g):
    wq, wk, wv = w_in[:, :C_QW], w_in[:, C_QW:C_QW + C_KW], w_in[:, C_QW + C_KW:]

    def dup(w):
        return jnp.broadcast_to(w.reshape(D, GQA_KVH, 1, HD), (D, GQA_KVH, 2, HD)).reshape(D, 2 * C_KW)
    w = jnp.concatenate([wq, dup(wk), dup(wv)], axis=1).astype(BF16)
    gain = jnp.concatenate([jnp.tile(qn_g * (HD ** -0.5 * LOG2E), GQA_HEADS), jnp.tile(kn_g, 2 * GQA_KVH)])[None, :]
    return w, gain


def kernel(x_prompt, x_sample, cache_na_k, cache_na_v, state_gla_fwd, state_gla_bwd, cache_gqa_k, cache_gqa_v, c, c_ctx, ada_w, ada_b, norm_mix_g, norm_ffn_g, ab_w_in, ab_w_out, na_rpb, gla_w2_fwd, gla_b2_fwd, gla_w2_bwd, gla_b2_bwd, gla_norm_g, gqa_w_in, gqa_w_out, gqa_q_norm_g, gqa_k_norm_g, ffn_w_up, ffn_conv_w, ffn_conv_b, ffn_w_down, final_norm_g):
    b, s, _ = x_prompt.shape
    db, ds, _ = x_sample.shape
    geom = _Geom(b, s, db, ds)
    assert ds % GRID_W == 0 and geom.np_rows % ds == 0 and geom.np_rows % TP == 0

    xs = [x_prompt.reshape(-1, D), x_sample.reshape(-1, D)]
    nrow = -(-(db + 1) // SUBLANES) * SUBLANES
    c_rows = jnp.zeros((nrow, D), F32).at[:db].set(c).at[db].set(c_ctx)
    mod = _modulation(c_rows, ada_w, ada_b).reshape(DEPTH, nrow, 1, 6 * D)

    gla_consts = _gla_constants()
    rope = _rope_tables(geom)
    zeros_state = jnp.zeros((b, GLA_HEADS, GLA_DK, GLA_DV), F32)
    na_k, na_v, gla_f, gla_b, gq_k, gq_v = [], [], [], [], [], []

    for i in range(DEPTH):
        j = i // 2
        mod_l = mod[i]
        if i % 2 == 0:
            w, w2_hi, w2_lo, b2 = _prep_ab(ab_w_in[j], gla_w2_fwd[j], gla_b2_fwd[j], gla_w2_bwd[j], gla_b2_bwd[j])
            proj, gates = _ab_in_proj(geom, xs, mod_l, norm_mix_g[i][None, :], w, w2_hi, w2_lo, b2)
            oa = _na_attention(geom, proj, cache_na_k[:, j], cache_na_v[:, j], na_rpb[j])
            gain = gla_norm_g[j][None, :]
            yg_p, sf, sb = _gla_call(proj, gates, zeros_state, zeros_state, gain, gla_consts, s, b, 0, "gla_prompt")
            yg_s, _, _ = _gla_call(proj, gates, state_gla_fwd[:, j], state_gla_bwd[:, j], gain, gla_consts,
                                   ds, db, geom.np_rows // ds, "gla_sample")
            x = _out_proj(geom, [oa, (yg_p, yg_s)], ab_w_out[j].astype(BF16), xs, mod_l, 2)
            pk = proj[:geom.np_rows].reshape(b, s, -1)
            na_k.append(pk[..., NA_W:2 * NA_W].reshape(b, s, NA_HEADS, HD).transpose(0, 2, 1, 3).astype(F32))
            na_v.append(pk[..., 2 * NA_W:3 * NA_W].reshape(b, s, NA_HEADS, HD).transpose(0, 2, 1, 3).astype(F32))
            gla_f.append(sf)
            gla_b.append(sb)
        else:
            w, gain = _prep_c(gqa_w_in[j], gqa_q_norm_g[j], gqa_k_norm_g[j])
            proj = _c_in_proj(geom, xs[0], mod_l, norm_mix_g[i][None, :], w, gain, rope)
            o = _gqa_attention(geom, proj, cache_gqa_k[:, j], cache_gqa_v[:, j])
            x = _out_proj(geom, [o], gqa_w_out[j].astype(BF16), xs, mod_l, 2)
            pk = proj[:geom.np_rows].reshape(b, s, -1)
            kd = pk[..., C_QW:C_QW + 2 * C_KW].reshape(b, s, GQA_KVH, 2, HD)[:, :, :, 0]
            vd = pk[..., C_QW + 2 * C_KW:].reshape(b, s, GQA_KVH, 2, HD)[:, :, :, 0]
            gq_k.append(kd.transpose(0, 2, 1, 3).astype(F32))
            gq_v.append(vd.transpose(0, 2, 1, 3).astype(F32))
        x = _conv_ffn(geom, x, mod_l, norm_ffn_g[i][None, :], ffn_w_up[i].astype(BF16), ffn_conv_w[i],
                      ffn_conv_b[i][None, :], ffn_w_down[i].astype(BF16))
        xs = [x]

    y_prompt = _final_norm(x, final_norm_g[None, :], 0, geom.np_rows).reshape(b, s, D)
    y_sample = _final_norm(x, final_norm_g[None, :], geom.np_rows, geom.ns_rows).reshape(db, ds, D)
    return (y_prompt, y_sample,
            jnp.stack(na_k, axis=1), jnp.stack(na_v, axis=1),
            jnp.stack(gla_f, axis=1), jnp.stack(gla_b, axis=1),
            jnp.stack(gq_k, axis=1), jnp.stack(gq_v, axis=1))
```

```python
import functools

import numpy as np
import jax
import jax.numpy as jnp
from jax import lax
from jax.experimental import pallas as pl
from jax.experimental.pallas import tpu as pltpu

F32 = jnp.float32
BF16 = jnp.bfloat16

D = 1024
DEPTH = 4
HD = 64
GRID_W = 64
EPS = 1e-6
NA_HEADS = 8
NA_W = NA_HEADS * HD
NA_KR = 8
NA_KC = 16
GLA_HEADS = 4
GLA_DK = 64
GLA_DV = 128
GLA_KW = GLA_HEADS * GLA_DK
GLA_VW = GLA_HEADS * GLA_DV
GLA_RANK = 16
GLA_CHUNK = 64
GQA_HEADS = 16
GQA_KVH = 4
C_QW = GQA_HEADS * HD
C_KW = GQA_KVH * HD
ROPE_THETA = 10000.0
D_FF = 2816
AB_MAIN = 3 * NA_W + 2 * GLA_KW + 2 * GLA_VW

LANES = 128
SUBLANES = 8
TM = 256
TP = 512
TF = 256
LOG2E = 1.4426950408889634
HALO = 16
FF_CHUNK = 256
NA_QROWS = 8
NA_KROWS = NA_QROWS + NA_KR - 1
VMEM_LIMIT = 56 * 1024 * 1024


def _dot(a, b):
    return jnp.dot(a, b, preferred_element_type=F32)


def _dot_nt(a, b):
    return lax.dot_general(a, b, (((1,), (1,)), ((), ())), preferred_element_type=F32)


def _dot_tn(a, b):
    return lax.dot_general(a, b, (((0,), (0,)), ((), ())), preferred_element_type=F32)


def _split(a):
    hi = a.astype(BF16)
    lo = (a - hi.astype(F32)).astype(BF16)
    return hi, lo


def _dot_split(a, w_hi, w_lo):
    a_hi, a_lo = _split(a)
    return _dot(a_hi, w_hi) + _dot(a_lo, w_hi) + _dot(a_hi, w_lo)


def _norm_mod(x, g, sc, sh):
    xn = x * lax.rsqrt(jnp.mean(x * x, axis=-1, keepdims=True) + EPS)
    return (xn * g) * (1.0 + sc) + sh


def _silu(x):
    return x / (1.0 + jnp.exp(-x))


def _lane_lo(shape=(1, LANES)):
    return lax.broadcasted_iota(jnp.int32, shape, len(shape) - 1) < HD


def _stack_heads(x):
    lo = _lane_lo()
    zero = jnp.zeros_like(x)
    return jnp.concatenate([jnp.where(lo, x, zero), jnp.where(lo, zero, x)], axis=0)


def _params(sem):
    return pltpu.CompilerParams(dimension_semantics=sem, vmem_limit_bytes=VMEM_LIMIT)


def _mod_kernel(c_ref, w_ref, b_ref, o_ref):
    a = _silu(c_ref[...])
    w = w_ref[...]
    w_hi, w_lo = _split(w)
    o_ref[...] = _dot_split(a, w_hi, w_lo) + b_ref[...]


def _modulation(c_rows, ada_w, ada_b):
    nrow = c_rows.shape[0]
    tn = 1024
    return pl.pallas_call(
        _mod_kernel,
        out_shape=jax.ShapeDtypeStruct((DEPTH, nrow, 6 * D), F32),
        grid=(DEPTH, 6 * D // tn),
        in_specs=[pl.BlockSpec((nrow, D), lambda l, n: (0, 0)),
                  pl.BlockSpec((None, D, tn), lambda l, n: (l, 0, n)),
                  pl.BlockSpec((None, 1, tn), lambda l, n: (l, 0, n))],
        out_specs=pl.BlockSpec((None, nrow, tn), lambda l, n: (l, 0, n)),
        compiler_params=_params(("parallel", "parallel")),
        name="modulation",
    )(c_rows, ada_w, ada_b.reshape(DEPTH, 1, 6 * D))


class _Geom:
    def __init__(self, b, s, db, ds):
        assert s == TM and ds % TM == 0
        self.b, self.s, self.db, self.ds = b, s, db, ds
        self.np_rows = b * s
        self.ns_rows = db * ds
        self.n = self.np_rows + self.ns_rows
        self.npt = self.np_rows // TM
        self.tps = ds // TM
        self.ctx_row = db

    def mod_spec(self, k, tm=TM):
        assert self.np_rows % tm == 0 and self.ds % tm == 0
        npt, tps, ctx = self.np_rows // tm, self.ds // tm, self.ctx_row

        def imap(i):
            return (jnp.where(i < npt, ctx, (i - npt) // tps), 0, k)
        return pl.BlockSpec((None, 1, D), imap)

    def split_specs(self, tm, width):
        npt = self.np_rows // tm
        return [pl.BlockSpec((tm, width), lambda i: (jnp.minimum(i, npt - 1), 0)),
                pl.BlockSpec((tm, width), lambda i: (jnp.maximum(i - npt, 0), 0))]


def _read_split(p_ref, s_ref, npt):
    return jnp.where(pl.program_id(0) < npt, p_ref[...], s_ref[...])


def _const_spec(shape):
    nd = len(shape)
    return pl.BlockSpec(shape, lambda *_: (0,) * nd)


def _ab_in_kernel(*refs, split_npt):
    nx = 2 if split_npt else 1
    g_ref, sc_ref, sh_ref, w_ref, w2hi_ref, w2lo_ref, b2_ref, o_ref, gate_ref = refs[nx:]
    x = _read_split(refs[0], refs[1], split_npt) if split_npt else refs[0][...]
    h = _norm_mod(x, g_ref[...], sc_ref[...], sh_ref[...]).astype(BF16)
    cw = 512
    lr = _dot(h, w_ref[:, AB_MAIN:AB_MAIN + LANES])
    o_ref[:, 0:cw] = _dot(h, w_ref[:, 0:cw]).astype(BF16)
    z = _dot_split(lr, w2hi_ref[...], w2lo_ref[...]) + b2_ref[...]
    for j in range(1, AB_MAIN // cw):
        o_ref[:, j * cw:(j + 1) * cw] = _dot(h, w_ref[:, j * cw:(j + 1) * cw]).astype(BF16)
    gate_ref[...] = (jnp.minimum(z, 0.0) - jnp.log(1.0 + jnp.exp(-jnp.abs(z)))) * (1.0 / 16.0)


def _ab_in_proj(geom, xs, mod_l, g, w, w2hi, w2lo, b2):
    split = len(xs) == 2
    x_specs = geom.split_specs(TP, D) if split else [pl.BlockSpec((TP, D), lambda i: (i, 0))]
    return pl.pallas_call(
        functools.partial(_ab_in_kernel, split_npt=geom.np_rows // TP if split else 0),
        out_shape=(jax.ShapeDtypeStruct((geom.n, AB_MAIN), BF16),
                   jax.ShapeDtypeStruct((geom.n, 2 * GLA_KW), F32)),
        grid=(geom.n // TP,),
        in_specs=x_specs + [_const_spec((1, D)),
                            geom.mod_spec(1, TP), geom.mod_spec(0, TP),
                            _const_spec(w.shape), _const_spec(w2hi.shape), _const_spec(w2lo.shape),
                            _const_spec(b2.shape)],
        out_specs=(pl.BlockSpec((TP, AB_MAIN), lambda i: (i, 0)),
                   pl.BlockSpec((TP, 2 * GLA_KW), lambda i: (i, 0))),
        compiler_params=_params(("parallel",)),
        name="ab_in_proj",
    )(*xs, g, mod_l, mod_l, w, w2hi, w2lo, b2)


def _softmax_units(n_units, scores, values, emit):
    def finish(u, p_parts, l):
        emit(u, sum(_dot(p, v) for p, v in zip(p_parts, values(u))) / l)

    s_ahead = scores(0)
    pending = None
    for u in range(n_units):
        s_parts = s_ahead
        if u + 1 < n_units:
            s_ahead = scores(u + 1)
        if pending is not None:
            finish(u - 1, *pending)
        m = functools.reduce(jnp.maximum, [jnp.max(s, axis=-1, keepdims=True) for s in s_parts])
        p_parts = [jnp.exp2(s - m) for s in s_parts]
        l = sum(jnp.sum(p, axis=-1, keepdims=True) for p in p_parts)
        pending = ([p.astype(BF16) for p in p_parts], l)
    finish(n_units - 1, *pending)


def _head_of_pair(x, hh):
    lo = _lane_lo()
    zero = jnp.zeros_like(x)
    return jnp.where(lo, x, zero) if hh == 0 else jnp.where(lo, zero, x)


def _pair_attn_kernel(*refs, has_ctx, row_split, kv_share):
    if has_ctx:
        q_ref, k_ref, v_ref, kc_ref, vc_ref, o_ref = refs
    else:
        q_ref, k_ref, v_ref, o_ref = refs
    npairs = q_ref.shape[1] // LANES
    rows = q_ref.shape[0] // row_split
    units = [(m, r, hh) for m in range(npairs) for r in range(row_split) for hh in range(2)]
    outs = {}

    def kv_cols(m):
        c0 = (m // kv_share) * LANES
        return slice(c0, c0 + LANES)

    def scores(u):
        m, r, hh = units[u]
        qb = _head_of_pair(q_ref[r * rows:(r + 1) * rows, m * LANES:(m + 1) * LANES], hh)
        parts = [_dot_nt(qb, k_ref[:, kv_cols(m)])]
        if has_ctx:
            parts.append(_dot_nt(qb, kc_ref[...]))
        return parts

    def values(u):
        m = units[u][0]
        return [v_ref[:, kv_cols(m)]] + ([vc_ref[...]] if has_ctx else [])

    def emit(u, out):
        outs[units[u]] = out

    _softmax_units(len(units), scores, values, emit)
    lo = _lane_lo()
    for m in range(npairs):
        for r in range(row_split):
            o_ref[r * rows:(r + 1) * rows, m * LANES:(m + 1) * LANES] = jnp.where(
                lo, outs[(m, r, 0)], outs[(m, r, 1)]).astype(BF16)


def _shared_kv_attn_kernel(*refs, has_ctx, row_split, kv_share):
    if has_ctx:
        q_ref, k_ref, v_ref, kc_ref, vc_ref, o_ref = refs
    else:
        q_ref, k_ref, v_ref, o_ref = refs
    npairs = q_ref.shape[1] // LANES
    rows = q_ref.shape[0] // row_split
    units = [(m, r, hh) for m in range(npairs) for r in range(row_split) for hh in range(2)]
    outs = {}

    def kv_cols(m):
        c0 = (m // kv_share) * LANES
        return slice(c0, c0 + LANES)

    def values_t(ref, cols):
        vt = ref[:, cols].astype(F32).T
        return jnp.where(lax.broadcasted_iota(jnp.int32, vt.shape, 0) < HD, vt, 1.0).astype(BF16)

    vt_cache = {}

    def values(u):
        m = units[u][0]
        key = m // kv_share
        if key not in vt_cache:
            vt_cache[key] = [values_t(v_ref, kv_cols(m))] + ([values_t(vc_ref, slice(None))] if has_ctx else [])
        return vt_cache[key]

    def scores(u):
        m, r, hh = units[u]
        qb = _head_of_pair(q_ref[r * rows:(r + 1) * rows, m * LANES:(m + 1) * LANES], hh)
        parts = [_dot_nt(k_ref[:, kv_cols(m)], qb)]
        if has_ctx:
            parts.append(_dot_nt(kc_ref[...], qb))
        return parts

    def finish(u, p_parts):
        ext = sum(_dot(vt, p) for vt, p in zip(values(u), p_parts))
        outs[units[u]] = ext[0:HD] / ext[HD:HD + 1]

    s_ahead = scores(0)
    pending = None
    for u in range(len(units)):
        s_parts = s_ahead
        if u + 1 < len(units):
            s_ahead = scores(u + 1)
        if pending is not None:
            finish(u - 1, pending)
        m_col = functools.reduce(jnp.maximum, [jnp.max(s, axis=0, keepdims=True) for s in s_parts])
        pending = [jnp.exp2(s - m_col).astype(BF16) for s in s_parts]
    finish(len(units) - 1, pending)
    for m in range(npairs):
        for r in range(row_split):
            both = jnp.concatenate([outs[(m, r, 0)], outs[(m, r, 1)]], axis=0)
            o_ref[r * rows:(r + 1) * rows, m * LANES:(m + 1) * LANES] = both.T.astype(BF16)


def _na_block_rows(rb, rows):
    r0 = rb * NA_QROWS
    return r0, int(np.clip(r0 - NA_KR // 2, 0, rows - NA_KROWS))


def _na_kernel(q_ref, k_ref, v_ref, kc_ref, vc_ref, tab_ref, rowmask_ref, o_ref):
    rows = q_ref.shape[0] // GRID_W
    nrb = rows // NA_QROWS
    nq = NA_QROWS * GRID_W
    nk = NA_KROWS * GRID_W
    units = [(rb, hh) for rb in range(nrb) for hh in range(2)]
    outs = {}

    def bias_t(rb, hh):
        r0, kstart = _na_block_rows(rb, rows)
        strips = []
        for jp in range(NA_QROWS // 2):
            first = kstart - (r0 + 2 * jp) + 2 * (NA_KR - 1)
            strips.append(tab_ref[hh, first * GRID_W:first * GRID_W + nk, :] + rowmask_ref[rb, jp])
        return jnp.concatenate(strips, axis=1)

    def keys_at(rb):
        k0 = _na_block_rows(rb, rows)[1] * GRID_W
        return slice(k0, k0 + nk)

    def scores(u):
        rb, hh = units[u]
        qb = _head_of_pair(q_ref[rb * nq:(rb + 1) * nq, :], hh)
        return [_dot_nt(k_ref[keys_at(rb), :], qb) + bias_t(rb, hh), _dot_nt(kc_ref[...], qb)]

    vt_cache = {}

    def with_ones(v):
        vt = v.astype(F32).T.astype(BF16)
        return jnp.concatenate([vt, jnp.ones((HALO, vt.shape[1]), BF16)], axis=0)

    def values_t(rb):
        if "ctx" not in vt_cache:
            vt_cache["ctx"] = with_ones(vc_ref[...])
        if rb not in vt_cache:
            vt_cache[rb] = with_ones(v_ref[keys_at(rb), :])
        return [vt_cache[rb], vt_cache["ctx"]]

    def finish(u, p_parts):
        rb, hh = units[u]
        ext = sum(_dot(vt, p) for vt, p in zip(values_t(rb), p_parts))
        outs[units[u]] = ext[hh * HD:(hh + 1) * HD] / ext[LANES:LANES + 1]

    s_ahead = scores(0)
    pending = None
    for u in range(len(units)):
        s_parts = s_ahead
        if u + 1 < len(units):
            s_ahead = scores(u + 1)
        if pending is not None:
            finish(u - 1, pending)
        m_col = functools.reduce(jnp.maximum, [jnp.max(s, axis=0, keepdims=True) for s in s_parts])
        pending = [jnp.exp2(s - m_col).astype(BF16) for s in s_parts]
    finish(len(units) - 1, pending)
    for rb in range(nrb):
        both = jnp.concatenate([outs[(rb, 0)], outs[(rb, 1)]], axis=0)
        o_ref[rb * nq:(rb + 1) * nq, :] = both.T.astype(BF16)


NA_TAB_BLOCKS = 30


def _na_bias_tables(rpb, rows):
    ndr = 2 * NA_KR - 1
    col = np.arange(GRID_W)
    cs = np.clip(col - NA_KC // 2, 0, GRID_W - NA_KC)
    col_ok = (col[None, :] >= cs[:, None]) & (col[None, :] < cs[:, None] + NA_KC)
    dc = np.clip(col[None, :] - col[:, None] + NA_KC - 1, 0, 2 * NA_KC - 2)
    pick_dc = (np.arange(2 * NA_KC - 1)[:, None, None] == dc[None]).astype(np.float32)
    by_col = jnp.einsum('hdj,jqc->hdcq', rpb.astype(F32) * LOG2E, pick_dc, precision=lax.Precision.HIGHEST)
    by_col = jnp.where(col_ok.T[None, None], by_col, -jnp.inf)
    flat = by_col.reshape(NA_HEADS // 2, 2, ndr * GRID_W, GRID_W)
    lead = NA_KR - 1
    tail = NA_TAB_BLOCKS - ndr - lead

    def shifted(shift):
        return jnp.pad(flat, ((0, 0), (0, 0), ((lead + shift) * GRID_W, (tail - shift) * GRID_W), (0, 0)))
    table = jnp.concatenate([shifted(0), shifted(1)], axis=-1)
    nrb = rows // NA_QROWS
    row_mask = np.full((nrb, NA_QROWS // 2, NA_KROWS, GRID_W, 2, GRID_W), -np.inf, np.float32)
    for rb in range(nrb):
        r0, kstart = _na_block_rows(rb, rows)
        for qr in range(NA_QROWS):
            rs = int(np.clip(r0 + qr - NA_KR // 2, 0, rows - NA_KR))
            for kk in range(NA_KROWS):
                if rs <= kstart + kk < rs + NA_KR:
                    row_mask[rb, qr // 2, kk, :, qr % 2, :] = 0.0
    return table, jnp.asarray(row_mask.reshape(nrb, NA_QROWS // 2, NA_KROWS * GRID_W, 2 * GRID_W))


def _na_attention(geom, proj, k_ctx, v_ctx, rpb):
    npairs = NA_HEADS // 2
    s, ds = geom.s, geom.ds
    oa_p = pl.pallas_call(
        functools.partial(_pair_attn_kernel, has_ctx=False, row_split=1, kv_share=1),
        out_shape=jax.ShapeDtypeStruct((geom.np_rows, NA_W), BF16),
        grid=(geom.b,),
        in_specs=[pl.BlockSpec((s, NA_W), lambda b: (b, 0)),
                  pl.BlockSpec((s, NA_W), lambda b: (b, 1)),
                  pl.BlockSpec((s, NA_W), lambda b: (b, 2))],
        out_specs=pl.BlockSpec((s, NA_W), lambda b: (b, 0)),
        compiler_params=_params(("parallel",)),
        name="na_prompt_attn",
    )(proj, proj, proj)

    rows = ds // GRID_W
    assert rows % NA_QROWS == 0 and rows >= NA_KROWS
    table, row_mask = _na_bias_tables(rpb, rows)
    s0 = geom.np_rows // ds
    kc = k_ctx.reshape(geom.db, npairs, 2, -1, HD).transpose(0, 1, 3, 2, 4).reshape(geom.db, npairs, -1, LANES).astype(BF16)
    vc = v_ctx.reshape(geom.db, npairs, 2, -1, HD).transpose(0, 1, 3, 2, 4).reshape(geom.db, npairs, -1, LANES).astype(BF16)
    nctx = kc.shape[2]
    tab_spec = pl.BlockSpec((None, 2, NA_TAB_BLOCKS * GRID_W, LANES), lambda p, b: (p, 0, 0, 0))
    oa_s = pl.pallas_call(
        _na_kernel,
        out_shape=jax.ShapeDtypeStruct((geom.ns_rows, NA_W), BF16),
        grid=(npairs, geom.db),
        in_specs=[pl.BlockSpec((ds, LANES), lambda p, b: (s0 + b, p)),
                  pl.BlockSpec((ds, LANES), lambda p, b: (s0 + b, npairs + p)),
                  pl.BlockSpec((ds, LANES), lambda p, b: (s0 + b, 2 * npairs + p)),
                  pl.BlockSpec((None, None, nctx, LANES), lambda p, b: (b, p, 0, 0)),
                  pl.BlockSpec((None, None, nctx, LANES), lambda p, b: (b, p, 0, 0)),
                  tab_spec, _const_spec(row_mask.shape)],
        out_specs=pl.BlockSpec((ds, LANES), lambda p, b: (b, p)),
        compiler_params=_params(("parallel", "parallel")),
        name="na_sample_attn",
    )(proj, proj, proj, kc, vc, table, row_mask)
    return oa_p, oa_s


GLA_LEVELS = 6


def _gla_constants():
    c = GLA_CHUNK
    t = np.arange(c)
    i, j = t[:, None], t[None, :]
    out = {}
    for fwd in (True, False):
        tri = (j <= i) if fwd else (j >= i)
        masks = np.zeros((GLA_LEVELS + 1, c, c), np.float32)
        for lvl in range(GLA_LEVELS):
            half = c >> (lvl + 1)
            upper = (t % (2 * half)) >= half
            same = (i // (2 * half)) == (j // (2 * half))
            masks[lvl] = (same & upper[:, None] & ~upper[None, :]) if fwd else (same & ~upper[:, None] & upper[None, :])
        masks[GLA_LEVELS] = np.eye(c)
        out[fwd] = (jnp.asarray(tri.astype(np.float32), BF16),
                    jnp.asarray(np.concatenate([masks, masks], axis=1).reshape(-1, c), F32))
    return out


def _row_of_block(x, block, row):
    c = x.shape[0]
    if block >= SUBLANES:
        return jnp.concatenate(
            [jnp.broadcast_to(x[b * block + row:b * block + row + 1, :], (block, x.shape[1])) for b in range(c // block)],
            axis=0)
    per = SUBLANES // block
    sub = lax.broadcasted_iota(jnp.int32, x.shape, 0) % SUBLANES
    out = _row_of_block(x, SUBLANES, row)
    for p in range(1, per):
        out = jnp.where(sub >= p * block, _row_of_block(x, SUBLANES, p * block + row), out)
    return out


def _gla_cumulative(g, tri):
    g_hi, g_lo = _split(g)
    return _dot(tri, g_hi) + _dot(tri, g_lo)


def _gla_intra(q, k, g, b, masks, fwd):
    c = GLA_CHUNK
    t = lax.broadcasted_iota(jnp.int32, b.shape, 0)
    att = jnp.zeros((2 * c, c), F32)
    for lvl in range(GLA_LEVELS + 1):
        half = c >> (lvl + 1)
        if half >= 2:
            ref = _row_of_block(b, 2 * half, half - 1 if fwd else half)
            past_mid = ((t % (2 * half)) >= half) == fwd
            w = jnp.exp(jnp.where(past_mid, b - ref, ref - b))
            ql, kl = q * w, k * w
        elif half == 1:
            w = jnp.exp(jnp.where((t % 2 == 1) == fwd, g, 0.0))
            ql, kl = q * w, k * w
        else:
            ql, kl = q, k
        p = _dot_nt(_stack_heads(ql).astype(BF16), kl.astype(BF16))
        att = att + p * masks[lvl * 2 * c:(lvl + 1) * 2 * c]
    return att.astype(BF16)


def _gla_outputs(q, k, v, b, att, st, fwd):
    c = GLA_CHUNK
    edge = b[c - 1:c] if fwd else b[0:1]
    q_dec = q * jnp.exp(b)
    o_st = _dot_nt(_stack_heads(q_dec).astype(BF16), st.astype(BF16))
    r = _dot(att, v)
    o = jnp.concatenate([o_st[0:c] + r[0:c, 0:GLA_DV], o_st[c:] + r[c:, GLA_DV:]], axis=1)
    k_dec = (k * jnp.exp(edge - b)).astype(BF16)
    u = _dot_tn(v, k_dec)
    st_new = jnp.exp(edge) * st + jnp.where(_lane_lo(), u[0:GLA_DV], u[GLA_DV:])
    return o, st_new


GLA_GROUP = 4


def _gla_kernel(q_ref, k_ref, v_ref, rg_ref, gf_ref, gb_ref, s0f_ref, s0b_ref, gain_ref,
                tri_f_ref, masks_f_ref, tri_b_ref, masks_b_ref, y_ref, sf_ref, sb_ref,
                of_scr, ob_scr, st_scr):
    t = q_ref.shape[0]
    c = GLA_CHUNK
    nc = t // c
    assert nc % GLA_GROUP == 0
    for d, s0_ref in enumerate((s0f_ref, s0b_ref)):
        st_scr[d] = jnp.concatenate([s0_ref[0], s0_ref[1]], axis=0).T
    per_dir = ((gf_ref, of_scr, tri_f_ref, masks_f_ref), (gb_ref, ob_scr, tri_b_ref, masks_b_ref))

    def body(gi, carry):
        streams = []
        for u in range(GLA_GROUP):
            for d in range(2):
                ci = gi * GLA_GROUP + u
                cc = ci if d == 0 else nc - 1 - ci
                streams.append((d, pl.ds(pl.multiple_of(cc * c, c), c)))
        qkg = [(q_ref[rows, :].astype(F32), k_ref[rows, :].astype(F32), per_dir[d][0][rows, :])
               for d, rows in streams]
        cums = [_gla_cumulative(g, per_dir[d][2][...]) for (d, _), (_, _, g) in zip(streams, qkg)]
        atts = [_gla_intra(q, k, g, b, per_dir[d][3][...], d == 0)
                for (d, _), (q, k, g), b in zip(streams, qkg, cums)]
        for (d, rows), (q, k, _), b, att in zip(streams, qkg, cums, atts):
            o, st_new = _gla_outputs(q, k, v_ref[rows, :], b, att, st_scr[d], d == 0)
            per_dir[d][1][rows, :] = o
            st_scr[d] = st_new
        return carry

    lax.fori_loop(0, nc // GLA_GROUP, body, 0)

    for d, s_ref in enumerate((sf_ref, sb_ref)):
        s_pair = st_scr[d].T
        s_ref[0] = s_pair[0:GLA_DK]
        s_ref[1] = s_pair[GLA_DK:]

    rt = 256
    for r in range(t // rt):
        og = of_scr[r * rt:(r + 1) * rt, :] + ob_scr[r * rt:(r + 1) * rt, :]
        halves = []
        for hh in range(2):
            x = og[:, hh * GLA_DV:(hh + 1) * GLA_DV]
            halves.append(x * lax.rsqrt(jnp.mean(x * x, axis=-1, keepdims=True) + EPS))
        y = jnp.concatenate(halves, axis=1) * gain_ref[...] * _silu(rg_ref[r * rt:(r + 1) * rt, :].astype(F32))
        y_ref[r * rt:(r + 1) * rt, :] = y.astype(BF16)


def _gla_call(proj, gates, s0f, s0b, gain, consts, t, nb, row0, name):
    npairs = GLA_HEADS // 2
    qcol = 3 * NA_W // LANES
    kcol = qcol + GLA_KW // LANES
    vcol = (3 * NA_W + 2 * GLA_KW) // (2 * GLA_DV)
    rcol = vcol + GLA_VW // (2 * GLA_DV)
    tri_f, masks_f = consts[True]
    tri_b, masks_b = consts[False]
    st_spec = pl.BlockSpec((None, 2, GLA_DK, GLA_DV), lambda b, p: (b, p, 0, 0))
    return pl.pallas_call(
        _gla_kernel,
        out_shape=(jax.ShapeDtypeStruct((nb * t, GLA_VW), BF16),
                   jax.ShapeDtypeStruct((nb, GLA_HEADS, GLA_DK, GLA_DV), F32),
                   jax.ShapeDtypeStruct((nb, GLA_HEADS, GLA_DK, GLA_DV), F32)),
        grid=(nb, npairs),
        in_specs=[pl.BlockSpec((t, LANES), lambda b, p: (row0 + b, qcol + p)),
                  pl.BlockSpec((t, LANES), lambda b, p: (row0 + b, kcol + p)),
                  pl.BlockSpec((t, 2 * GLA_DV), lambda b, p: (row0 + b, vcol + p)),
                  pl.BlockSpec((t, 2 * GLA_DV), lambda b, p: (row0 + b, rcol + p)),
                  pl.BlockSpec((t, LANES), lambda b, p: (row0 + b, p)),
                  pl.BlockSpec((t, LANES), lambda b, p: (row0 + b, npairs + p)),
                  st_spec, st_spec,
                  pl.BlockSpec((1, 2 * GLA_DV), lambda b, p: (0, p)),
                  _const_spec(tri_f.shape), _const_spec(masks_f.shape),
                  _const_spec(tri_b.shape), _const_spec(masks_b.shape)],
        out_specs=(pl.BlockSpec((t, 2 * GLA_DV), lambda b, p: (b, p)), st_spec, st_spec),
        scratch_shapes=[pltpu.VMEM((t, 2 * GLA_DV), F32), pltpu.VMEM((t, 2 * GLA_DV), F32),
                        pltpu.VMEM((2, GLA_DV, LANES), F32)],
        compiler_params=_params(("parallel", "parallel")),
        name=name,
    )(proj, proj, proj, proj, gates, gates, s0f, s0b, gain, tri_f, masks_f, tri_b, masks_b)


def _out_proj_kernel(*refs, n_act, split_npt, x_split):
    npt = split_npt
    nx = 2 if x_split else 1
    act_refs = refs[:2 * n_act]
    w_ref = refs[2 * n_act]
    x_refs = refs[2 * n_act + 1:2 * n_act + 1 + nx]
    gate_ref, o_ref = refs[2 * n_act + 1 + nx:]
    acc = None
    k0 = 0
    for a in range(n_act):
        act = _read_split(act_refs[2 * a], act_refs[2 * a + 1], npt)
        kw = act.shape[1]
        part = _dot(act, w_ref[k0:k0 + kw, :])
        acc = part if acc is None else acc + part
        k0 += kw
    x = _read_split(x_refs[0], x_refs[1], npt) if x_split else x_refs[0][...]
    o_ref[...] = x + gate_ref[...] * acc


def _out_proj(geom, act_pairs, w, xs, mod_l, gate_k):
    x_split = len(xs) == 2
    in_specs, args = [], []
    for a_p, a_s in act_pairs:
        in_specs += geom.split_specs(TP, a_p.shape[1])
        args += [a_p, a_s]
    in_specs.append(_const_spec(w.shape))
    in_specs += geom.split_specs(TP, D) if x_split else [pl.BlockSpec((TP, D), lambda i: (i, 0))]
    in_specs.append(geom.mod_spec(gate_k, TP))
    return pl.pallas_call(
        functools.partial(_out_proj_kernel, n_act=len(act_pairs), split_npt=geom.np_rows // TP, x_split=x_split),
        out_shape=jax.ShapeDtypeStruct((geom.n, D), F32),
        grid=(geom.n // TP,),
        in_specs=in_specs,
        out_specs=pl.BlockSpec((TP, D), lambda i: (i, 0)),
        compiler_params=_params(("parallel",)),
        name="out_proj",
    )(*args, w, *xs, mod_l)


C_OUT = C_QW + 4 * C_KW


def _c_in_kernel(x_ref, g_ref, sc_ref, sh_ref, w_ref, gain_ref, cos_ref, sin_ref, grp_ref, o_ref):
    h = _norm_mod(x_ref[...], g_ref[...], sc_ref[...], sh_ref[...]).astype(BF16)
    cw = 512
    nch = C_OUT // cw
    n_norm = (C_QW + 2 * C_KW) // LANES
    quarter = HD // 4
    first_half = lax.broadcasted_iota(jnp.int32, (1, LANES), 1) % (2 * quarter) < quarter
    y_ahead = _dot(h, w_ref[:, 0:cw])
    for j in range(nch):
        y = y_ahead
        if j + 1 < nch:
            y_ahead = _dot(h, w_ref[:, (j + 1) * cw:(j + 2) * cw])
        for s in range(cw // LANES):
            blk = j * (cw // LANES) + s
            yb = y[:, s * LANES:(s + 1) * LANES]
            if blk < n_norm:
                ms = _dot((yb * yb).astype(BF16), grp_ref[...])
                yn = yb * lax.rsqrt(ms + EPS) * gain_ref[:, blk * LANES:(blk + 1) * LANES]
                partner = jnp.where(first_half, pltpu.roll(yn, LANES - quarter, 1), pltpu.roll(yn, quarter, 1))
                yb = yn * cos_ref[...] + partner * sin_ref[...]
            o_ref[:, blk * LANES:(blk + 1) * LANES] = yb.astype(BF16)


def _rope_tables(geom):
    tpos = np.arange(geom.ds)
    inv = ROPE_THETA ** (-np.arange(0, HD // 2, 2, dtype=np.float32) / (HD // 2))
    ang_r = (tpos // GRID_W).astype(np.float32)[:, None] * inv
    ang_c = (tpos % GRID_W).astype(np.float32)[:, None] * inv
    ang_r, ang_c = jnp.asarray(ang_r, F32), jnp.asarray(ang_c, F32)
    cos = jnp.concatenate([jnp.cos(ang_r)] * 2 + [jnp.cos(ang_c)] * 2, axis=1)
    sin = jnp.concatenate([-jnp.sin(ang_r), jnp.sin(ang_r), -jnp.sin(ang_c), jnp.sin(ang_c)], axis=1)
    cos = jnp.concatenate([jnp.tile(cos, (1, 2)), jnp.ones((TP, LANES), F32)], axis=0)
    sin = jnp.concatenate([jnp.tile(sin, (1, 2)), jnp.zeros((TP, LANES), F32)], axis=0)
    lane = np.arange(LANES)
    grp = (lane[:, None] // HD == lane[None, :] // HD).astype(np.float32) / HD
    return cos, sin, jnp.asarray(grp, BF16)


def _c_in_proj(geom, x, mod_l, g, w, gain, tables):
    cos, sin, grp = tables
    npt, tps = geom.np_rows // TP, geom.ds // TP

    def pos_map(i):
        return (jnp.where(i < npt, tps, (i - npt) % tps), 0)
    return pl.pallas_call(
        _c_in_kernel,
        out_shape=jax.ShapeDtypeStruct((geom.n, C_OUT), BF16),
        grid=(geom.n // TP,),
        in_specs=[pl.BlockSpec((TP, D), lambda i: (i, 0)),
                  _const_spec((1, D)),
                  geom.mod_spec(1, TP), geom.mod_spec(0, TP),
                  _const_spec(w.shape), _const_spec(gain.shape),
                  pl.BlockSpec((TP, LANES), pos_map), pl.BlockSpec((TP, LANES), pos_map),
                  _const_spec(grp.shape)],
        out_specs=pl.BlockSpec((TP, C_OUT), lambda i: (i, 0)),
        compiler_params=_params(("parallel",)),
        name="gqa_in_proj",
    )(x, g, mod_l, mod_l, w, gain, cos, sin, grp)


def _gqa_attention(geom, proj, k_ctx, v_ctx):
    nqp = GQA_HEADS // 2
    kcol = C_QW // LANES
    vcol = kcol + 2 * C_KW // LANES
    s, ds = geom.s, geom.ds
    kvw = 2 * C_KW
    o_p = pl.pallas_call(
        functools.partial(_pair_attn_kernel, has_ctx=False, row_split=1, kv_share=2),
        out_shape=jax.ShapeDtypeStruct((geom.np_rows, C_QW), BF16),
        grid=(geom.b,),
        in_specs=[pl.BlockSpec((s, C_QW), lambda b: (b, 0)),
                  pl.BlockSpec((s, kvw), lambda b: (b, C_QW // kvw)),
                  pl.BlockSpec((s, kvw), lambda b: (b, C_QW // kvw + 1))],
        out_specs=pl.BlockSpec((s, C_QW), lambda b: (b, 0)),
        compiler_params=_params(("parallel",)),
        name="gqa_prompt_attn",
    )(proj, proj, proj)

    tq = ds
    nqt = ds // tq
    q0 = geom.np_rows // tq
    s0 = geom.np_rows // ds
    kc = jnp.concatenate([k_ctx, k_ctx], axis=-1).astype(BF16)
    vc = jnp.concatenate([v_ctx, v_ctx], axis=-1).astype(BF16)
    nctx = kc.shape[2]
    o_s = pl.pallas_call(
        functools.partial(_shared_kv_attn_kernel, has_ctx=True, row_split=tq // 256, kv_share=1),
        out_shape=jax.ShapeDtypeStruct((geom.ns_rows, C_QW), BF16),
        grid=(geom.db, nqp, nqt),
        in_specs=[pl.BlockSpec((tq, LANES), lambda b, m, t: (q0 + nqt * b + t, m)),
                  pl.BlockSpec((ds, LANES), lambda b, m, t: (s0 + b, kcol + m // 2)),
                  pl.BlockSpec((ds, LANES), lambda b, m, t: (s0 + b, vcol + m // 2)),
                  pl.BlockSpec((None, None, nctx, LANES), lambda b, m, t: (b, m // 2, 0, 0)),
                  pl.BlockSpec((None, None, nctx, LANES), lambda b, m, t: (b, m // 2, 0, 0))],
        out_specs=pl.BlockSpec((tq, LANES), lambda b, m, t: (nqt * b + t, m)),
        compiler_params=_params(("parallel", "parallel", "parallel")),
        name="gqa_sample_attn",
    )(proj, proj, proj, kc, vc)
    return o_p, o_s


def _ffn_kernel(x_ref, xp_ref, xn_ref, g_ref, sc_ref, sh_ref, gate_ref, wup_ref, cw_ref, cb_ref, wdn_ref,
                o_ref, acc_ref, h_ref, *, npt, tps, seq):
    i = pl.program_id(0)
    j = jnp.maximum(i - npt, 0) % tps
    is_prompt = i < npt
    has_prev = jnp.logical_and(jnp.logical_not(is_prompt), j > 0).astype(F32)
    has_next = jnp.logical_and(jnp.logical_not(is_prompt), j < tps - 1).astype(F32)
    g, sc, sh = g_ref[...], sc_ref[...], sh_ref[...]
    x = x_ref[...]
    h_ref[0:HALO, :] = (_norm_mod(xp_ref[...], g, sc, sh) * has_prev).astype(BF16)
    h_ref[HALO:HALO + TF, :] = _norm_mod(x, g, sc, sh).astype(BF16)
    h_ref[HALO + TF:, :] = (_norm_mod(xn_ref[...], g, sc, sh) * has_next).astype(BF16)
    rows = TF + 2 * HALO
    nch = D_FF // FF_CHUNK
    inner = list(range(seq, TF, seq))
    sub = lax.broadcasted_iota(jnp.int32, (SUBLANES, FF_CHUNK), 0)
    no_prev = jnp.where(jnp.logical_and(is_prompt, sub == 0), 0.0, 1.0)
    no_next = jnp.where(jnp.logical_and(is_prompt, sub == SUBLANES - 1), 0.0, 1.0)

    def up(c):
        return [_dot(h_ref[...], wup_ref[:, off + c * FF_CHUNK:off + (c + 1) * FF_CHUNK]) for off in (0, D_FF)]

    def conv(u, off, c):
        cols = slice(off + c * FF_CHUNK, off + (c + 1) * FF_CHUNK)
        u_prev = pltpu.roll(u, 1, 0)[HALO:HALO + TF]
        u_next = pltpu.roll(u, rows - 1, 0)[HALO:HALO + TF]
        for b in inner:
            u_prev = jnp.concatenate([u_prev[:b], u_prev[b:b + SUBLANES] * no_prev, u_prev[b + SUBLANES:]], axis=0)
            u_next = jnp.concatenate([u_next[:b - SUBLANES], u_next[b - SUBLANES:b] * no_next, u_next[b:]], axis=0)
        return (cw_ref[0:1, cols] * u_prev + cw_ref[1:2, cols] * u[HALO:HALO + TF]
                + cw_ref[2:3, cols] * u_next + cb_ref[:, cols])

    def down(c, a):
        part = _dot(a, wdn_ref[c * FF_CHUNK:(c + 1) * FF_CHUNK, :])
        if c == 0:
            acc_ref[...] = part
        else:
            acc_ref[...] += part

    u_ahead = up(0)
    a_prev = None
    for c in range(nch):
        u_val, u_gate = u_ahead
        if c + 1 < nch:
            u_ahead = up(c + 1)
        if a_prev is not None:
            down(c - 1, a_prev)
        a_prev = (_silu(conv(u_gate, D_FF, c)) * conv(u_val, 0, c)).astype(BF16)
    down(nch - 1, a_prev)
    o_ref[...] = x + gate_ref[...] * acc_ref[...]


def _conv_ffn(geom, x, mod_l, g, wup, cw, cb, wdn):
    assert TF % geom.s == 0 and geom.np_rows % TF == 0 and geom.ds % TF == 0
    nblk = geom.n // HALO
    per = TF // HALO
    return pl.pallas_call(
        functools.partial(_ffn_kernel, npt=geom.np_rows // TF, tps=geom.ds // TF, seq=geom.s),
        out_shape=jax.ShapeDtypeStruct((geom.n, D), F32),
        grid=(geom.n // TF,),
        in_specs=[pl.BlockSpec((TF, D), lambda i: (i, 0)),
                  pl.BlockSpec((HALO, D), lambda i: (jnp.maximum(i * per - 1, 0), 0)),
                  pl.BlockSpec((HALO, D), lambda i: (jnp.minimum((i + 1) * per, nblk - 1), 0)),
                  _const_spec((1, D)),
                  geom.mod_spec(4, TF), geom.mod_spec(3, TF), geom.mod_spec(5, TF),
                  _const_spec(wup.shape), _const_spec(cw.shape), _const_spec(cb.shape), _const_spec(wdn.shape)],
        out_specs=pl.BlockSpec((TF, D), lambda i: (i, 0)),
        scratch_shapes=[pltpu.VMEM((TF, D), F32), pltpu.VMEM((TF + 2 * HALO, D), BF16)],
        compiler_params=_params(("parallel",)),
        name="conv_ffn",
    )(x, x, x, g, mod_l, mod_l, mod_l, wup, cw, cb, wdn)


def _final_norm_kernel(x_ref, g_ref, o_ref):
    x = x_ref[...]
    o_ref[...] = x * lax.rsqrt(jnp.mean(x * x, axis=-1, keepdims=True) + EPS) * g_ref[...]


def _final_norm(x, g, row0, nrows):
    t0 = row0 // TM
    return pl.pallas_call(
        _final_norm_kernel,
        out_shape=jax.ShapeDtypeStruct((nrows, D), F32),
        grid=(nrows // TM,),
        in_specs=[pl.BlockSpec((TM, D), lambda i: (t0 + i, 0)), _const_spec((1, D))],
        out_specs=pl.BlockSpec((TM, D), lambda i: (i, 0)),
        compiler_params=_params(("parallel",)),
        name="final_norm",
    )(x, g)


def _prep_ab(w_in, w2_f, b2_f, w2_b, b2_b):
    scale = np.ones((w_in.shape[1],), np.float32)
    scale[0:NA_W] = HD ** -0.5 * LOG2E
    scale[3 * NA_W:3 * NA_W + GLA_KW] = GLA_DK ** -0.5
    w = jnp.pad(w_in * scale, ((0, 0), (0, AB_MAIN + LANES - w_in.shape[1]))).astype(BF16)
    w2 = jnp.zeros((LANES, 2 * GLA_KW), F32)
    w2 = w2.at[0:GLA_RANK, 0:GLA_KW].set(w2_f).at[GLA_RANK:2 * GLA_RANK, GLA_KW:].set(w2_b)
    w2_hi, w2_lo = _split(w2)
    b2 = jnp.concatenate([b2_f, b2_b])[None, :]
    return w, w2_hi, w2_lo, b2


def _prep_c(w_in, qn_g, kn_g):
    wq, wk, wv = w_in[:, :C_QW], w_in[:, C_QW:C_QW + C_KW], w_in[:, C_QW + C_KW:]

    def dup(w):
        return jnp.broadcast_to(w.reshape(D, GQA_KVH, 1, HD), (D, GQA_KVH, 2, HD)).reshape(D, 2 * C_KW)
    w = jnp.concatenate([wq, dup(wk), dup(wv)], axis=1).astype(BF16)
    gain = jnp.concatenate([jnp.tile(qn_g * (HD ** -0.5 * LOG2E), GQA_HEADS), jnp.tile(kn_g, 2 * GQA_KVH)])[None, :]
    return w, gain


def kernel(x_prompt, x_sample, cache_na_k, cache_na_v, state_gla_fwd, state_gla_bwd, cache_gqa_k, cache_gqa_v, c, c_ctx, ada_w, ada_b, norm_mix_g, norm_ffn_g, ab_w_in, ab_w_out, na_rpb, gla_w2_fwd, gla_b2_fwd, gla_w2_bwd, gla_b2_bwd, gla_norm_g, gqa_w_in, gqa_w_out, gqa_q_norm_g, gqa_k_norm_g, ffn_w_up, ffn_conv_w, ffn_conv_b, ffn_w_down, final_norm_g):
    b, s, _ = x_prompt.shape
    db, ds, _ = x_sample.shape
    geom = _Geom(b, s, db, ds)
    assert ds % GRID_W == 0 and geom.np_rows % ds == 0 and geom.np_rows % TP == 0

    xs = [x_prompt.reshape(-1, D), x_sample.reshape(-1, D)]
    nrow = -(-(db + 1) // SUBLANES) * SUBLANES
    c_rows = jnp.zeros((nrow, D), F32).at[:db].set(c).at[db].set(c_ctx)
    mod = _modulation(c_rows, ada_w, ada_b).reshape(DEPTH, nrow, 1, 6 * D)

    gla_consts = _gla_constants()
    rope = _rope_tables(geom)
    zeros_state = jnp.zeros((b, GLA_HEADS, GLA_DK, GLA_DV), F32)
    na_k, na_v, gla_f, gla_b, gq_k, gq_v = [], [], [], [], [], []

    for i in range(DEPTH):
        j = i // 2
        mod_l = mod[i]
        if i % 2 == 0:
            w, w2_hi, w2_lo, b2 = _prep_ab(ab_w_in[j], gla_w2_fwd[j], gla_b2_fwd[j], gla_w2_bwd[j], gla_b2_bwd[j])
            proj, gates = _ab_in_proj(geom, xs, mod_l, norm_mix_g[i][None, :], w, w2_hi, w2_lo, b2)
            oa = _na_attention(geom, proj, cache_na_k[:, j], cache_na_v[:, j], na_rpb[j])
            gain = gla_norm_g[j][None, :]
            yg_p, sf, sb = _gla_call(proj, gates, zeros_state, zeros_state, gain, gla_consts, s, b, 0, "gla_prompt")
            yg_s, _, _ = _gla_call(proj, gates, state_gla_fwd[:, j], state_gla_bwd[:, j], gain, gla_consts,
                                   ds, db, geom.np_rows // ds, "gla_sample")
            x = _out_proj(geom, [oa, (yg_p, yg_s)], ab_w_out[j].astype(BF16), xs, mod_l, 2)
            pk = proj[:geom.np_rows].reshape(b, s, -1)
            na_k.append(pk[..., NA_W:2 * NA_W].reshape(b, s, NA_HEADS, HD).transpose(0, 2, 1, 3).astype(F32))
            na_v.append(pk[..., 2 * NA_W:3 * NA_W].reshape(b, s, NA_HEADS, HD).transpose(0, 2, 1, 3).astype(F32))
            gla_f.append(sf)
            gla_b.append(sb)
        else:
            w, gain = _prep_c(gqa_w_in[j], gqa_q_norm_g[j], gqa_k_norm_g[j])
            proj = _c_in_proj(geom, xs[0], mod_l, norm_mix_g[i][None, :], w, gain, rope)
            o = _gqa_attention(geom, proj, cache_gqa_k[:, j], cache_gqa_v[:, j])
            x = _out_proj(geom, [o], gqa_w_out[j].astype(BF16), xs, mod_l, 2)
            pk = proj[:geom.np_rows].reshape(b, s, -1)
            kd = pk[..., C_QW:C_QW + 2 * C_KW].reshape(b, s, GQA_KVH, 2, HD)[:, :, :, 0]
            vd = pk[..., C_QW + 2 * C_KW:].reshape(b, s, GQA_KVH, 2, HD)[:, :, :, 0]
            gq_k.append(kd.transpose(0, 2, 1, 3).astype(F32))
            gq_v.append(vd.transpose(0, 2, 1, 3).astype(F32))
        x = _conv_ffn(geom, x, mod_l, norm_ffn_g[i][None, :], ffn_w_up[i].astype(BF16), ffn_conv_w[i],
                      ffn_conv_b[i][None, :], ffn_w_down[i].astype(BF16))
        xs = [x]

    y_prompt = _final_norm(x, final_norm_g[None, :], 0, geom.np_rows).reshape(b, s, D)
    y_sample = _final_norm(x, final_norm_g[None, :], geom.np_rows, geom.ns_rows).reshape(db, ds, D)
    return (y_prompt, y_sample,
            jnp.stack(na_k, axis=1), jnp.stack(na_v, axis=1),
            jnp.stack(gla_f, axis=1), jnp.stack(gla_b, axis=1),
            jnp.stack(gq_k, axis=1), jnp.stack(gq_v, axis=1))
```

```python
import functools

import numpy as np
import jax
import jax.numpy as jnp
from jax import lax
from jax.experimental import pallas as pl
from jax.experimental.pallas import tpu as pltpu

F32 = jnp.float32
BF16 = jnp.bfloat16

D = 1024
DEPTH = 4
HD = 64
GRID_W = 64
EPS = 1e-6
NA_HEADS = 8
NA_W = NA_HEADS * HD
NA_KR = 8
NA_KC = 16
GLA_HEADS = 4
GLA_DK = 64
GLA_DV = 128
GLA_KW = GLA_HEADS * GLA_DK
GLA_VW = GLA_HEADS * GLA_DV
GLA_RANK = 16
GLA_CHUNK = 64
GQA_HEADS = 16
GQA_KVH = 4
C_QW = GQA_HEADS * HD
C_KW = GQA_KVH * HD
ROPE_THETA = 10000.0
D_FF = 2816
AB_MAIN = 3 * NA_W + 2 * GLA_KW + 2 * GLA_VW

LANES = 128
SUBLANES = 8
TM = 256
TP = 512
TF = 256
LOG2E = 1.4426950408889634
HALO = 16
FF_CHUNK = 256
NA_QROWS = 8
NA_KROWS = NA_QROWS + NA_KR - 1
VMEM_LIMIT = 56 * 1024 * 1024


def _dot(a, b):
    return jnp.dot(a, b, preferred_element_type=F32)


def _dot_nt(a, b):
    return lax.dot_general(a, b, (((1,), (1,)), ((), ())), preferred_element_type=F32)


def _dot_tn(a, b):
    return lax.dot_general(a, b, (((0,), (0,)), ((), ())), preferred_element_type=F32)


def _split(a):
    hi = a.astype(BF16)
    lo = (a - hi.astype(F32)).astype(BF16)
    return hi, lo


def _dot_split(a, w_hi, w_lo):
    a_hi, a_lo = _split(a)
    return _dot(a_hi, w_hi) + _dot(a_lo, w_hi) + _dot(a_hi, w_lo)


def _norm_mod(x, g, sc, sh):
    xn = x * lax.rsqrt(jnp.mean(x * x, axis=-1, keepdims=True) + EPS)
    return (xn * g) * (1.0 + sc) + sh


def _silu(x):
    return x / (1.0 + jnp.exp(-x))


def _lane_lo(shape=(1, LANES)):
    return lax.broadcasted_iota(jnp.int32, shape, len(shape) - 1) < HD


def _stack_heads(x):
    lo = _lane_lo()
    zero = jnp.zeros_like(x)
    return jnp.concatenate([jnp.where(lo, x, zero), jnp.where(lo, zero, x)], axis=0)


def _params(sem):
    return pltpu.CompilerParams(dimension_semantics=sem, vmem_limit_bytes=VMEM_LIMIT)


def _mod_kernel(c_ref, w_ref, b_ref, o_ref):
    a = _silu(c_ref[...])
    w = w_ref[...]
    w_hi, w_lo = _split(w)
    o_ref[...] = _dot_split(a, w_hi, w_lo) + b_ref[...]


def _modulation(c_rows, ada_w, ada_b):
    nrow = c_rows.shape[0]
    tn = 1024
    return pl.pallas_call(
        _mod_kernel,
        out_shape=jax.ShapeDtypeStruct((DEPTH, nrow, 6 * D), F32),
        grid=(DEPTH, 6 * D // tn),
        in_specs=[pl.BlockSpec((nrow, D), lambda l, n: (0, 0)),
                  pl.BlockSpec((None, D, tn), lambda l, n: (l, 0, n)),
                  pl.BlockSpec((None, 1, tn), lambda l, n: (l, 0, n))],
        out_specs=pl.BlockSpec((None, nrow, tn), lambda l, n: (l, 0, n)),
        compiler_params=_params(("parallel", "parallel")),
        name="modulation",
    )(c_rows, ada_w, ada_b.reshape(DEPTH, 1, 6 * D))


class _Geom:
    def __init__(self, b, s, db, ds):
        assert s == TM and ds % TM == 0
        self.b, self.s, self.db, self.ds = b, s, db, ds
        self.np_rows = b * s
        self.ns_rows = db * ds
        self.n = self.np_rows + self.ns_rows
        self.npt = self.np_rows // TM
        self.tps = ds // TM
        self.ctx_row = db

    def mod_spec(self, k, tm=TM):
        assert self.np_rows % tm == 0 and self.ds % tm == 0
        npt, tps, ctx = self.np_rows // tm, self.ds // tm, self.ctx_row

        def imap(i):
            return (jnp.where(i < npt, ctx, (i - npt) // tps), 0, k)
        return pl.BlockSpec((None, 1, D), imap)

    def split_specs(self, tm, width):
        npt = self.np_rows // tm
        return [pl.BlockSpec((tm, width), lambda i: (jnp.minimum(i, npt - 1), 0)),
                pl.BlockSpec((tm, width), lambda i: (jnp.maximum(i - npt, 0), 0))]


def _read_split(p_ref, s_ref, npt):
    return jnp.where(pl.program_id(0) < npt, p_ref[...], s_ref[...])


def _const_spec(shape):
    nd = len(shape)
    return pl.BlockSpec(shape, lambda *_: (0,) * nd)


def _ab_in_kernel(*refs, split_npt):
    nx = 2 if split_npt else 1
    g_ref, sc_ref, sh_ref, w_ref, w2hi_ref, w2lo_ref, b2_ref, o_ref, gate_ref = refs[nx:]
    x = _read_split(refs[0], refs[1], split_npt) if split_npt else refs[0][...]
    h = _norm_mod(x, g_ref[...], sc_ref[...], sh_ref[...]).astype(BF16)
    cw = 512
    lr = _dot(h, w_ref[:, AB_MAIN:AB_MAIN + LANES])
    o_ref[:, 0:cw] = _dot(h, w_ref[:, 0:cw]).astype(BF16)
    z = _dot_split(lr, w2hi_ref[...], w2lo_ref[...]) + b2_ref[...]
    for j in range(1, AB_MAIN // cw):
        o_ref[:, j * cw:(j + 1) * cw] = _dot(h, w_ref[:, j * cw:(j + 1) * cw]).astype(BF16)
    gate_ref[...] = (jnp.minimum(z, 0.0) - jnp.log(1.0 + jnp.exp(-jnp.abs(z)))) * (1.0 / 16.0)


def _ab_in_proj(geom, xs, mod_l, g, w, w2hi, w2lo, b2):
    split = len(xs) == 2
    x_specs = geom.split_specs(TP, D) if split else [pl.BlockSpec((TP, D), lambda i: (i, 0))]
    return pl.pallas_call(
        functools.partial(_ab_in_kernel, split_npt=geom.np_rows // TP if split else 0),
        out_shape=(jax.ShapeDtypeStruct((geom.n, AB_MAIN), BF16),
                   jax.ShapeDtypeStruct((geom.n, 2 * GLA_KW), F32)),
        grid=(geom.n // TP,),
        in_specs=x_specs + [_const_spec((1, D)),
                            geom.mod_spec(1, TP), geom.mod_spec(0, TP),
                            _const_spec(w.shape), _const_spec(w2hi.shape), _const_spec(w2lo.shape),
                            _const_spec(b2.shape)],
        out_specs=(pl.BlockSpec((TP, AB_MAIN), lambda i: (i, 0)),
                   pl.BlockSpec((TP, 2 * GLA_KW), lambda i: (i, 0))),
        compiler_params=_params(("parallel",)),
        name="ab_in_proj",
    )(*xs, g, mod_l, mod_l, w, w2hi, w2lo, b2)


def _softmax_units(n_units, scores, values, emit):
    def finish(u, p_parts, l):
        emit(u, sum(_dot(p, v) for p, v in zip(p_parts, values(u))) / l)

    s_ahead = scores(0)
    pending = None
    for u in range(n_units):
        s_parts = s_ahead
        if u + 1 < n_units:
            s_ahead = scores(u + 1)
        if pending is not None:
            finish(u - 1, *pending)
        m = functools.reduce(jnp.maximum, [jnp.max(s, axis=-1, keepdims=True) for s in s_parts])
        p_parts = [jnp.exp2(s - m) for s in s_parts]
        l = sum(jnp.sum(p, axis=-1, keepdims=True) for p in p_parts)
        pending = ([p.astype(BF16) for p in p_parts], l)
    finish(n_units - 1, *pending)


def _head_of_pair(x, hh):
    lo = _lane_lo()
    zero = jnp.zeros_like(x)
    return jnp.where(lo, x, zero) if hh == 0 else jnp.where(lo, zero, x)


def _pair_attn_kernel(*refs, has_ctx, row_split, kv_share):
    if has_ctx:
        q_ref, k_ref, v_ref, kc_ref, vc_ref, o_ref = refs
    else:
        q_ref, k_ref, v_ref, o_ref = refs
    npairs = q_ref.shape[1] // LANES
    rows = q_ref.shape[0] // row_split
    units = [(m, r, hh) for m in range(npairs) for r in range(row_split) for hh in range(2)]
    outs = {}

    def kv_cols(m):
        c0 = (m // kv_share) * LANES
        return slice(c0, c0 + LANES)

    def scores(u):
        m, r, hh = units[u]
        qb = _head_of_pair(q_ref[r * rows:(r + 1) * rows, m * LANES:(m + 1) * LANES], hh)
        parts = [_dot_nt(qb, k_ref[:, kv_cols(m)])]
        if has_ctx:
            parts.append(_dot_nt(qb, kc_ref[...]))
        return parts

    def values(u):
        m = units[u][0]
        return [v_ref[:, kv_cols(m)]] + ([vc_ref[...]] if has_ctx else [])

    def emit(u, out):
        outs[units[u]] = out

    _softmax_units(len(units), scores, values, emit)
    lo = _lane_lo()
    for m in range(npairs):
        for r in range(row_split):
            o_ref[r * rows:(r + 1) * rows, m * LANES:(m + 1) * LANES] = jnp.where(
                lo, outs[(m, r, 0)], outs[(m, r, 1)]).astype(BF16)


def _shared_kv_attn_kernel(*refs, has_ctx, row_split, kv_share):
    if has_ctx:
        q_ref, k_ref, v_ref, kc_ref, vc_ref, o_ref = refs
    else:
        q_ref, k_ref, v_ref, o_ref = refs
    npairs = q_ref.shape[1] // LANES
    rows = q_ref.shape[0] // row_split
    units = [(m, r, hh) for m in range(npairs) for r in range(row_split) for hh in range(2)]
    outs = {}

    def kv_cols(m):
        c0 = (m // kv_share) * LANES
        return slice(c0, c0 + LANES)

    def values_t(ref, cols):
        vt = ref[:, cols].astype(F32).T
        return jnp.where(lax.broadcasted_iota(jnp.int32, vt.shape, 0) < HD, vt, 1.0).astype(BF16)

    vt_cache = {}

    def values(u):
        m = units[u][0]
        key = m // kv_share
        if key not in vt_cache:
            vt_cache[key] = [values_t(v_ref, kv_cols(m))] + ([values_t(vc_ref, slice(None))] if has_ctx else [])
        return vt_cache[key]

    def scores(u):
        m, r, hh = units[u]
        qb = _head_of_pair(q_ref[r * rows:(r + 1) * rows, m * LANES:(m + 1) * LANES], hh)
        parts = [_dot_nt(k_ref[:, kv_cols(m)], qb)]
        if has_ctx:
            parts.append(_dot_nt(kc_ref[...], qb))
        return parts

    def finish(u, p_parts):
        ext = sum(_dot(vt, p) for vt, p in zip(values(u), p_parts))
        outs[units[u]] = ext[0:HD] / ext[HD:HD + 1]

    s_ahead = scores(0)
    pending = None
    for u in range(len(units)):
        s_parts = s_ahead
        if u + 1 < len(units):
            s_ahead = scores(u + 1)
        if pending is not None:
            finish(u - 1, pending)
        m_col = functools.reduce(jnp.maximum, [jnp.max(s, axis=0, keepdims=True) for s in s_parts])
        pending = [jnp.exp2(s - m_col).astype(BF16) for s in s_parts]
    finish(len(units) - 1, pending)
    for m in range(npairs):
        for r in range(row_split):
            both = jnp.concatenate([outs[(m, r, 0)], outs[(m, r, 1)]], axis=0)
            o_ref[r * rows:(r + 1) * rows, m * LANES:(m + 1) * LANES] = both.T.astype(BF16)


def _na_block_rows(rb, rows):
    r0 = rb * NA_QROWS
    return r0, int(np.clip(r0 - NA_KR // 2, 0, rows - NA_KROWS))


def _na_kernel(q_ref, k_ref, v_ref, kc_ref, vc_ref, tab_ref, rowmask_ref, o_ref):
    rows = q_ref.shape[0] // GRID_W
    nrb = rows // NA_QROWS
    nq = NA_QROWS * GRID_W
    nk = NA_KROWS * GRID_W
    units = [(rb, hh) for rb in range(nrb) for hh in range(2)]
    outs = {}

    def bias_t(rb, hh):
        r0, kstart = _na_block_rows(rb, rows)
        strips = []
        for jp in range(NA_QROWS // 2):
            first = kstart - (r0 + 2 * jp) + 2 * (NA_KR - 1)
            strips.append(tab_ref[hh, first * GRID_W:first * GRID_W + nk, :] + rowmask_ref[rb, jp])
        return jnp.concatenate(strips, axis=1)

    def keys_at(rb):
        k0 = _na_block_rows(rb, rows)[1] * GRID_W
        return slice(k0, k0 + nk)

    def scores(u):
        rb, hh = units[u]
        qb = _head_of_pair(q_ref[rb * nq:(rb + 1) * nq, :], hh)
        return [_dot_nt(k_ref[keys_at(rb), :], qb) + bias_t(rb, hh), _dot_nt(kc_ref[...], qb)]

    vt_cache = {}

    def with_ones(v):
        vt = v.astype(F32).T.astype(BF16)
        return jnp.concatenate([vt, jnp.ones((HALO, vt.shape[1]), BF16)], axis=0)

    def values_t(rb):
        if "ctx" not in vt_cache:
            vt_cache["ctx"] = with_ones(vc_ref[...])
        if rb not in vt_cache:
            vt_cache[rb] = with_ones(v_ref[keys_at(rb), :])
        return [vt_cache[rb], vt_cache["ctx"]]

    def finish(u, p_parts):
        rb, hh = units[u]
        ext = sum(_dot(vt, p) for vt, p in zip(values_t(rb), p_parts))
        outs[units[u]] = ext[hh * HD:(hh + 1) * HD] / ext[LANES:LANES + 1]

    s_ahead = scores(0)
    pending = None
    for u in range(len(units)):
        s_parts = s_ahead
        if u + 1 < len(units):
            s_ahead = scores(u + 1)
        if pending is not None:
            finish(u - 1, pending)
        m_col = functools.reduce(jnp.maximum, [jnp.max(s, axis=0, keepdims=True) for s in s_parts])
        pending = [jnp.exp2(s - m_col).astype(BF16) for s in s_parts]
    finish(len(units) - 1, pending)
    for rb in range(nrb):
        both = jnp.concatenate([outs[(rb, 0)], outs[(rb, 1)]], axis=0)
        o_ref[rb * nq:(rb + 1) * nq, :] = both.T.astype(BF16)


NA_TAB_BLOCKS = 30


def _na_bias_tables(rpb, rows):
    ndr = 2 * NA_KR - 1
    col = np.arange(GRID_W)
    cs = np.clip(col - NA_KC // 2, 0, GRID_W - NA_KC)
    col_ok = (col[None, :] >= cs[:, None]) & (col[None, :] < cs[:, None] + NA_KC)
    dc = np.clip(col[None, :] - col[:, None] + NA_KC - 1, 0, 2 * NA_KC - 2)
    pick_dc = (np.arange(2 * NA_KC - 1)[:, None, None] == dc[None]).astype(np.float32)
    by_col = jnp.einsum('hdj,jqc->hdcq', rpb.astype(F32) * LOG2E, pick_dc, precision=lax.Precision.HIGHEST)
    by_col = jnp.where(col_ok.T[None, None], by_col, -jnp.inf)
    flat = by_col.reshape(NA_HEADS // 2, 2, ndr * GRID_W, GRID_W)
    lead = NA_KR - 1
    tail = NA_TAB_BLOCKS - ndr - lead

    def shifted(shift):
        return jnp.pad(flat, ((0, 0), (0, 0), ((lead + shift) * GRID_W, (tail - shift) * GRID_W), (0, 0)))
    table = jnp.concatenate([shifted(0), shifted(1)], axis=-1)
    nrb = rows // NA_QROWS
    row_mask = np.full((nrb, NA_QROWS // 2, NA_KROWS, GRID_W, 2, GRID_W), -np.inf, np.float32)
    for rb in range(nrb):
        r0, kstart = _na_block_rows(rb, rows)
        for qr in range(NA_QROWS):
            rs = int(np.clip(r0 + qr - NA_KR // 2, 0, rows - NA_KR))
            for kk in range(NA_KROWS):
                if rs <= kstart + kk < rs + NA_KR:
                    row_mask[rb, qr // 2, kk, :, qr % 2, :] = 0.0
    return table, jnp.asarray(row_mask.reshape(nrb, NA_QROWS // 2, NA_KROWS * GRID_W, 2 * GRID_W))


def _na_attention(geom, proj, k_ctx, v_ctx, rpb):
    npairs = NA_HEADS // 2
    s, ds = geom.s, geom.ds
    oa_p = pl.pallas_call(
        functools.partial(_pair_attn_kernel, has_ctx=False, row_split=1, kv_share=1),
        out_shape=jax.ShapeDtypeStruct((geom.np_rows, NA_W), BF16),
        grid=(geom.b,),
        in_specs=[pl.BlockSpec((s, NA_W), lambda b: (b, 0)),
                  pl.BlockSpec((s, NA_W), lambda b: (b, 1)),
                  pl.BlockSpec((s, NA_W), lambda b: (b, 2))],
        out_specs=pl.BlockSpec((s, NA_W), lambda b: (b, 0)),
        compiler_params=_params(("parallel",)),
        name="na_prompt_attn",
    )(proj, proj, proj)

    rows = ds // GRID_W
    assert rows % NA_QROWS == 0 and rows >= NA_KROWS
    table, row_mask = _na_bias_tables(rpb, rows)
    s0 = geom.np_rows // ds
    kc = k_ctx.reshape(geom.db, npairs, 2, -1, HD).transpose(0, 1, 3, 2, 4).reshape(geom.db, npairs, -1, LANES).astype(BF16)
    vc = v_ctx.reshape(geom.db, npairs, 2, -1, HD).transpose(0, 1, 3, 2, 4).reshape(geom.db, npairs, -1, LANES).astype(BF16)
    nctx = kc.shape[2]
    tab_spec = pl.BlockSpec((None, 2, NA_TAB_BLOCKS * GRID_W, LANES), lambda p, b: (p, 0, 0, 0))
    oa_s = pl.pallas_call(
        _na_kernel,
        out_shape=jax.ShapeDtypeStruct((geom.ns_rows, NA_W), BF16),
        grid=(npairs, geom.db),
        in_specs=[pl.BlockSpec((ds, LANES), lambda p, b: (s0 + b, p)),
                  pl.BlockSpec((ds, LANES), lambda p, b: (s0 + b, npairs + p)),
                  pl.BlockSpec((ds, LANES), lambda p, b: (s0 + b, 2 * npairs + p)),
                  pl.BlockSpec((None, None, nctx, LANES), lambda p, b: (b, p, 0, 0)),
                  pl.BlockSpec((None, None, nctx, LANES), lambda p, b: (b, p, 0, 0)),
                  tab_spec, _const_spec(row_mask.shape)],
        out_specs=pl.BlockSpec((ds, LANES), lambda p, b: (b, p)),
        compiler_params=_params(("parallel", "parallel")),
        name="na_sample_attn",
    )(proj, proj, proj, kc, vc, table, row_mask)
    return oa_p, oa_s


GLA_LEVELS = 6


def _gla_constants():
    c = GLA_CHUNK
    t = np.arange(c)
    i, j = t[:, None], t[None, :]
    out = {}
    for fwd in (True, False):
        tri = (j <= i) if fwd else (j >= i)
        masks = np.zeros((GLA_LEVELS + 1, c, c), np.float32)
        for lvl in range(GLA_LEVELS):
            half = c >> (lvl + 1)
            upper = (t % (2 * half)) >= half
            same = (i // (2 * half)) == (j // (2 * half))
            masks[lvl] = (same & upper[:, None] & ~upper[None, :]) if fwd else (same & ~upper[:, None] & upper[None, :])
        masks[GLA_LEVELS] = np.eye(c)
        out[fwd] = (jnp.asarray(tri.astype(np.float32), BF16),
                    jnp.asarray(np.concatenate([masks, masks], axis=1).reshape(-1, c), F32))
    return out


def _row_of_block(x, block, row):
    c = x.shape[0]
    if block >= SUBLANES:
        return jnp.concatenate(
            [jnp.broadcast_to(x[b * block + row:b * block + row + 1, :], (block, x.shape[1])) for b in range(c // block)],
            axis=0)
    per = SUBLANES // block
    sub = lax.broadcasted_iota(jnp.int32, x.shape, 0) % SUBLANES
    out = _row_of_block(x, SUBLANES, row)
    for p in range(1, per):
        out = jnp.where(sub >= p * block, _row_of_block(x, SUBLANES, p * block + row), out)
    return out


def _gla_cumulative(g, tri):
    g_hi, g_lo = _split(g)
    return _dot(tri, g_hi) + _dot(tri, g_lo)


def _gla_intra(q, k, g, b, masks, fwd):
    c = GLA_CHUNK
    t = lax.broadcasted_iota(jnp.int32, b.shape, 0)
    att = jnp.zeros((2 * c, c), F32)
    for lvl in range(GLA_LEVELS + 1):
        half = c >> (lvl + 1)
        if half >= 2:
            ref = _row_of_block(b, 2 * half, half - 1 if fwd else half)
            past_mid = ((t % (2 * half)) >= half) == fwd
            w = jnp.exp(jnp.where(past_mid, b - ref, ref - b))
            ql, kl = q * w, k * w
        elif half == 1:
            w = jnp.exp(jnp.where((t % 2 == 1) == fwd, g, 0.0))
            ql, kl = q * w, k * w
        else:
            ql, kl = q, k
        p = _dot_nt(_stack_heads(ql).astype(BF16), kl.astype(BF16))
        att = att + p * masks[lvl * 2 * c:(lvl + 1) * 2 * c]
    return att.astype(BF16)


def _gla_outputs(q, k, v, b, att, st, fwd):
    c = GLA_CHUNK
    edge = b[c - 1:c] if fwd else b[0:1]
    q_dec = q * jnp.exp(b)
    o_st = _dot_nt(_stack_heads(q_dec).astype(BF16), st.astype(BF16))
    r = _dot(att, v)
    o = jnp.concatenate([o_st[0:c] + r[0:c, 0:GLA_DV], o_st[c:] + r[c:, GLA_DV:]], axis=1)
    k_dec = (k * jnp.exp(edge - b)).astype(BF16)
    u = _dot_tn(v, k_dec)
    st_new = jnp.exp(edge) * st + jnp.where(_lane_lo(), u[0:GLA_DV], u[GLA_DV:])
    return o, st_new


GLA_GROUP = 4


def _gla_kernel(q_ref, k_ref, v_ref, rg_ref, gf_ref, gb_ref, s0f_ref, s0b_ref, gain_ref,
                tri_f_ref, masks_f_ref, tri_b_ref, masks_b_ref, y_ref, sf_ref, sb_ref,
                of_scr, ob_scr, st_scr):
    t = q_ref.shape[0]
    c = GLA_CHUNK
    nc = t // c
    assert nc % GLA_GROUP == 0
    for d, s0_ref in enumerate((s0f_ref, s0b_ref)):
        st_scr[d] = jnp.concatenate([s0_ref[0], s0_ref[1]], axis=0).T
    per_dir = ((gf_ref, of_scr, tri_f_ref, masks_f_ref), (gb_ref, ob_scr, tri_b_ref, masks_b_ref))

    def body(gi, carry):
        streams = []
        for u in range(GLA_GROUP):
            for d in range(2):
                ci = gi * GLA_GROUP + u
                cc = ci if d == 0 else nc - 1 - ci
                streams.append((d, pl.ds(pl.multiple_of(cc * c, c), c)))
        qkg = [(q_ref[rows, :].astype(F32), k_ref[rows, :].astype(F32), per_dir[d][0][rows, :])
               for d, rows in streams]
        cums = [_gla_cumulative(g, per_dir[d][2][...]) for (d, _), (_, _, g) in zip(streams, qkg)]
        atts = [_gla_intra(q, k, g, b, per_dir[d][3][...], d == 0)
                for (d, _), (q, k, g), b in zip(streams, qkg, cums)]
        for (d, rows), (q, k, _), b, att in zip(streams, qkg, cums, atts):
            o, st_new = _gla_outputs(q, k, v_ref[rows, :], b, att, st_scr[d], d == 0)
            per_dir[d][1][rows, :] = o
            st_scr[d] = st_new
        return carry

    lax.fori_loop(0, nc // GLA_GROUP, body, 0)

    for d, s_ref in enumerate((sf_ref, sb_ref)):
        s_pair = st_scr[d].T
        s_ref[0] = s_pair[0:GLA_DK]
        s_ref[1] = s_pair[GLA_DK:]

    rt = 256
    for r in range(t // rt):
        og = of_scr[r * rt:(r + 1) * rt, :] + ob_scr[r * rt:(r + 1) * rt, :]
        halves = []
        for hh in range(2):
            x = og[:, hh * GLA_DV:(hh + 1) * GLA_DV]
            halves.append(x * lax.rsqrt(jnp.mean(x * x, axis=-1, keepdims=True) + EPS))
        y = jnp.concatenate(halves, axis=1) * gain_ref[...] * _silu(rg_ref[r * rt:(r + 1) * rt, :].astype(F32))
        y_ref[r * rt:(r + 1) * rt, :] = y.astype(BF16)


def _gla_call(proj, gates, s0f, s0b, gain, consts, t, nb, row0, name):
    npairs = GLA_HEADS // 2
    qcol = 3 * NA_W // LANES
    kcol = qcol + GLA_KW // LANES
    vcol = (3 * NA_W + 2 * GLA_KW) // (2 * GLA_DV)
    rcol = vcol + GLA_VW // (2 * GLA_DV)
    tri_f, masks_f = consts[True]
    tri_b, masks_b = consts[False]
    st_spec = pl.BlockSpec((None, 2, GLA_DK, GLA_DV), lambda b, p: (b, p, 0, 0))
    return pl.pallas_call(
        _gla_kernel,
        out_shape=(jax.ShapeDtypeStruct((nb * t, GLA_VW), BF16),
                   jax.ShapeDtypeStruct((nb, GLA_HEADS, GLA_DK, GLA_DV), F32),
                   jax.ShapeDtypeStruct((nb, GLA_HEADS, GLA_DK, GLA_DV), F32)),
        grid=(nb, npairs),
        in_specs=[pl.BlockSpec((t, LANES), lambda b, p: (row0 + b, qcol + p)),
                  pl.BlockSpec((t, LANES), lambda b, p: (row0 + b, kcol + p)),
                  pl.BlockSpec((t, 2 * GLA_DV), lambda b, p: (row0 + b, vcol + p)),
                  pl.BlockSpec((t, 2 * GLA_DV), lambda b, p: (row0 + b, rcol + p)),
                  pl.BlockSpec((t, LANES), lambda b, p: (row0 + b, p)),
                  pl.BlockSpec((t, LANES), lambda b, p: (row0 + b, npairs + p)),
                  st_spec, st_spec,
                  pl.BlockSpec((1, 2 * GLA_DV), lambda b, p: (0, p)),
                  _const_spec(tri_f.shape), _const_spec(masks_f.shape),
                  _const_spec(tri_b.shape), _const_spec(masks_b.shape)],
        out_specs=(pl.BlockSpec((t, 2 * GLA_DV), lambda b, p: (b, p)), st_spec, st_spec),
        scratch_shapes=[pltpu.VMEM((t, 2 * GLA_DV), F32), pltpu.VMEM((t, 2 * GLA_DV), F32),
                        pltpu.VMEM((2, GLA_DV, LANES), F32)],
        compiler_params=_params(("parallel", "parallel")),
        name=name,
    )(proj, proj, proj, proj, gates, gates, s0f, s0b, gain, tri_f, masks_f, tri_b, masks_b)


C_OUT = C_QW + 4 * C_KW


def _c_in_kernel(x_ref, g_ref, sc_ref, sh_ref, w_ref, gain_ref, cos_ref, sin_ref, grp_ref, o_ref):
    h = _norm_mod(x_ref[...], g_ref[...], sc_ref[...], sh_ref[...]).astype(BF16)
    cw = 512
    nch = C_OUT // cw
    n_norm = (C_QW + 2 * C_KW) // LANES
    quarter = HD // 4
    first_half = lax.broadcasted_iota(jnp.int32, (1, LANES), 1) % (2 * quarter) < quarter
    y_ahead = _dot(h, w_ref[:, 0:cw])
    for j in range(nch):
        y = y_ahead
        if j + 1 < nch:
            y_ahead = _dot(h, w_ref[:, (j + 1) * cw:(j + 2) * cw])
        for s in range(cw // LANES):
            blk = j * (cw // LANES) + s
            yb = y[:, s * LANES:(s + 1) * LANES]
            if blk < n_norm:
                ms = _dot((yb * yb).astype(BF16), grp_ref[...])
                yn = yb * lax.rsqrt(ms + EPS) * gain_ref[:, blk * LANES:(blk + 1) * LANES]
                partner = jnp.where(first_half, pltpu.roll(yn, LANES - quarter, 1), pltpu.roll(yn, quarter, 1))
                yb = yn * cos_ref[...] + partner * sin_ref[...]
            o_ref[:, blk * LANES:(blk + 1) * LANES] = yb.astype(BF16)


def _rope_tables(geom):
    tpos = np.arange(geom.ds)
    inv = ROPE_THETA ** (-np.arange(0, HD // 2, 2, dtype=np.float32) / (HD // 2))
    ang_r = (tpos // GRID_W).astype(np.float32)[:, None] * inv
    ang_c = (tpos % GRID_W).astype(np.float32)[:, None] * inv
    ang_r, ang_c = jnp.asarray(ang_r, F32), jnp.asarray(ang_c, F32)
    cos = jnp.concatenate([jnp.cos(ang_r)] * 2 + [jnp.cos(ang_c)] * 2, axis=1)
    sin = jnp.concatenate([-jnp.sin(ang_r), jnp.sin(ang_r), -jnp.sin(ang_c), jnp.sin(ang_c)], axis=1)
    cos = jnp.concatenate([jnp.tile(cos, (1, 2)), jnp.ones((TP, LANES), F32)], axis=0)
    sin = jnp.concatenate([jnp.tile(sin, (1, 2)), jnp.zeros((TP, LANES), F32)], axis=0)
    lane = np.arange(LANES)
    grp = (lane[:, None] // HD == lane[None, :] // HD).astype(np.float32) / HD
    return cos, sin, jnp.asarray(grp, BF16)


def _c_in_proj(geom, x, mod_l, g, w, gain, tables):
    cos, sin, grp = tables
    npt, tps = geom.np_rows // TP, geom.ds // TP

    def pos_map(i):
        return (jnp.where(i < npt, tps, (i - npt) % tps), 0)
    return pl.pallas_call(
        _c_in_kernel,
        out_shape=jax.ShapeDtypeStruct((geom.n, C_OUT), BF16),
        grid=(geom.n // TP,),
        in_specs=[pl.BlockSpec((TP, D), lambda i: (i, 0)),
                  _const_spec((1, D)),
                  geom.mod_spec(1, TP), geom.mod_spec(0, TP),
                  _const_spec(w.shape), _const_spec(gain.shape),
                  pl.BlockSpec((TP, LANES), pos_map), pl.BlockSpec((TP, LANES), pos_map),
                  _const_spec(grp.shape)],
        out_specs=pl.BlockSpec((TP, C_OUT), lambda i: (i, 0)),
        compiler_params=_params(("parallel",)),
        name="gqa_in_proj",
    )(x, g, mod_l, mod_l, w, gain, cos, sin, grp)


def _gqa_attention(geom, proj, k_ctx, v_ctx):
    nqp = GQA_HEADS // 2
    kcol = C_QW // LANES
    vcol = kcol + 2 * C_KW // LANES
    s, ds = geom.s, geom.ds
    kvw = 2 * C_KW
    o_p = pl.pallas_call(
        functools.partial(_pair_attn_kernel, has_ctx=False, row_split=1, kv_share=2),
        out_shape=jax.ShapeDtypeStruct((geom.np_rows, C_QW), BF16),
        grid=(geom.b,),
        in_specs=[pl.BlockSpec((s, C_QW), lambda b: (b, 0)),
                  pl.BlockSpec((s, kvw), lambda b: (b, C_QW // kvw)),
                  pl.BlockSpec((s, kvw), lambda b: (b, C_QW // kvw + 1))],
        out_specs=pl.BlockSpec((s, C_QW), lambda b: (b, 0)),
        compiler_params=_params(("parallel",)),
        name="gqa_prompt_attn",
    )(proj, proj, proj)

    tq = ds
    nqt = ds // tq
    q0 = geom.np_rows // tq
    s0 = geom.np_rows // ds
    kc = jnp.concatenate([k_ctx, k_ctx], axis=-1).astype(BF16)
    vc = jnp.concatenate([v_ctx, v_ctx], axis=-1).astype(BF16)
    nctx = kc.shape[2]
    o_s = pl.pallas_call(
        functools.partial(_shared_kv_attn_kernel, has_ctx=True, row_split=tq // 256, kv_share=1),
        out_shape=jax.ShapeDtypeStruct((geom.ns_rows, C_QW), BF16),
        grid=(geom.db, nqp, nqt),
        in_specs=[pl.BlockSpec((tq, LANES), lambda b, m, t: (q0 + nqt * b + t, m)),
                  pl.BlockSpec((ds, LANES), lambda b, m, t: (s0 + b, kcol + m // 2)),
                  pl.BlockSpec((ds, LANES), lambda b, m, t: (s0 + b, vcol + m // 2)),
                  pl.BlockSpec((None, None, nctx, LANES), lambda b, m, t: (b, m // 2, 0, 0)),
                  pl.BlockSpec((None, None, nctx, LANES), lambda b, m, t: (b, m // 2, 0, 0))],
        out_specs=pl.BlockSpec((tq, LANES), lambda b, m, t: (nqt * b + t, m)),
        compiler_params=_params(("parallel", "parallel", "parallel")),
        name="gqa_sample_attn",
    )(proj, proj, proj, kc, vc)
    return o_p, o_s


def _ffn_kernel(*refs, npt, tps, seq, x_split, n_act):
    i = pl.program_id(0)
    j = jnp.maximum(i - npt, 0) % tps
    is_prompt = i < npt
    has_prev = jnp.logical_and(jnp.logical_not(is_prompt), j > 0).astype(F32)
    has_next = jnp.logical_and(jnp.logical_not(is_prompt), j < tps - 1).astype(F32)
    nx = 4 if x_split else 3
    x_refs = refs[:nx]
    act_refs = refs[nx:nx + 4 * n_act]
    (wout_ref, gate1_ref, g_ref, sc_ref, sh_ref, gate_ref, wup_ref, cw_ref, cb_ref, wdn_ref,
     o_ref, acc_ref, h_ref, xm_ref) = refs[nx + 4 * n_act:]
    x_main = _read_split(x_refs[0], x_refs[1], npt) if x_split else x_refs[0][...]
    mix = None
    k0 = 0
    for a in range(n_act):
        a_p, a_s, a_prev, a_next = act_refs[4 * a:4 * a + 4]
        act = jnp.concatenate([a_prev[...], _read_split(a_p, a_s, npt), a_next[...]], axis=0)
        part = _dot(act, wout_ref[k0:k0 + act.shape[1], :])
        mix = part if mix is None else mix + part
        k0 += act.shape[1]
    xm_ref[...] = jnp.concatenate([x_refs[-2][...], x_main, x_refs[-1][...]], axis=0) + gate1_ref[...] * mix
    g, sc, sh = g_ref[...], sc_ref[...], sh_ref[...]
    x = xm_ref[HALO:HALO + TF, :]
    h_ref[0:HALO, :] = (_norm_mod(xm_ref[0:HALO, :], g, sc, sh) * has_prev).astype(BF16)
    h_ref[HALO:HALO + TF, :] = _norm_mod(x, g, sc, sh).astype(BF16)
    h_ref[HALO + TF:, :] = (_norm_mod(xm_ref[HALO + TF:, :], g, sc, sh) * has_next).astype(BF16)
    rows = TF + 2 * HALO
    nch = D_FF // FF_CHUNK
    inner = list(range(seq, TF, seq))
    sub = lax.broadcasted_iota(jnp.int32, (SUBLANES, FF_CHUNK), 0)
    no_prev = jnp.where(jnp.logical_and(is_prompt, sub == 0), 0.0, 1.0)
    no_next = jnp.where(jnp.logical_and(is_prompt, sub == SUBLANES - 1), 0.0, 1.0)

    def up(c):
        return [_dot(h_ref[...], wup_ref[:, off + c * FF_CHUNK:off + (c + 1) * FF_CHUNK]) for off in (0, D_FF)]

    def conv(u, off, c):
        cols = slice(off + c * FF_CHUNK, off + (c + 1) * FF_CHUNK)
        u_prev = pltpu.roll(u, 1, 0)[HALO:HALO + TF]
        u_next = pltpu.roll(u, rows - 1, 0)[HALO:HALO + TF]
        for b in inner:
            u_prev = jnp.concatenate([u_prev[:b], u_prev[b:b + SUBLANES] * no_prev, u_prev[b + SUBLANES:]], axis=0)
            u_next = jnp.concatenate([u_next[:b - SUBLANES], u_next[b - SUBLANES:b] * no_next, u_next[b:]], axis=0)
        return (cw_ref[0:1, cols] * u_prev + cw_ref[1:2, cols] * u[HALO:HALO + TF]
                + cw_ref[2:3, cols] * u_next + cb_ref[:, cols])

    def down(c, a):
        part = _dot(a, wdn_ref[c * FF_CHUNK:(c + 1) * FF_CHUNK, :])
        if c == 0:
            acc_ref[...] = part
        else:
            acc_ref[...] += part

    u_ahead = up(0)
    a_prev = None
    for c in range(nch):
        u_val, u_gate = u_ahead
        if c + 1 < nch:
            u_ahead = up(c + 1)
        if a_prev is not None:
            down(c - 1, a_prev)
        a_prev = (_silu(conv(u_gate, D_FF, c)) * conv(u_val, 0, c)).astype(BF16)
    down(nch - 1, a_prev)
    o_ref[...] = x + gate_ref[...] * acc_ref[...]


def _mixer_out_ffn(geom, xs, act_pairs, w_out, mod_l, g, wup, cw, cb, wdn):
    assert TF % geom.s == 0 and geom.np_rows % TF == 0 and geom.ds % TF == 0
    per = TF // HALO
    npt = geom.np_rows // TF

    def halo_specs(width, nblk, first_tile):
        def prev(i):
            return (jnp.maximum((jnp.maximum(i - first_tile, 0)) * per - 1, 0), 0)

        def nxt(i):
            return (jnp.minimum((jnp.maximum(i - first_tile, 0) + 1) * per, nblk - 1), 0)
        return [pl.BlockSpec((HALO, width), prev), pl.BlockSpec((HALO, width), nxt)]

    x_split = len(xs) == 2
    if x_split:
        in_specs = geom.split_specs(TF, D) + halo_specs(D, geom.ns_rows // HALO, npt)
        args = [xs[0], xs[1], xs[1], xs[1]]
    else:
        in_specs = [pl.BlockSpec((TF, D), lambda i: (i, 0))] + halo_specs(D, geom.n // HALO, 0)
        args = [xs[0], xs[0], xs[0]]
    for a_p, a_s in act_pairs:
        in_specs += geom.split_specs(TF, a_p.shape[1]) + halo_specs(a_p.shape[1], geom.ns_rows // HALO, npt)
        args += [a_p, a_s, a_s, a_s]
    in_specs += [_const_spec(w_out.shape), geom.mod_spec(2, TF), _const_spec((1, D)),
                 geom.mod_spec(4, TF), geom.mod_spec(3, TF), geom.mod_spec(5, TF),
                 _const_spec(wup.shape), _const_spec(cw.shape), _const_spec(cb.shape), _const_spec(wdn.shape)]
    args += [w_out, mod_l, g, mod_l, mod_l, mod_l, wup, cw, cb, wdn]
    return pl.pallas_call(
        functools.partial(_ffn_kernel, npt=npt, tps=geom.ds // TF, seq=geom.s, x_split=x_split,
                          n_act=len(act_pairs)),
        out_shape=jax.ShapeDtypeStruct((geom.n, D), F32),
        grid=(geom.n // TF,),
        in_specs=in_specs,
        out_specs=pl.BlockSpec((TF, D), lambda i: (i, 0)),
        scratch_shapes=[pltpu.VMEM((TF, D), F32), pltpu.VMEM((TF + 2 * HALO, D), BF16),
                        pltpu.VMEM((TF + 2 * HALO, D), F32)],
        compiler_params=_params(("parallel",)),
        name="mixer_out_ffn",
    )(*args)


def _final_norm_kernel(x_ref, g_ref, o_ref):
    x = x_ref[...]
    o_ref[...] = x * lax.rsqrt(jnp.mean(x * x, axis=-1, keepdims=True) + EPS) * g_ref[...]


def _final_norm(x, g, row0, nrows):
    tn = 1024
    assert row0 % tn == 0 and nrows % tn == 0
    t0 = row0 // tn
    return pl.pallas_call(
        _final_norm_kernel,
        out_shape=jax.ShapeDtypeStruct((nrows, D), F32),
        grid=(nrows // tn,),
        in_specs=[pl.BlockSpec((tn, D), lambda i: (t0 + i, 0)), _const_spec((1, D))],
        out_specs=pl.BlockSpec((tn, D), lambda i: (i, 0)),
        compiler_params=_params(("parallel",)),
        name="final_norm",
    )(x, g)


def _prep_ab(w_in, w2_f, b2_f, w2_b, b2_b):
    scale = np.ones((w_in.shape[1],), np.float32)
    scale[0:NA_W] = HD ** -0.5 * LOG2E
    scale[3 * NA_W:3 * NA_W + GLA_KW] = GLA_DK ** -0.5
    w = jnp.pad(w_in * scale, ((0, 0), (0, AB_MAIN + LANES - w_in.shape[1]))).astype(BF16)
    w2 = jnp.zeros((LANES, 2 * GLA_KW), F32)
    w2 = w2.at[0:GLA_RANK, 0:GLA_KW].set(w2_f).at[GLA_RANK:2 * GLA_RANK, GLA_KW:].set(w2_b)
    w2_hi, w2_lo = _split(w2)
    b2 = jnp.concatenate([b2_f, b2_b])[None, :]
    return w, w2_hi, w2_lo, b2


def _prep_c(w_in, qn_g, kn_g):
    wq, wk, wv = w_in[:, :C_QW], w_in[:, C_QW:C_QW + C_KW], w_in[:, C_QW + C_KW:]

    def dup(w):
        return jnp.broadcast_to(w.reshape(D, GQA_KVH, 1, HD), (D, GQA_KVH, 2, HD)).reshape(D, 2 * C_KW)
    w = jnp.concatenate([wq, dup(wk), dup(wv)], axis=1).astype(BF16)
    gain = jnp.concatenate([jnp.tile(qn_g * (HD ** -0.5 * LOG2E), GQA_HEADS), jnp.tile(kn_g, 2 * GQA_KVH)])[None, :]
    return w, gain


def kernel(x_prompt, x_sample, cache_na_k, cache_na_v, state_gla_fwd, state_gla_bwd, cache_gqa_k, cache_gqa_v, c, c_ctx, ada_w, ada_b, norm_mix_g, norm_ffn_g, ab_w_in, ab_w_out, na_rpb, gla_w2_fwd, gla_b2_fwd, gla_w2_bwd, gla_b2_bwd, gla_norm_g, gqa_w_in, gqa_w_out, gqa_q_norm_g, gqa_k_norm_g, ffn_w_up, ffn_conv_w, ffn_conv_b, ffn_w_down, final_norm_g):
    b, s, _ = x_prompt.shape
    db, ds, _ = x_sample.shape
    geom = _Geom(b, s, db, ds)
    assert ds % GRID_W == 0 and geom.np_rows % ds == 0 and geom.np_rows % TP == 0

    xs = [x_prompt.reshape(-1, D), x_sample.reshape(-1, D)]
    nrow = -(-(db + 1) // SUBLANES) * SUBLANES
    c_rows = jnp.zeros((nrow, D), F32).at[:db].set(c).at[db].set(c_ctx)
    mod = _modulation(c_rows, ada_w, ada_b).reshape(DEPTH, nrow, 1, 6 * D)

    gla_consts = _gla_constants()
    rope = _rope_tables(geom)
    zeros_state = jnp.zeros((b, GLA_HEADS, GLA_DK, GLA_DV), F32)
    na_k, na_v, gla_f, gla_b, gq_k, gq_v = [], [], [], [], [], []

    for i in range(DEPTH):
        j = i // 2
        mod_l = mod[i]
        if i % 2 == 0:
            w, w2_hi, w2_lo, b2 = _prep_ab(ab_w_in[j], gla_w2_fwd[j], gla_b2_fwd[j], gla_w2_bwd[j], gla_b2_bwd[j])
            proj, gates = _ab_in_proj(geom, xs, mod_l, norm_mix_g[i][None, :], w, w2_hi, w2_lo, b2)
            oa = _na_attention(geom, proj, cache_na_k[:, j], cache_na_v[:, j], na_rpb[j])
            gain = gla_norm_g[j][None, :]
            yg_p, sf, sb = _gla_call(proj, gates, zeros_state, zeros_state, gain, gla_consts, s, b, 0, "gla_prompt")
            yg_s, _, _ = _gla_call(proj, gates, state_gla_fwd[:, j], state_gla_bwd[:, j], gain, gla_consts,
                                   ds, db, geom.np_rows // ds, "gla_sample")
            acts, w_out = [oa, (yg_p, yg_s)], ab_w_out[j]
            pk = proj[:geom.np_rows].reshape(b, s, -1)
            na_k.append(pk[..., NA_W:2 * NA_W].reshape(b, s, NA_HEADS, HD).transpose(0, 2, 1, 3).astype(F32))
            na_v.append(pk[..., 2 * NA_W:3 * NA_W].reshape(b, s, NA_HEADS, HD).transpose(0, 2, 1, 3).astype(F32))
            gla_f.append(sf)
            gla_b.append(sb)
        else:
            w, gain = _prep_c(gqa_w_in[j], gqa_q_norm_g[j], gqa_k_norm_g[j])
            proj = _c_in_proj(geom, xs[0], mod_l, norm_mix_g[i][None, :], w, gain, rope)
            o = _gqa_attention(geom, proj, cache_gqa_k[:, j], cache_gqa_v[:, j])
            acts, w_out = [o], gqa_w_out[j]
            pk = proj[:geom.np_rows].reshape(b, s, -1)
            kd = pk[..., C_QW:C_QW + 2 * C_KW].reshape(b, s, GQA_KVH, 2, HD)[:, :, :, 0]
            vd = pk[..., C_QW + 2 * C_KW:].reshape(b, s, GQA_KVH, 2, HD)[:, :, :, 0]
            gq_k.append(kd.transpose(0, 2, 1, 3).astype(F32))
            gq_v.append(vd.transpose(0, 2, 1, 3).astype(F32))
        x = _mixer_out_ffn(geom, xs, acts, w_out.astype(BF16), mod_l, norm_ffn_g[i][None, :],
                           ffn_w_up[i].astype(BF16), ffn_conv_w[i], ffn_conv_b[i][None, :], ffn_w_down[i].astype(BF16))
        xs = [x]

    y_prompt = _final_norm(x, final_norm_g[None, :], 0, geom.np_rows).reshape(b, s, D)
    y_sample = _final_norm(x, final_norm_g[None, :], geom.np_rows, geom.ns_rows).reshape(db, ds, D)
    return (y_prompt, y_sample,
            jnp.stack(na_k, axis=1), jnp.stack(na_v, axis=1),
            jnp.stack(gla_f, axis=1), jnp.stack(gla_b, axis=1),
            jnp.stack(gq_k, axis=1), jnp.stack(gq_v, axis=1))
```

```python
import functools

import numpy as np
import jax
import jax.numpy as jnp
from jax import lax
from jax.experimental import pallas as pl
from jax.experimental.pallas import tpu as pltpu

F32 = jnp.float32
BF16 = jnp.bfloat16

D = 1024
DEPTH = 4
HD = 64
GRID_W = 64
EPS = 1e-6
NA_HEADS = 8
NA_W = NA_HEADS * HD
NA_KR = 8
NA_KC = 16
GLA_HEADS = 4
GLA_DK = 64
GLA_DV = 128
GLA_KW = GLA_HEADS * GLA_DK
GLA_VW = GLA_HEADS * GLA_DV
GLA_RANK = 16
GLA_CHUNK = 64
GQA_HEADS = 16
GQA_KVH = 4
C_QW = GQA_HEADS * HD
C_KW = GQA_KVH * HD
ROPE_THETA = 10000.0
D_FF = 2816
AB_MAIN = 3 * NA_W + 2 * GLA_KW + 2 * GLA_VW

LANES = 128
SUBLANES = 8
TM = 256
TP = 512
TF = 256
LOG2E = 1.4426950408889634
HALO = 16
FF_CHUNK = 256
NA_QROWS = 8
NA_KROWS = NA_QROWS + NA_KR - 1
VMEM_LIMIT = 56 * 1024 * 1024


def _dot(a, b):
    return jnp.dot(a, b, preferred_element_type=F32)


def _dot_nt(a, b):
    return lax.dot_general(a, b, (((1,), (1,)), ((), ())), preferred_element_type=F32)


def _dot_tn(a, b):
    return lax.dot_general(a, b, (((0,), (0,)), ((), ())), preferred_element_type=F32)


def _split(a):
    hi = a.astype(BF16)
    lo = (a - hi.astype(F32)).astype(BF16)
    return hi, lo


def _dot_split(a, w_hi, w_lo):
    a_hi, a_lo = _split(a)
    return _dot(a_hi, w_hi) + _dot(a_lo, w_hi) + _dot(a_hi, w_lo)


def _norm_mod(x, g, sc, sh):
    xn = x * lax.rsqrt(jnp.mean(x * x, axis=-1, keepdims=True) + EPS)
    return (xn * g) * (1.0 + sc) + sh


def _silu(x):
    return x / (1.0 + jnp.exp(-x))


def _lane_lo(shape=(1, LANES)):
    return lax.broadcasted_iota(jnp.int32, shape, len(shape) - 1) < HD


def _stack_heads(x):
    lo = _lane_lo()
    zero = jnp.zeros_like(x)
    return jnp.concatenate([jnp.where(lo, x, zero), jnp.where(lo, zero, x)], axis=0)


def _params(sem):
    return pltpu.CompilerParams(dimension_semantics=sem, vmem_limit_bytes=VMEM_LIMIT)


def _mod_kernel(c_ref, w_ref, b_ref, o_ref):
    a = _silu(c_ref[...])
    w = w_ref[...]
    w_hi, w_lo = _split(w)
    o_ref[...] = _dot_split(a, w_hi, w_lo) + b_ref[...]


def _modulation(c_rows, ada_w, ada_b):
    nrow = c_rows.shape[0]
    tn = 1024
    return pl.pallas_call(
        _mod_kernel,
        out_shape=jax.ShapeDtypeStruct((DEPTH, nrow, 6 * D), F32),
        grid=(DEPTH, 6 * D // tn),
        in_specs=[pl.BlockSpec((nrow, D), lambda l, n: (0, 0)),
                  pl.BlockSpec((None, D, tn), lambda l, n: (l, 0, n)),
                  pl.BlockSpec((None, 1, tn), lambda l, n: (l, 0, n))],
        out_specs=pl.BlockSpec((None, nrow, tn), lambda l, n: (l, 0, n)),
        compiler_params=_params(("parallel", "parallel")),
        name="modulation",
    )(c_rows, ada_w, ada_b.reshape(DEPTH, 1, 6 * D))


class _Geom:
    def __init__(self, b, s, db, ds):
        assert s == TM and ds % TM == 0
        self.b, self.s, self.db, self.ds = b, s, db, ds
        self.np_rows = b * s
        self.ns_rows = db * ds
        self.n = self.np_rows + self.ns_rows
        self.npt = self.np_rows // TM
        self.tps = ds // TM
        self.ctx_row = db

    def mod_spec(self, k, tm=TM):
        assert self.np_rows % tm == 0 and self.ds % tm == 0
        npt, tps, ctx = self.np_rows // tm, self.ds // tm, self.ctx_row

        def imap(i):
            return (jnp.where(i < npt, ctx, (i - npt) // tps), 0, k)
        return pl.BlockSpec((None, 1, D), imap)

    def split_specs(self, tm, width):
        npt = self.np_rows // tm
        return [pl.BlockSpec((tm, width), lambda i: (jnp.minimum(i, npt - 1), 0)),
                pl.BlockSpec((tm, width), lambda i: (jnp.maximum(i - npt, 0), 0))]


def _read_split(p_ref, s_ref, npt):
    return jnp.where(pl.program_id(0) < npt, p_ref[...], s_ref[...])


def _const_spec(shape):
    nd = len(shape)
    return pl.BlockSpec(shape, lambda *_: (0,) * nd)


def _ab_in_kernel(*refs, split_npt):
    nx = 2 if split_npt else 1
    g_ref, sc_ref, sh_ref, w_ref, w2hi_ref, w2lo_ref, b2_ref, o_ref, gate_ref = refs[nx:]
    x = _read_split(refs[0], refs[1], split_npt) if split_npt else refs[0][...]
    h = _norm_mod(x, g_ref[...], sc_ref[...], sh_ref[...]).astype(BF16)
    cw = 512
    lr = _dot(h, w_ref[:, AB_MAIN:AB_MAIN + LANES])
    o_ref[:, 0:cw] = _dot(h, w_ref[:, 0:cw]).astype(BF16)
    z = _dot_split(lr, w2hi_ref[...], w2lo_ref[...]) + b2_ref[...]
    for j in range(1, AB_MAIN // cw):
        o_ref[:, j * cw:(j + 1) * cw] = _dot(h, w_ref[:, j * cw:(j + 1) * cw]).astype(BF16)
    gate_ref[...] = (jnp.minimum(z, 0.0) - jnp.log(1.0 + jnp.exp(-jnp.abs(z)))) * (1.0 / 16.0)


def _ab_in_proj(geom, xs, mod_l, g, w, w2hi, w2lo, b2):
    split = len(xs) == 2
    x_specs = geom.split_specs(TP, D) if split else [pl.BlockSpec((TP, D), lambda i: (i, 0))]
    return pl.pallas_call(
        functools.partial(_ab_in_kernel, split_npt=geom.np_rows // TP if split else 0),
        out_shape=(jax.ShapeDtypeStruct((geom.n, AB_MAIN), BF16),
                   jax.ShapeDtypeStruct((geom.n, 2 * GLA_KW), F32)),
        grid=(geom.n // TP,),
        in_specs=x_specs + [_const_spec((1, D)),
                            geom.mod_spec(1, TP), geom.mod_spec(0, TP),
                            _const_spec(w.shape), _const_spec(w2hi.shape), _const_spec(w2lo.shape),
                            _const_spec(b2.shape)],
        out_specs=(pl.BlockSpec((TP, AB_MAIN), lambda i: (i, 0)),
                   pl.BlockSpec((TP, 2 * GLA_KW), lambda i: (i, 0))),
        compiler_params=_params(("parallel",)),
        name="ab_in_proj",
    )(*xs, g, mod_l, mod_l, w, w2hi, w2lo, b2)


def _softmax_units(n_units, scores, values, emit):
    def finish(u, p_parts, l):
        emit(u, sum(_dot(p, v) for p, v in zip(p_parts, values(u))) / l)

    s_ahead = scores(0)
    pending = None
    for u in range(n_units):
        s_parts = s_ahead
        if u + 1 < n_units:
            s_ahead = scores(u + 1)
        if pending is not None:
            finish(u - 1, *pending)
        m = functools.reduce(jnp.maximum, [jnp.max(s, axis=-1, keepdims=True) for s in s_parts])
        p_parts = [jnp.exp2(s - m) for s in s_parts]
        l = sum(jnp.sum(p, axis=-1, keepdims=True) for p in p_parts)
        pending = ([p.astype(BF16) for p in p_parts], l)
    finish(n_units - 1, *pending)


def _head_of_pair(x, hh):
    lo = _lane_lo()
    zero = jnp.zeros_like(x)
    return jnp.where(lo, x, zero) if hh == 0 else jnp.where(lo, zero, x)


def _pair_attn_kernel(*refs, has_ctx, row_split, kv_share):
    if has_ctx:
        q_ref, k_ref, v_ref, kc_ref, vc_ref, o_ref = refs
    else:
        q_ref, k_ref, v_ref, o_ref = refs
    npairs = q_ref.shape[1] // LANES
    rows = q_ref.shape[0] // row_split
    units = [(m, r, hh) for m in range(npairs) for r in range(row_split) for hh in range(2)]
    outs = {}

    def kv_cols(m):
        c0 = (m // kv_share) * LANES
        return slice(c0, c0 + LANES)

    def scores(u):
        m, r, hh = units[u]
        qb = _head_of_pair(q_ref[r * rows:(r + 1) * rows, m * LANES:(m + 1) * LANES], hh)
        parts = [_dot_nt(qb, k_ref[:, kv_cols(m)])]
        if has_ctx:
            parts.append(_dot_nt(qb, kc_ref[...]))
        return parts

    def values(u):
        m = units[u][0]
        return [v_ref[:, kv_cols(m)]] + ([vc_ref[...]] if has_ctx else [])

    def emit(u, out):
        outs[units[u]] = out

    _softmax_units(len(units), scores, values, emit)
    lo = _lane_lo()
    for m in range(npairs):
        for r in range(row_split):
            o_ref[r * rows:(r + 1) * rows, m * LANES:(m + 1) * LANES] = jnp.where(
                lo, outs[(m, r, 0)], outs[(m, r, 1)]).astype(BF16)


def _shared_kv_attn_kernel(*refs, has_ctx, row_split, kv_share):
    if has_ctx:
        q_ref, k_ref, v_ref, kc_ref, vc_ref, o_ref = refs
    else:
        q_ref, k_ref, v_ref, o_ref = refs
    npairs = q_ref.shape[1] // LANES
    rows = q_ref.shape[0] // row_split
    units = [(m, r, hh) for m in range(npairs) for r in range(row_split) for hh in range(2)]
    outs = {}

    def kv_cols(m):
        c0 = (m // kv_share) * LANES
        return slice(c0, c0 + LANES)

    def values_t(ref, cols):
        vt = ref[:, cols].astype(F32).T
        return jnp.where(lax.broadcasted_iota(jnp.int32, vt.shape, 0) < HD, vt, 1.0).astype(BF16)

    vt_cache = {}

    def values(u):
        m = units[u][0]
        key = m // kv_share
        if key not in vt_cache:
            vt_cache[key] = [values_t(v_ref, kv_cols(m))] + ([values_t(vc_ref, slice(None))] if has_ctx else [])
        return vt_cache[key]

    def scores(u):
        m, r, hh = units[u]
        qb = _head_of_pair(q_ref[r * rows:(r + 1) * rows, m * LANES:(m + 1) * LANES], hh)
        parts = [_dot_nt(k_ref[:, kv_cols(m)], qb)]
        if has_ctx:
            parts.append(_dot_nt(kc_ref[...], qb))
        return parts

    def finish(u, p_parts):
        ext = sum(_dot(vt, p) for vt, p in zip(values(u), p_parts))
        outs[units[u]] = ext[0:HD] / ext[HD:HD + 1]

    s_ahead = scores(0)
    pending = None
    for u in range(len(units)):
        s_parts = s_ahead
        if u + 1 < len(units):
            s_ahead = scores(u + 1)
        if pending is not None:
            finish(u - 1, pending)
        m_col = functools.reduce(jnp.maximum, [jnp.max(s, axis=0, keepdims=True) for s in s_parts])
        pending = [jnp.exp2(s - m_col).astype(BF16) for s in s_parts]
    finish(len(units) - 1, pending)
    for m in range(npairs):
        for r in range(row_split):
            both = jnp.concatenate([outs[(m, r, 0)], outs[(m, r, 1)]], axis=0)
            o_ref[r * rows:(r + 1) * rows, m * LANES:(m + 1) * LANES] = both.T.astype(BF16)


def _na_block_rows(rb, rows):
    r0 = rb * NA_QROWS
    return r0, int(np.clip(r0 - NA_KR // 2, 0, rows - NA_KROWS))


def _na_kernel(q_ref, k_ref, v_ref, kc_ref, vc_ref, tab_ref, rowmask_ref, o_ref):
    rows = q_ref.shape[0] // GRID_W
    nrb = rows // NA_QROWS
    nq = NA_QROWS * GRID_W
    nk = NA_KROWS * GRID_W
    units = [(rb, hh) for rb in range(nrb) for hh in range(2)]
    outs = {}

    def probabilities(u, s_loc, s_ctx):
        rb, hh = units[u]
        r0, kstart = _na_block_rows(rb, rows)
        loc_cols, ctx_cols = [], []
        for jp in range(NA_QROWS // 2):
            starts = [int(np.clip(r0 + 2 * jp + e - NA_KR // 2, 0, rows - NA_KR)) - kstart for e in (0, 1)]
            lo, hi = min(starts), max(starts) + NA_KR
            first = kstart - (r0 + 2 * jp) + 2 * (NA_KR - 1)
            lanes = slice(jp * LANES, (jp + 1) * LANES)
            s = (s_loc[lo * GRID_W:hi * GRID_W, lanes]
                 + tab_ref[hh, (first + lo) * GRID_W:(first + hi) * GRID_W, :]
                 + rowmask_ref[rb, jp, lo * GRID_W:hi * GRID_W, :])
            sc = s_ctx[:, lanes]
            m_col = jnp.maximum(jnp.max(s, axis=0, keepdims=True), jnp.max(sc, axis=0, keepdims=True))
            pieces = [jnp.exp2(s - m_col).astype(BF16)]
            if lo:
                pieces.insert(0, jnp.zeros((lo * GRID_W, LANES), BF16))
            if hi < NA_KROWS:
                pieces.append(jnp.zeros(((NA_KROWS - hi) * GRID_W, LANES), BF16))
            loc_cols.append(jnp.concatenate(pieces, axis=0))
            ctx_cols.append(jnp.exp2(sc - m_col).astype(BF16))
        return [jnp.concatenate(loc_cols, axis=1), jnp.concatenate(ctx_cols, axis=1)]

    def keys_at(rb):
        k0 = _na_block_rows(rb, rows)[1] * GRID_W
        return slice(k0, k0 + nk)

    def scores(u):
        rb, hh = units[u]
        qb = _head_of_pair(q_ref[rb * nq:(rb + 1) * nq, :], hh)
        return [_dot_nt(k_ref[keys_at(rb), :], qb), _dot_nt(kc_ref[...], qb)]

    vt_cache = {}

    def with_ones(v):
        vt = v.astype(F32).T.astype(BF16)
        return jnp.concatenate([vt, jnp.ones((HALO, vt.shape[1]), BF16)], axis=0)

    def values_t(rb):
        if "ctx" not in vt_cache:
            vt_cache["ctx"] = with_ones(vc_ref[...])
        if rb not in vt_cache:
            vt_cache[rb] = with_ones(v_ref[keys_at(rb), :])
        return [vt_cache[rb], vt_cache["ctx"]]

    def finish(u, p_parts):
        rb, hh = units[u]
        ext = sum(_dot(vt, p) for vt, p in zip(values_t(rb), p_parts))
        outs[units[u]] = ext[hh * HD:(hh + 1) * HD] / ext[LANES:LANES + 1]

    s_ahead = scores(0)
    pending = None
    for u in range(len(units)):
        s_parts = s_ahead
        if u + 1 < len(units):
            s_ahead = scores(u + 1)
        if pending is not None:
            finish(u - 1, pending)
        pending = probabilities(u, *s_parts)
    finish(len(units) - 1, pending)
    for rb in range(nrb):
        both = jnp.concatenate([outs[(rb, 0)], outs[(rb, 1)]], axis=0)
        o_ref[rb * nq:(rb + 1) * nq, :] = both.T.astype(BF16)


NA_TAB_BLOCKS = 30


def _na_bias_tables(rpb, rows):
    ndr = 2 * NA_KR - 1
    col = np.arange(GRID_W)
    cs = np.clip(col - NA_KC // 2, 0, GRID_W - NA_KC)
    col_ok = (col[None, :] >= cs[:, None]) & (col[None, :] < cs[:, None] + NA_KC)
    dc = np.clip(col[None, :] - col[:, None] + NA_KC - 1, 0, 2 * NA_KC - 2)
    pick_dc = (np.arange(2 * NA_KC - 1)[:, None, None] == dc[None]).astype(np.float32)
    by_col = jnp.einsum('hdj,jqc->hdcq', rpb.astype(F32) * LOG2E, pick_dc, precision=lax.Precision.HIGHEST)
    by_col = jnp.where(col_ok.T[None, None], by_col, -jnp.inf)
    flat = by_col.reshape(NA_HEADS // 2, 2, ndr * GRID_W, GRID_W)
    lead = NA_KR - 1
    tail = NA_TAB_BLOCKS - ndr - lead

    def shifted(shift):
        return jnp.pad(flat, ((0, 0), (0, 0), ((lead + shift) * GRID_W, (tail - shift) * GRID_W), (0, 0)))
    table = jnp.concatenate([shifted(0), shifted(1)], axis=-1)
    nrb = rows // NA_QROWS
    row_mask = np.full((nrb, NA_QROWS // 2, NA_KROWS, GRID_W, 2, GRID_W), -np.inf, np.float32)
    for rb in range(nrb):
        r0, kstart = _na_block_rows(rb, rows)
        for qr in range(NA_QROWS):
            rs = int(np.clip(r0 + qr - NA_KR // 2, 0, rows - NA_KR))
            for kk in range(NA_KROWS):
                if rs <= kstart + kk < rs + NA_KR:
                    row_mask[rb, qr // 2, kk, :, qr % 2, :] = 0.0
    return table, jnp.asarray(row_mask.reshape(nrb, NA_QROWS // 2, NA_KROWS * GRID_W, 2 * GRID_W))


def _na_attention(geom, proj, k_ctx, v_ctx, rpb):
    npairs = NA_HEADS // 2
    s, ds = geom.s, geom.ds
    oa_p = pl.pallas_call(
        functools.partial(_pair_attn_kernel, has_ctx=False, row_split=1, kv_share=1),
        out_shape=jax.ShapeDtypeStruct((geom.np_rows, NA_W), BF16),
        grid=(geom.b,),
        in_specs=[pl.BlockSpec((s, NA_W), lambda b: (b, 0)),
                  pl.BlockSpec((s, NA_W), lambda b: (b, 1)),
                  pl.BlockSpec((s, NA_W), lambda b: (b, 2))],
        out_specs=pl.BlockSpec((s, NA_W), lambda b: (b, 0)),
        compiler_params=_params(("parallel",)),
        name="na_prompt_attn",
    )(proj, proj, proj)

    rows = ds // GRID_W
    assert rows % NA_QROWS == 0 and rows >= NA_KROWS
    table, row_mask = _na_bias_tables(rpb, rows)
    s0 = geom.np_rows // ds
    kc = k_ctx.reshape(geom.db, npairs, 2, -1, HD).transpose(0, 1, 3, 2, 4).reshape(geom.db, npairs, -1, LANES).astype(BF16)
    vc = v_ctx.reshape(geom.db, npairs, 2, -1, HD).transpose(0, 1, 3, 2, 4).reshape(geom.db, npairs, -1, LANES).astype(BF16)
    nctx = kc.shape[2]
    tab_spec = pl.BlockSpec((None, 2, NA_TAB_BLOCKS * GRID_W, LANES), lambda p, b: (p, 0, 0, 0))
    oa_s = pl.pallas_call(
        _na_kernel,
        out_shape=jax.ShapeDtypeStruct((geom.ns_rows, NA_W), BF16),
        grid=(npairs, geom.db),
        in_specs=[pl.BlockSpec((ds, LANES), lambda p, b: (s0 + b, p)),
                  pl.BlockSpec((ds, LANES), lambda p, b: (s0 + b, npairs + p)),
                  pl.BlockSpec((ds, LANES), lambda p, b: (s0 + b, 2 * npairs + p)),
                  pl.BlockSpec((None, None, nctx, LANES), lambda p, b: (b, p, 0, 0)),
                  pl.BlockSpec((None, None, nctx, LANES), lambda p, b: (b, p, 0, 0)),
                  tab_spec, _const_spec(row_mask.shape)],
        out_specs=pl.BlockSpec((ds, LANES), lambda p, b: (b, p)),
        compiler_params=_params(("parallel", "parallel")),
        name="na_sample_attn",
    )(proj, proj, proj, kc, vc, table, row_mask)
    return oa_p, oa_s


GLA_LEVELS = 6


def _gla_constants():
    c = GLA_CHUNK
    t = np.arange(c)
    i, j = t[:, None], t[None, :]
    out = {}
    for fwd in (True, False):
        tri = (j <= i) if fwd else (j >= i)
        masks = np.zeros((GLA_LEVELS + 1, c, c), np.float32)
        for lvl in range(GLA_LEVELS):
            half = c >> (lvl + 1)
            upper = (t % (2 * half)) >= half
            same = (i // (2 * half)) == (j // (2 * half))
            masks[lvl] = (same & upper[:, None] & ~upper[None, :]) if fwd else (same & ~upper[:, None] & upper[None, :])
        masks[GLA_LEVELS] = np.eye(c)
        out[fwd] = (jnp.asarray(tri.astype(np.float32), BF16),
                    jnp.asarray(np.concatenate([masks, masks], axis=1).reshape(-1, c), F32))
    return out


def _row_of_block(x, block, row):
    c = x.shape[0]
    if block >= SUBLANES:
        return jnp.concatenate(
            [jnp.broadcast_to(x[b * block + row:b * block + row + 1, :], (block, x.shape[1])) for b in range(c // block)],
            axis=0)
    per = SUBLANES // block
    sub = lax.broadcasted_iota(jnp.int32, x.shape, 0) % SUBLANES
    out = _row_of_block(x, SUBLANES, row)
    for p in range(1, per):
        out = jnp.where(sub >= p * block, _row_of_block(x, SUBLANES, p * block + row), out)
    return out


def _gla_cumulative(g, tri):
    g_hi, g_lo = _split(g)
    return _dot(tri, g_hi) + _dot(tri, g_lo)


def _gla_intra(q, k, g, b, masks, fwd):
    c = GLA_CHUNK
    t = lax.broadcasted_iota(jnp.int32, b.shape, 0)
    att = jnp.zeros((2 * c, c), F32)
    for lvl in range(GLA_LEVELS + 1):
        half = c >> (lvl + 1)
        if half >= 2:
            ref = _row_of_block(b, 2 * half, half - 1 if fwd else half)
            past_mid = ((t % (2 * half)) >= half) == fwd
            w = jnp.exp(jnp.where(past_mid, b - ref, ref - b))
            ql, kl = q * w, k * w
        elif half == 1:
            w = jnp.exp(jnp.where((t % 2 == 1) == fwd, g, 0.0))
            ql, kl = q * w, k * w
        else:
            ql, kl = q, k
        p = _dot_nt(_stack_heads(ql).astype(BF16), kl.astype(BF16))
        att = att + p * masks[lvl * 2 * c:(lvl + 1) * 2 * c]
    return att.astype(BF16)


def _gla_outputs(q, k, v, b, att, st, fwd):
    c = GLA_CHUNK
    edge = b[c - 1:c] if fwd else b[0:1]
    q_dec = q * jnp.exp(b)
    o_st = _dot_nt(_stack_heads(q_dec).astype(BF16), st.astype(BF16))
    r = _dot(att, v)
    o = jnp.concatenate([o_st[0:c] + r[0:c, 0:GLA_DV], o_st[c:] + r[c:, GLA_DV:]], axis=1)
    k_dec = (k * jnp.exp(edge - b)).astype(BF16)
    u = _dot_tn(v, k_dec)
    st_new = jnp.exp(edge) * st + jnp.where(_lane_lo(), u[0:GLA_DV], u[GLA_DV:])
    return o, st_new


GLA_GROUP = 8


def _gla_kernel(q_ref, k_ref, v_ref, rg_ref, gf_ref, gb_ref, s0f_ref, s0b_ref, gain_ref,
                tri_f_ref, masks_f_ref, tri_b_ref, masks_b_ref, y_ref, sf_ref, sb_ref,
                of_scr, ob_scr, st_scr):
    t = q_ref.shape[0]
    c = GLA_CHUNK
    nc = t // c
    group = min(GLA_GROUP, nc)
    assert nc % group == 0
    for d, s0_ref in enumerate((s0f_ref, s0b_ref)):
        st_scr[d] = jnp.concatenate([s0_ref[0], s0_ref[1]], axis=0).T
    per_dir = ((gf_ref, of_scr, tri_f_ref, masks_f_ref), (gb_ref, ob_scr, tri_b_ref, masks_b_ref))

    def body(gi, carry):
        streams = []
        for u in range(group):
            for d in range(2):
                ci = gi * group + u
                cc = ci if d == 0 else nc - 1 - ci
                streams.append((d, pl.ds(pl.multiple_of(cc * c, c), c)))
        qkg = [(q_ref[rows, :].astype(F32), k_ref[rows, :].astype(F32), per_dir[d][0][rows, :])
               for d, rows in streams]
        cums = [_gla_cumulative(g, per_dir[d][2][...]) for (d, _), (_, _, g) in zip(streams, qkg)]
        atts = [_gla_intra(q, k, g, b, per_dir[d][3][...], d == 0)
                for (d, _), (q, k, g), b in zip(streams, qkg, cums)]
        for (d, rows), (q, k, _), b, att in zip(streams, qkg, cums, atts):
            o, st_new = _gla_outputs(q, k, v_ref[rows, :], b, att, st_scr[d], d == 0)
            per_dir[d][1][rows, :] = o
            st_scr[d] = st_new
        return carry

    lax.fori_loop(0, nc // group, body, 0)

    for d, s_ref in enumerate((sf_ref, sb_ref)):
        s_pair = st_scr[d].T
        s_ref[0] = s_pair[0:GLA_DK]
        s_ref[1] = s_pair[GLA_DK:]

    rt = 256
    for r in range(t // rt):
        og = of_scr[r * rt:(r + 1) * rt, :] + ob_scr[r * rt:(r + 1) * rt, :]
        halves = []
        for hh in range(2):
            x = og[:, hh * GLA_DV:(hh + 1) * GLA_DV]
            halves.append(x * lax.rsqrt(jnp.mean(x * x, axis=-1, keepdims=True) + EPS))
        y = jnp.concatenate(halves, axis=1) * gain_ref[...] * _silu(rg_ref[r * rt:(r + 1) * rt, :].astype(F32))
        y_ref[r * rt:(r + 1) * rt, :] = y.astype(BF16)


def _gla_call(proj, gates, s0f, s0b, gain, consts, t, nb, row0, name):
    npairs = GLA_HEADS // 2
    qcol = 3 * NA_W // LANES
    kcol = qcol + GLA_KW // LANES
    vcol = (3 * NA_W + 2 * GLA_KW) // (2 * GLA_DV)
    rcol = vcol + GLA_VW // (2 * GLA_DV)
    tri_f, masks_f = consts[True]
    tri_b, masks_b = consts[False]
    st_spec = pl.BlockSpec((None, 2, GLA_DK, GLA_DV), lambda b, p: (b, p, 0, 0))
    return pl.pallas_call(
        _gla_kernel,
        out_shape=(jax.ShapeDtypeStruct((nb * t, GLA_VW), BF16),
                   jax.ShapeDtypeStruct((nb, GLA_HEADS, GLA_DK, GLA_DV), F32),
                   jax.ShapeDtypeStruct((nb, GLA_HEADS, GLA_DK, GLA_DV), F32)),
        grid=(nb, npairs),
        in_specs=[pl.BlockSpec((t, LANES), lambda b, p: (row0 + b, qcol + p)),
                  pl.BlockSpec((t, LANES), lambda b, p: (row0 + b, kcol + p)),
                  pl.BlockSpec((t, 2 * GLA_DV), lambda b, p: (row0 + b, vcol + p)),
                  pl.BlockSpec((t, 2 * GLA_DV), lambda b, p: (row0 + b, rcol + p)),
                  pl.BlockSpec((t, LANES), lambda b, p: (row0 + b, p)),
                  pl.BlockSpec((t, LANES), lambda b, p: (row0 + b, npairs + p)),
                  st_spec, st_spec,
                  pl.BlockSpec((1, 2 * GLA_DV), lambda b, p: (0, p)),
                  _const_spec(tri_f.shape), _const_spec(masks_f.shape),
                  _const_spec(tri_b.shape), _const_spec(masks_b.shape)],
        out_specs=(pl.BlockSpec((t, 2 * GLA_DV), lambda b, p: (b, p)), st_spec, st_spec),
        scratch_shapes=[pltpu.VMEM((t, 2 * GLA_DV), F32), pltpu.VMEM((t, 2 * GLA_DV), F32),
                        pltpu.VMEM((2, GLA_DV, LANES), F32)],
        compiler_params=_params(("parallel", "parallel")),
        name=name,
    )(proj, proj, proj, proj, gates, gates, s0f, s0b, gain, tri_f, masks_f, tri_b, masks_b)


C_OUT = C_QW + 4 * C_KW


def _c_in_kernel(x_ref, g_ref, sc_ref, sh_ref, w_ref, gain_ref, cos_ref, sin_ref, grp_ref, o_ref):
    h = _norm_mod(x_ref[...], g_ref[...], sc_ref[...], sh_ref[...]).astype(BF16)
    cw = 512
    nch = C_OUT // cw
    n_norm = (C_QW + 2 * C_KW) // LANES
    quarter = HD // 4
    first_half = lax.broadcasted_iota(jnp.int32, (1, LANES), 1) % (2 * quarter) < quarter
    y_ahead = _dot(h, w_ref[:, 0:cw])
    for j in range(nch):
        y = y_ahead
        if j + 1 < nch:
            y_ahead = _dot(h, w_ref[:, (j + 1) * cw:(j + 2) * cw])
        for s in range(cw // LANES):
            blk = j * (cw // LANES) + s
            yb = y[:, s * LANES:(s + 1) * LANES]
            if blk < n_norm:
                ms = _dot((yb * yb).astype(BF16), grp_ref[...])
                yn = yb * lax.rsqrt(ms + EPS) * gain_ref[:, blk * LANES:(blk + 1) * LANES]
                partner = jnp.where(first_half, pltpu.roll(yn, LANES - quarter, 1), pltpu.roll(yn, quarter, 1))
                yb = yn * cos_ref[...] + partner * sin_ref[...]
            o_ref[:, blk * LANES:(blk + 1) * LANES] = yb.astype(BF16)


def _rope_tables(geom):
    tpos = np.arange(geom.ds)
    inv = ROPE_THETA ** (-np.arange(0, HD // 2, 2, dtype=np.float32) / (HD // 2))
    ang_r = (tpos // GRID_W).astype(np.float32)[:, None] * inv
    ang_c = (tpos % GRID_W).astype(np.float32)[:, None] * inv
    ang_r, ang_c = jnp.asarray(ang_r, F32), jnp.asarray(ang_c, F32)
    cos = jnp.concatenate([jnp.cos(ang_r)] * 2 + [jnp.cos(ang_c)] * 2, axis=1)
    sin = jnp.concatenate([-jnp.sin(ang_r), jnp.sin(ang_r), -jnp.sin(ang_c), jnp.sin(ang_c)], axis=1)
    cos = jnp.concatenate([jnp.tile(cos, (1, 2)), jnp.ones((TP, LANES), F32)], axis=0)
    sin = jnp.concatenate([jnp.tile(sin, (1, 2)), jnp.zeros((TP, LANES), F32)], axis=0)
    lane = np.arange(LANES)
    grp = (lane[:, None] // HD == lane[None, :] // HD).astype(np.float32) / HD
    return cos, sin, jnp.asarray(grp, BF16)


def _c_in_proj(geom, x, mod_l, g, w, gain, tables):
    cos, sin, grp = tables
    npt, tps = geom.np_rows // TP, geom.ds // TP

    def pos_map(i):
        return (jnp.where(i < npt, tps, (i - npt) % tps), 0)
    return pl.pallas_call(
        _c_in_kernel,
        out_shape=jax.ShapeDtypeStruct((geom.n, C_OUT), BF16),
        grid=(geom.n // TP,),
        in_specs=[pl.BlockSpec((TP, D), lambda i: (i, 0)),
                  _const_spec((1, D)),
                  geom.mod_spec(1, TP), geom.mod_spec(0, TP),
                  _const_spec(w.shape), _const_spec(gain.shape),
                  pl.BlockSpec((TP, LANES), pos_map), pl.BlockSpec((TP, LANES), pos_map),
                  _const_spec(grp.shape)],
        out_specs=pl.BlockSpec((TP, C_OUT), lambda i: (i, 0)),
        compiler_params=_params(("parallel",)),
        name="gqa_in_proj",
    )(x, g, mod_l, mod_l, w, gain, cos, sin, grp)


def _gqa_attention(geom, proj, k_ctx, v_ctx):
    nqp = GQA_HEADS // 2
    kcol = C_QW // LANES
    vcol = kcol + 2 * C_KW // LANES
    s, ds = geom.s, geom.ds
    kvw = 2 * C_KW
    o_p = pl.pallas_call(
        functools.partial(_pair_attn_kernel, has_ctx=False, row_split=1, kv_share=2),
        out_shape=jax.ShapeDtypeStruct((geom.np_rows, C_QW), BF16),
        grid=(geom.b,),
        in_specs=[pl.BlockSpec((s, C_QW), lambda b: (b, 0)),
                  pl.BlockSpec((s, kvw), lambda b: (b, C_QW // kvw)),
                  pl.BlockSpec((s, kvw), lambda b: (b, C_QW // kvw + 1))],
        out_specs=pl.BlockSpec((s, C_QW), lambda b: (b, 0)),
        compiler_params=_params(("parallel",)),
        name="gqa_prompt_attn",
    )(proj, proj, proj)

    tq = ds
    nqt = ds // tq
    q0 = geom.np_rows // tq
    s0 = geom.np_rows // ds
    kc = jnp.concatenate([k_ctx, k_ctx], axis=-1).astype(BF16)
    vc = jnp.concatenate([v_ctx, v_ctx], axis=-1).astype(BF16)
    nctx = kc.shape[2]
    o_s = pl.pallas_call(
        functools.partial(_shared_kv_attn_kernel, has_ctx=True, row_split=tq // 256, kv_share=1),
        out_shape=jax.ShapeDtypeStruct((geom.ns_rows, C_QW), BF16),
        grid=(geom.db, nqp, nqt),
        in_specs=[pl.BlockSpec((tq, LANES), lambda b, m, t: (q0 + nqt * b + t, m)),
                  pl.BlockSpec((ds, LANES), lambda b, m, t: (s0 + b, kcol + m // 2)),
                  pl.BlockSpec((ds, LANES), lambda b, m, t: (s0 + b, vcol + m // 2)),
                  pl.BlockSpec((None, None, nctx, LANES), lambda b, m, t: (b, m // 2, 0, 0)),
                  pl.BlockSpec((None, None, nctx, LANES), lambda b, m, t: (b, m // 2, 0, 0))],
        out_specs=pl.BlockSpec((tq, LANES), lambda b, m, t: (nqt * b + t, m)),
        compiler_params=_params(("parallel", "parallel", "parallel")),
        name="gqa_sample_attn",
    )(proj, proj, proj, kc, vc)
    return o_p, o_s


def _ffn_kernel(*refs, npt, tps, seq, x_split, n_act):
    i = pl.program_id(0)
    j = jnp.maximum(i - npt, 0) % tps
    is_prompt = i < npt
    has_prev = jnp.logical_and(jnp.logical_not(is_prompt), j > 0).astype(F32)
    has_next = jnp.logical_and(jnp.logical_not(is_prompt), j < tps - 1).astype(F32)
    nx = 4 if x_split else 3
    x_refs = refs[:nx]
    act_refs = refs[nx:nx + 4 * n_act]
    (wout_ref, gate1_ref, g_ref, sc_ref, sh_ref, gate_ref, wup_ref, cw_ref, cb_ref, wdn_ref,
     o_ref, acc_ref, h_ref, xm_ref) = refs[nx + 4 * n_act:]
    x_main = _read_split(x_refs[0], x_refs[1], npt) if x_split else x_refs[0][...]
    mix = None
    k0 = 0
    for a in range(n_act):
        a_p, a_s, a_prev, a_next = act_refs[4 * a:4 * a + 4]
        act = jnp.concatenate([a_prev[...], _read_split(a_p, a_s, npt), a_next[...]], axis=0)
        part = _dot(act, wout_ref[k0:k0 + act.shape[1], :])
        mix = part if mix is None else mix + part
        k0 += act.shape[1]
    xm_ref[...] = jnp.concatenate([x_refs[-2][...], x_main, x_refs[-1][...]], axis=0) + gate1_ref[...] * mix
    g, sc, sh = g_ref[...], sc_ref[...], sh_ref[...]
    x = xm_ref[HALO:HALO + TF, :]
    h_ref[0:HALO, :] = (_norm_mod(xm_ref[0:HALO, :], g, sc, sh) * has_prev).astype(BF16)
    h_ref[HALO:HALO + TF, :] = _norm_mod(x, g, sc, sh).astype(BF16)
    h_ref[HALO + TF:, :] = (_norm_mod(xm_ref[HALO + TF:, :], g, sc, sh) * has_next).astype(BF16)
    rows = TF + 2 * HALO
    nch = D_FF // FF_CHUNK
    inner = list(range(seq, TF, seq))
    sub = lax.broadcasted_iota(jnp.int32, (SUBLANES, FF_CHUNK), 0)
    no_prev = jnp.where(jnp.logical_and(is_prompt, sub == 0), 0.0, 1.0)
    no_next = jnp.where(jnp.logical_and(is_prompt, sub == SUBLANES - 1), 0.0, 1.0)

    def up(c):
        return [_dot(h_ref[...], wup_ref[:, off + c * FF_CHUNK:off + (c + 1) * FF_CHUNK]) for off in (0, D_FF)]

    def conv(u, off, c):
        cols = slice(off + c * FF_CHUNK, off + (c + 1) * FF_CHUNK)
        u_prev = pltpu.roll(u, 1, 0)[HALO:HALO + TF]
        u_next = pltpu.roll(u, rows - 1, 0)[HALO:HALO + TF]
        for b in inner:
            u_prev = jnp.concatenate([u_prev[:b], u_prev[b:b + SUBLANES] * no_prev, u_prev[b + SUBLANES:]], axis=0)
            u_next = jnp.concatenate([u_next[:b - SUBLANES], u_next[b - SUBLANES:b] * no_next, u_next[b:]], axis=0)
        return (cw_ref[0:1, cols] * u_prev + cw_ref[1:2, cols] * u[HALO:HALO + TF]
                + cw_ref[2:3, cols] * u_next + cb_ref[:, cols])

    def down(c, a):
        part = _dot(a, wdn_ref[c * FF_CHUNK:(c + 1) * FF_CHUNK, :])
        if c == 0:
            acc_ref[...] = part
        else:
            acc_ref[...] += part

    u_ahead = up(0)
    a_prev = None
    for c in range(nch):
        u_val, u_gate = u_ahead
        if c + 1 < nch:
            u_ahead = up(c + 1)
        if a_prev is not None:
            down(c - 1, a_prev)
        a_prev = (_silu(conv(u_gate, D_FF, c)) * conv(u_val, 0, c)).astype(BF16)
    down(nch - 1, a_prev)
    o_ref[...] = x + gate_ref[...] * acc_ref[...]


def _mixer_out_ffn(geom, xs, act_pairs, w_out, mod_l, g, wup, cw, cb, wdn):
    assert TF % geom.s == 0 and geom.np_rows % TF == 0 and geom.ds % TF == 0
    per = TF // HALO
    npt = geom.np_rows // TF

    def halo_specs(width, nblk, first_tile):
        def prev(i):
            return (jnp.maximum((jnp.maximum(i - first_tile, 0)) * per - 1, 0), 0)

        def nxt(i):
            return (jnp.minimum((jnp.maximum(i - first_tile, 0) + 1) * per, nblk - 1), 0)
        return [pl.BlockSpec((HALO, width), prev), pl.BlockSpec((HALO, width), nxt)]

    x_split = len(xs) == 2
    if x_split:
        in_specs = geom.split_specs(TF, D) + halo_specs(D, geom.ns_rows // HALO, npt)
        args = [xs[0], xs[1], xs[1], xs[1]]
    else:
        in_specs = [pl.BlockSpec((TF, D), lambda i: (i, 0))] + halo_specs(D, geom.n // HALO, 0)
        args = [xs[0], xs[0], xs[0]]
    for a_p, a_s in act_pairs:
        in_specs += geom.split_specs(TF, a_p.shape[1]) + halo_specs(a_p.shape[1], geom.ns_rows // HALO, npt)
        args += [a_p, a_s, a_s, a_s]
    in_specs += [_const_spec(w_out.shape), geom.mod_spec(2, TF), _const_spec((1, D)),
                 geom.mod_spec(4, TF), geom.mod_spec(3, TF), geom.mod_spec(5, TF),
                 _const_spec(wup.shape), _const_spec(cw.shape), _const_spec(cb.shape), _const_spec(wdn.shape)]
    args += [w_out, mod_l, g, mod_l, mod_l, mod_l, wup, cw, cb, wdn]
    return pl.pallas_call(
        functools.partial(_ffn_kernel, npt=npt, tps=geom.ds // TF, seq=geom.s, x_split=x_split,
                          n_act=len(act_pairs)),
        out_shape=jax.ShapeDtypeStruct((geom.n, D), F32),
        grid=(geom.n // TF,),
        in_specs=in_specs,
        out_specs=pl.BlockSpec((TF, D), lambda i: (i, 0)),
        scratch_shapes=[pltpu.VMEM((TF, D), F32), pltpu.VMEM((TF + 2 * HALO, D), BF16),
                        pltpu.VMEM((TF + 2 * HALO, D), F32)],
        compiler_params=_params(("parallel",)),
        name="mixer_out_ffn",
    )(*args)


def _final_norm_kernel(x_ref, g_ref, o_ref):
    x = x_ref[...]
    o_ref[...] = x * lax.rsqrt(jnp.mean(x * x, axis=-1, keepdims=True) + EPS) * g_ref[...]


def _final_norm(x, g, row0, nrows):
    tn = 1024
    assert row0 % tn == 0 and nrows % tn == 0
    t0 = row0 // tn
    return pl.pallas_call(
        _final_norm_kernel,
        out_shape=jax.ShapeDtypeStruct((nrows, D), F32),
        grid=(nrows // tn,),
        in_specs=[pl.BlockSpec((tn, D), lambda i: (t0 + i, 0)), _const_spec((1, D))],
        out_specs=pl.BlockSpec((tn, D), lambda i: (i, 0)),
        compiler_params=_params(("parallel",)),
        name="final_norm",
    )(x, g)


def _prep_ab(w_in, w2_f, b2_f, w2_b, b2_b):
    scale = np.ones((w_in.shape[1],), np.float32)
    scale[0:NA_W] = HD ** -0.5 * LOG2E
    scale[3 * NA_W:3 * NA_W + GLA_KW] = GLA_DK ** -0.5
    w = jnp.pad(w_in * scale, ((0, 0), (0, AB_MAIN + LANES - w_in.shape[1]))).astype(BF16)
    w2 = jnp.zeros((LANES, 2 * GLA_KW), F32)
    w2 = w2.at[0:GLA_RANK, 0:GLA_KW].set(w2_f).at[GLA_RANK:2 * GLA_RANK, GLA_KW:].set(w2_b)
    w2_hi, w2_lo = _split(w2)
    b2 = jnp.concatenate([b2_f, b2_b])[None, :]
    return w, w2_hi, w2_lo, b2


def _prep_c(w_in, qn_g, kn_g):
    wq, wk, wv = w_in[:, :C_QW], w_in[:, C_QW:C_QW + C_KW], w_in[:, C_QW + C_KW:]

    def dup(w):
        return jnp.broadcast_to(w.reshape(D, GQA_KVH, 1, HD), (D, GQA_KVH, 2, HD)).reshape(D, 2 * C_KW)
    w = jnp.concatenate([wq, dup(wk), dup(wv)], axis=1).astype(BF16)
    gain = jnp.concatenate([jnp.tile(qn_g * (HD ** -0.5 * LOG2E), GQA_HEADS), jnp.tile(kn_g, 2 * GQA_KVH)])[None, :]
    return w, gain


def kernel(x_prompt, x_sample, cache_na_k, cache_na_v, state_gla_fwd, state_gla_bwd, cache_gqa_k, cache_gqa_v, c, c_ctx, ada_w, ada_b, norm_mix_g, norm_ffn_g, ab_w_in, ab_w_out, na_rpb, gla_w2_fwd, gla_b2_fwd, gla_w2_bwd, gla_b2_bwd, gla_norm_g, gqa_w_in, gqa_w_out, gqa_q_norm_g, gqa_k_norm_g, ffn_w_up, ffn_conv_w, ffn_conv_b, ffn_w_down, final_norm_g):
    b, s, _ = x_prompt.shape
    db, ds, _ = x_sample.shape
    geom = _Geom(b, s, db, ds)
    assert ds % GRID_W == 0 and geom.np_rows % ds == 0 and geom.np_rows % TP == 0

    xs = [x_prompt.reshape(-1, D), x_sample.reshape(-1, D)]
    nrow = -(-(db + 1) // SUBLANES) * SUBLANES
    c_rows = jnp.zeros((nrow, D), F32).at[:db].set(c).at[db].set(c_ctx)
    mod = _modulation(c_rows, ada_w, ada_b).reshape(DEPTH, nrow, 1, 6 * D)

    gla_consts = _gla_constants()
    rope = _rope_tables(geom)
    zeros_state = jnp.zeros((b, GLA_HEADS, GLA_DK, GLA_DV), F32)
    na_k, na_v, gla_f, gla_b, gq_k, gq_v = [], [], [], [], [], []

    for i in range(DEPTH):
        j = i // 2
        mod_l = mod[i]
        if i % 2 == 0:
            w, w2_hi, w2_lo, b2 = _prep_ab(ab_w_in[j], gla_w2_fwd[j], gla_b2_fwd[j], gla_w2_bwd[j], gla_b2_bwd[j])
            proj, gates = _ab_in_proj(geom, xs, mod_l, norm_mix_g[i][None, :], w, w2_hi, w2_lo, b2)
            oa = _na_attention(geom, proj, cache_na_k[:, j], cache_na_v[:, j], na_rpb[j])
            gain = gla_norm_g[j][None, :]
            yg_p, sf, sb = _gla_call(proj, gates, zeros_state, zeros_state, gain, gla_consts, s, b, 0, "gla_prompt")
            yg_s, _, _ = _gla_call(proj, gates, state_gla_fwd[:, j], state_gla_bwd[:, j], gain, gla_consts,
                                   ds, db, geom.np_rows // ds, "gla_sample")
            acts, w_out = [oa, (yg_p, yg_s)], ab_w_out[j]
            pk = proj[:geom.np_rows].reshape(b, s, -1)
            na_k.append(pk[..., NA_W:2 * NA_W].reshape(b, s, NA_HEADS, HD).transpose(0, 2, 1, 3).astype(F32))
            na_v.append(pk[..., 2 * NA_W:3 * NA_W].reshape(b, s, NA_HEADS, HD).transpose(0, 2, 1, 3).astype(F32))
            gla_f.append(sf)
            gla_b.append(sb)
        else:
            w, gain = _prep_c(gqa_w_in[j], gqa_q_norm_g[j], gqa_k_norm_g[j])
            proj = _c_in_proj(geom, xs[0], mod_l, norm_mix_g[i][None, :], w, gain, rope)
            o = _gqa_attention(geom, proj, cache_gqa_k[:, j], cache_gqa_v[:, j])
            acts, w_out = [o], gqa_w_out[j]
            pk = proj[:geom.np_rows].reshape(b, s, -1)
            kd = pk[..., C_QW:C_QW + 2 * C_KW].reshape(b, s, GQA_KVH, 2, HD)[:, :, :, 0]
            vd = pk[..., C_QW + 2 * C_KW:].reshape(b, s, GQA_KVH, 2, HD)[:, :, :, 0]
            gq_k.append(kd.transpose(0, 2, 1, 3).astype(F32))
            gq_v.append(vd.transpose(0, 2, 1, 3).astype(F32))
        x = _mixer_out_ffn(geom, xs, acts, w_out.astype(BF16), mod_l, norm_ffn_g[i][None, :],
                           ffn_w_up[i].astype(BF16), ffn_conv_w[i], ffn_conv_b[i][None, :], ffn_w_down[i].astype(BF16))
        xs = [x]

    y_prompt = _final_norm(x, final_norm_g[None, :], 0, geom.np_rows).reshape(b, s, D)
    y_sample = _final_norm(x, final_norm_g[None, :], geom.np_rows, geom.ns_rows).reshape(db, ds, D)
    return (y_prompt, y_sample,
            jnp.stack(na_k, axis=1), jnp.stack(na_v, axis=1),
            jnp.stack(gla_f, axis=1), jnp.stack(gla_b, axis=1),
            jnp.stack(gq_k, axis=1), jnp.stack(gq_v, axis=1))
```

```python
import functools

import numpy as np
import jax
import jax.numpy as jnp
from jax import lax
from jax.experimental import pallas as pl
from jax.experimental.pallas import tpu as pltpu

F32 = jnp.float32
BF16 = jnp.bfloat16

D = 1024
DEPTH = 4
HD = 64
GRID_W = 64
EPS = 1e-6
NA_HEADS = 8
NA_W = NA_HEADS * HD
NA_KR = 8
NA_KC = 16
GLA_HEADS = 4
GLA_DK = 64
GLA_DV = 128
GLA_KW = GLA_HEADS * GLA_DK
GLA_VW = GLA_HEADS * GLA_DV
GLA_RANK = 16
GLA_CHUNK = 64
GQA_HEADS = 16
GQA_KVH = 4
C_QW = GQA_HEADS * HD
C_KW = GQA_KVH * HD
ROPE_THETA = 10000.0
D_FF = 2816
AB_MAIN = 3 * NA_W + 2 * GLA_KW + 2 * GLA_VW

LANES = 128
SUBLANES = 8
TM = 256
TP = 1024
TF = 256
LOG2E = 1.4426950408889634
HALO = 16
FF_CHUNK = 256
NA_QROWS = 8
NA_KROWS = NA_QROWS + NA_KR - 1
VMEM_LIMIT = 56 * 1024 * 1024


def _dot(a, b):
    return jnp.dot(a, b, preferred_element_type=F32)


def _dot_nt(a, b):
    return lax.dot_general(a, b, (((1,), (1,)), ((), ())), preferred_element_type=F32)


def _dot_tn(a, b):
    return lax.dot_general(a, b, (((0,), (0,)), ((), ())), preferred_element_type=F32)


def _split(a):
    hi = a.astype(BF16)
    lo = (a - hi.astype(F32)).astype(BF16)
    return hi, lo


def _dot_split(a, w_hi, w_lo):
    a_hi, a_lo = _split(a)
    return _dot(a_hi, w_hi) + _dot(a_lo, w_hi) + _dot(a_hi, w_lo)


def _norm_mod(x, g, sc, sh):
    xn = x * lax.rsqrt(jnp.mean(x * x, axis=-1, keepdims=True) + EPS)
    return (xn * g) * (1.0 + sc) + sh


def _silu(x):
    return x / (1.0 + jnp.exp(-x))


def _lane_lo(shape=(1, LANES)):
    return lax.broadcasted_iota(jnp.int32, shape, len(shape) - 1) < HD


def _stack_heads(x):
    lo = _lane_lo()
    zero = jnp.zeros_like(x)
    return jnp.concatenate([jnp.where(lo, x, zero), jnp.where(lo, zero, x)], axis=0)


def _params(sem):
    return pltpu.CompilerParams(dimension_semantics=sem, vmem_limit_bytes=VMEM_LIMIT)


def _mod_kernel(c_ref, w_ref, b_ref, o_ref):
    a = _silu(c_ref[...])
    w = w_ref[...]
    w_hi, w_lo = _split(w)
    o_ref[...] = _dot_split(a, w_hi, w_lo) + b_ref[...]


def _modulation(c_rows, ada_w, ada_b):
    nrow = c_rows.shape[0]
    tn = 1024
    return pl.pallas_call(
        _mod_kernel,
        out_shape=jax.ShapeDtypeStruct((DEPTH, nrow, 6 * D), F32),
        grid=(DEPTH, 6 * D // tn),
        in_specs=[pl.BlockSpec((nrow, D), lambda l, n: (0, 0)),
                  pl.BlockSpec((None, D, tn), lambda l, n: (l, 0, n)),
                  pl.BlockSpec((None, 1, tn), lambda l, n: (l, 0, n))],
        out_specs=pl.BlockSpec((None, nrow, tn), lambda l, n: (l, 0, n)),
        compiler_params=_params(("parallel", "parallel")),
        name="modulation",
    )(c_rows, ada_w, ada_b.reshape(DEPTH, 1, 6 * D))


class _Geom:
    def __init__(self, b, s, db, ds):
        assert s == TM and ds % TM == 0
        self.b, self.s, self.db, self.ds = b, s, db, ds
        self.np_rows = b * s
        self.ns_rows = db * ds
        self.n = self.np_rows + self.ns_rows
        self.npt = self.np_rows // TM
        self.tps = ds // TM
        self.ctx_row = db

    def mod_spec(self, k, tm=TM):
        assert self.np_rows % tm == 0 and self.ds % tm == 0
        npt, tps, ctx = self.np_rows // tm, self.ds // tm, self.ctx_row

        def imap(i):
            return (jnp.where(i < npt, ctx, (i - npt) // tps), 0, k)
        return pl.BlockSpec((None, 1, D), imap)

    def split_specs(self, tm, width):
        npt = self.np_rows // tm
        return [pl.BlockSpec((tm, width), lambda i: (jnp.minimum(i, npt - 1), 0)),
                pl.BlockSpec((tm, width), lambda i: (jnp.maximum(i - npt, 0), 0))]


def _read_split(p_ref, s_ref, npt):
    return jnp.where(pl.program_id(0) < npt, p_ref[...], s_ref[...])


def _const_spec(shape):
    nd = len(shape)
    return pl.BlockSpec(shape, lambda *_: (0,) * nd)


def _ab_in_kernel(*refs, split_npt):
    nx = 2 if split_npt else 1
    g_ref, sc_ref, sh_ref, w_ref, w2hi_ref, w2lo_ref, b2_ref, o_ref, gate_ref = refs[nx:]
    x = _read_split(refs[0], refs[1], split_npt) if split_npt else refs[0][...]
    h = _norm_mod(x, g_ref[...], sc_ref[...], sh_ref[...]).astype(BF16)
    cw = 512
    lr = _dot(h, w_ref[:, AB_MAIN:AB_MAIN + LANES])
    o_ref[:, 0:cw] = _dot(h, w_ref[:, 0:cw]).astype(BF16)
    z = _dot_split(lr, w2hi_ref[...], w2lo_ref[...]) + b2_ref[...]
    for j in range(1, AB_MAIN // cw):
        o_ref[:, j * cw:(j + 1) * cw] = _dot(h, w_ref[:, j * cw:(j + 1) * cw]).astype(BF16)
    gate_ref[...] = (jnp.minimum(z, 0.0) - jnp.log(1.0 + jnp.exp(-jnp.abs(z)))) * (1.0 / 16.0)


def _ab_in_proj(geom, xs, mod_l, g, w, w2hi, w2lo, b2):
    split = len(xs) == 2
    x_specs = geom.split_specs(TP, D) if split else [pl.BlockSpec((TP, D), lambda i: (i, 0))]
    return pl.pallas_call(
        functools.partial(_ab_in_kernel, split_npt=geom.np_rows // TP if split else 0),
        out_shape=(jax.ShapeDtypeStruct((geom.n, AB_MAIN), BF16),
                   jax.ShapeDtypeStruct((geom.n, 2 * GLA_KW), F32)),
        grid=(geom.n // TP,),
        in_specs=x_specs + [_const_spec((1, D)),
                            geom.mod_spec(1, TP), geom.mod_spec(0, TP),
                            _const_spec(w.shape), _const_spec(w2hi.shape), _const_spec(w2lo.shape),
                            _const_spec(b2.shape)],
        out_specs=(pl.BlockSpec((TP, AB_MAIN), lambda i: (i, 0)),
                   pl.BlockSpec((TP, 2 * GLA_KW), lambda i: (i, 0))),
        compiler_params=_params(("parallel",)),
        name="ab_in_proj",
    )(*xs, g, mod_l, mod_l, w, w2hi, w2lo, b2)


def _softmax_units(n_units, scores, values, emit):
    def finish(u, p_parts, l):
        emit(u, sum(_dot(p, v) for p, v in zip(p_parts, values(u))) / l)

    s_ahead = scores(0)
    pending = None
    for u in range(n_units):
        s_parts = s_ahead
        if u + 1 < n_units:
            s_ahead = scores(u + 1)
        if pending is not None:
            finish(u - 1, *pending)
        m = functools.reduce(jnp.maximum, [jnp.max(s, axis=-1, keepdims=True) for s in s_parts])
        p_parts = [jnp.exp2(s - m) for s in s_parts]
        l = sum(jnp.sum(p, axis=-1, keepdims=True) for p in p_parts)
        pending = ([p.astype(BF16) for p in p_parts], l)
    finish(n_units - 1, *pending)


def _head_of_pair(x, hh):
    lo = _lane_lo()
    zero = jnp.zeros_like(x)
    return jnp.where(lo, x, zero) if hh == 0 else jnp.where(lo, zero, x)


def _pair_attn_kernel(*refs, has_ctx, row_split, kv_share):
    if has_ctx:
        q_ref, k_ref, v_ref, kc_ref, vc_ref, o_ref = refs
    else:
        q_ref, k_ref, v_ref, o_ref = refs
    npairs = q_ref.shape[1] // LANES
    rows = q_ref.shape[0] // row_split
    units = [(m, r, hh) for m in range(npairs) for r in range(row_split) for hh in range(2)]
    outs = {}

    def kv_cols(m):
        c0 = (m // kv_share) * LANES
        return slice(c0, c0 + LANES)

    def scores(u):
        m, r, hh = units[u]
        qb = _head_of_pair(q_ref[r * rows:(r + 1) * rows, m * LANES:(m + 1) * LANES], hh)
        parts = [_dot_nt(qb, k_ref[:, kv_cols(m)])]
        if has_ctx:
            parts.append(_dot_nt(qb, kc_ref[...]))
        return parts

    def values(u):
        m = units[u][0]
        return [v_ref[:, kv_cols(m)]] + ([vc_ref[...]] if has_ctx else [])

    def emit(u, out):
        outs[units[u]] = out

    _softmax_units(len(units), scores, values, emit)
    lo = _lane_lo()
    for m in range(npairs):
        for r in range(row_split):
            o_ref[r * rows:(r + 1) * rows, m * LANES:(m + 1) * LANES] = jnp.where(
                lo, outs[(m, r, 0)], outs[(m, r, 1)]).astype(BF16)


def _shared_kv_attn_kernel(*refs, has_ctx, row_split, kv_share):
    if has_ctx:
        q_ref, k_ref, v_ref, kc_ref, vc_ref, o_ref = refs
    else:
        q_ref, k_ref, v_ref, o_ref = refs
    npairs = q_ref.shape[1] // LANES
    rows = q_ref.shape[0] // row_split
    units = [(m, r, hh) for m in range(npairs) for r in range(row_split) for hh in range(2)]
    outs = {}

    def kv_cols(m):
        c0 = (m // kv_share) * LANES
        return slice(c0, c0 + LANES)

    def values_t(ref, cols):
        vt = ref[:, cols].astype(F32).T
        return jnp.where(lax.broadcasted_iota(jnp.int32, vt.shape, 0) < HD, vt, 1.0).astype(BF16)

    vt_cache = {}

    def values(u):
        m = units[u][0]
        key = m // kv_share
        if key not in vt_cache:
            vt_cache[key] = [values_t(v_ref, kv_cols(m))] + ([values_t(vc_ref, slice(None))] if has_ctx else [])
        return vt_cache[key]

    def scores(u):
        m, r, hh = units[u]
        qb = _head_of_pair(q_ref[r * rows:(r + 1) * rows, m * LANES:(m + 1) * LANES], hh)
        parts = [_dot_nt(k_ref[:, kv_cols(m)], qb)]
        if has_ctx:
            parts.append(_dot_nt(kc_ref[...], qb))
        return parts

    def finish(u, p_parts):
        ext = sum(_dot(vt, p) for vt, p in zip(values(u), p_parts))
        outs[units[u]] = ext[0:HD] / ext[HD:HD + 1]

    s_ahead = scores(0)
    pending = None
    for u in range(len(units)):
        s_parts = s_ahead
        if u + 1 < len(units):
            s_ahead = scores(u + 1)
        if pending is not None:
            finish(u - 1, pending)
        m_col = functools.reduce(jnp.maximum, [jnp.max(s, axis=0, keepdims=True) for s in s_parts])
        pending = [jnp.exp2(s - m_col).astype(BF16) for s in s_parts]
    finish(len(units) - 1, pending)
    for m in range(npairs):
        for r in range(row_split):
            both = jnp.concatenate([outs[(m, r, 0)], outs[(m, r, 1)]], axis=0)
            o_ref[r * rows:(r + 1) * rows, m * LANES:(m + 1) * LANES] = both.T.astype(BF16)


def _na_block_rows(rb, rows):
    r0 = rb * NA_QROWS
    return r0, int(np.clip(r0 - NA_KR // 2, 0, rows - NA_KROWS))


def _na_kernel(q_ref, k_ref, v_ref, kc_ref, vc_ref, tab_ref, rowmask_ref, o_ref):
    rows = q_ref.shape[0] // GRID_W
    nrb = rows // NA_QROWS
    nq = NA_QROWS * GRID_W
    nk = NA_KROWS * GRID_W
    units = [(rb, hh) for rb in range(nrb) for hh in range(2)]
    outs = {}

    def probabilities(u, s_loc, s_ctx):
        rb, hh = units[u]
        r0, kstart = _na_block_rows(rb, rows)
        loc_cols, ctx_cols = [], []
        for jp in range(NA_QROWS // 2):
            starts = [int(np.clip(r0 + 2 * jp + e - NA_KR // 2, 0, rows - NA_KR)) - kstart for e in (0, 1)]
            lo, hi = min(starts), max(starts) + NA_KR
            first = kstart - (r0 + 2 * jp) + 2 * (NA_KR - 1)
            lanes = slice(jp * LANES, (jp + 1) * LANES)
            s = (s_loc[lo * GRID_W:hi * GRID_W, lanes]
                 + tab_ref[hh, (first + lo) * GRID_W:(first + hi) * GRID_W, :]
                 + rowmask_ref[rb, jp, lo * GRID_W:hi * GRID_W, :])
            sc = s_ctx[:, lanes]
            m_col = jnp.maximum(jnp.max(s, axis=0, keepdims=True), jnp.max(sc, axis=0, keepdims=True))
            pieces = [jnp.exp2(s - m_col).astype(BF16)]
            if lo:
                pieces.insert(0, jnp.zeros((lo * GRID_W, LANES), BF16))
            if hi < NA_KROWS:
                pieces.append(jnp.zeros(((NA_KROWS - hi) * GRID_W, LANES), BF16))
            loc_cols.append(jnp.concatenate(pieces, axis=0))
            ctx_cols.append(jnp.exp2(sc - m_col).astype(BF16))
        return [jnp.concatenate(loc_cols, axis=1), jnp.concatenate(ctx_cols, axis=1)]

    def keys_at(rb):
        k0 = _na_block_rows(rb, rows)[1] * GRID_W
        return slice(k0, k0 + nk)

    def scores(u):
        rb, hh = units[u]
        qb = _head_of_pair(q_ref[rb * nq:(rb + 1) * nq, :], hh)
        return [_dot_nt(k_ref[keys_at(rb), :], qb), _dot_nt(kc_ref[...], qb)]

    vt_cache = {}

    def with_ones(v):
        vt = v.astype(F32).T.astype(BF16)
        return jnp.concatenate([vt, jnp.ones((HALO, vt.shape[1]), BF16)], axis=0)

    def values_t(rb):
        if "ctx" not in vt_cache:
            vt_cache["ctx"] = with_ones(vc_ref[...])
        if rb not in vt_cache:
            vt_cache[rb] = with_ones(v_ref[keys_at(rb), :])
        return [vt_cache[rb], vt_cache["ctx"]]

    def finish(u, p_parts):
        rb, hh = units[u]
        ext = sum(_dot(vt, p) for vt, p in zip(values_t(rb), p_parts))
        outs[units[u]] = ext[hh * HD:(hh + 1) * HD] / ext[LANES:LANES + 1]

    s_ahead = scores(0)
    pending = None
    for u in range(len(units)):
        s_parts = s_ahead
        if u + 1 < len(units):
            s_ahead = scores(u + 1)
        if pending is not None:
            finish(u - 1, pending)
        pending = probabilities(u, *s_parts)
    finish(len(units) - 1, pending)
    for rb in range(nrb):
        both = jnp.concatenate([outs[(rb, 0)], outs[(rb, 1)]], axis=0)
        o_ref[rb * nq:(rb + 1) * nq, :] = both.T.astype(BF16)


NA_TAB_BLOCKS = 30


def _na_bias_tables(rpb, rows):
    ndr = 2 * NA_KR - 1
    col = np.arange(GRID_W)
    cs = np.clip(col - NA_KC // 2, 0, GRID_W - NA_KC)
    col_ok = (col[None, :] >= cs[:, None]) & (col[None, :] < cs[:, None] + NA_KC)
    dc = np.clip(col[None, :] - col[:, None] + NA_KC - 1, 0, 2 * NA_KC - 2)
    pick_dc = (np.arange(2 * NA_KC - 1)[:, None, None] == dc[None]).astype(np.float32)
    by_col = jnp.einsum('hdj,jqc->hdcq', rpb.astype(F32) * LOG2E, pick_dc, precision=lax.Precision.HIGHEST)
    by_col = jnp.where(col_ok.T[None, None], by_col, -jnp.inf)
    flat = by_col.reshape(NA_HEADS // 2, 2, ndr * GRID_W, GRID_W)
    lead = NA_KR - 1
    tail = NA_TAB_BLOCKS - ndr - lead

    def shifted(shift):
        return jnp.pad(flat, ((0, 0), (0, 0), ((lead + shift) * GRID_W, (tail - shift) * GRID_W), (0, 0)))
    table = jnp.concatenate([shifted(0), shifted(1)], axis=-1)
    nrb = rows // NA_QROWS
    row_mask = np.full((nrb, NA_QROWS // 2, NA_KROWS, GRID_W, 2, GRID_W), -np.inf, np.float32)
    for rb in range(nrb):
        r0, kstart = _na_block_rows(rb, rows)
        for qr in range(NA_QROWS):
            rs = int(np.clip(r0 + qr - NA_KR // 2, 0, rows - NA_KR))
            for kk in range(NA_KROWS):
                if rs <= kstart + kk < rs + NA_KR:
                    row_mask[rb, qr // 2, kk, :, qr % 2, :] = 0.0
    return table, jnp.asarray(row_mask.reshape(nrb, NA_QROWS // 2, NA_KROWS * GRID_W, 2 * GRID_W))


def _na_attention(geom, proj, k_ctx, v_ctx, rpb):
    npairs = NA_HEADS // 2
    s, ds = geom.s, geom.ds
    oa_p = pl.pallas_call(
        functools.partial(_pair_attn_kernel, has_ctx=False, row_split=1, kv_share=1),
        out_shape=jax.ShapeDtypeStruct((geom.np_rows, NA_W), BF16),
        grid=(geom.b,),
        in_specs=[pl.BlockSpec((s, NA_W), lambda b: (b, 0)),
                  pl.BlockSpec((s, NA_W), lambda b: (b, 1)),
                  pl.BlockSpec((s, NA_W), lambda b: (b, 2))],
        out_specs=pl.BlockSpec((s, NA_W), lambda b: (b, 0)),
        compiler_params=_params(("parallel",)),
        name="na_prompt_attn",
    )(proj, proj, proj)

    rows = ds // GRID_W
    assert rows % NA_QROWS == 0 and rows >= NA_KROWS
    table, row_mask = _na_bias_tables(rpb, rows)
    s0 = geom.np_rows // ds
    kc = k_ctx.reshape(geom.db, npairs, 2, -1, HD).transpose(0, 1, 3, 2, 4).reshape(geom.db, npairs, -1, LANES).astype(BF16)
    vc = v_ctx.reshape(geom.db, npairs, 2, -1, HD).transpose(0, 1, 3, 2, 4).reshape(geom.db, npairs, -1, LANES).astype(BF16)
    nctx = kc.shape[2]
    tab_spec = pl.BlockSpec((None, 2, NA_TAB_BLOCKS * GRID_W, LANES), lambda p, b: (p, 0, 0, 0))
    oa_s = pl.pallas_call(
        _na_kernel,
        out_shape=jax.ShapeDtypeStruct((geom.ns_rows, NA_W), BF16),
        grid=(npairs, geom.db),
        in_specs=[pl.BlockSpec((ds, LANES), lambda p, b: (s0 + b, p)),
                  pl.BlockSpec((ds, LANES), lambda p, b: (s0 + b, npairs + p)),
                  pl.BlockSpec((ds, LANES), lambda p, b: (s0 + b, 2 * npairs + p)),
                  pl.BlockSpec((None, None, nctx, LANES), lambda p, b: (b, p, 0, 0)),
                  pl.BlockSpec((None, None, nctx, LANES), lambda p, b: (b, p, 0, 0)),
                  tab_spec, _const_spec(row_mask.shape)],
        out_specs=pl.BlockSpec((ds, LANES), lambda p, b: (b, p)),
        compiler_params=_params(("parallel", "parallel")),
        name="na_sample_attn",
    )(proj, proj, proj, kc, vc, table, row_mask)
    return oa_p, oa_s


GLA_LEVELS = 6


def _gla_constants():
    c = GLA_CHUNK
    t = np.arange(c)
    i, j = t[:, None], t[None, :]
    out = {}
    for fwd in (True, False):
        tri = (j <= i) if fwd else (j >= i)
        masks = np.zeros((GLA_LEVELS + 1, c, c), np.float32)
        for lvl in range(GLA_LEVELS):
            half = c >> (lvl + 1)
            upper = (t % (2 * half)) >= half
            same = (i // (2 * half)) == (j // (2 * half))
            masks[lvl] = (same & upper[:, None] & ~upper[None, :]) if fwd else (same & ~upper[:, None] & upper[None, :])
        masks[GLA_LEVELS] = np.eye(c)
        out[fwd] = (jnp.asarray(tri.astype(np.float32), BF16),
                    jnp.asarray(np.concatenate([masks, masks], axis=1).reshape(-1, c), F32))
    return out


def _row_of_block(x, block, row):
    c = x.shape[0]
    if block >= SUBLANES:
        return jnp.concatenate(
            [jnp.broadcast_to(x[b * block + row:b * block + row + 1, :], (block, x.shape[1])) for b in range(c // block)],
            axis=0)
    per = SUBLANES // block
    sub = lax.broadcasted_iota(jnp.int32, x.shape, 0) % SUBLANES
    out = _row_of_block(x, SUBLANES, row)
    for p in range(1, per):
        out = jnp.where(sub >= p * block, _row_of_block(x, SUBLANES, p * block + row), out)
    return out


def _gla_cumulative(g, tri):
    g_hi, g_lo = _split(g)
    return _dot(tri, g_hi) + _dot(tri, g_lo)


def _gla_intra(q, k, g, b, masks, fwd):
    c = GLA_CHUNK
    t = lax.broadcasted_iota(jnp.int32, b.shape, 0)
    att = jnp.zeros((2 * c, c), F32)
    for lvl in range(GLA_LEVELS + 1):
        half = c >> (lvl + 1)
        if half >= 2:
            ref = _row_of_block(b, 2 * half, half - 1 if fwd else half)
            past_mid = ((t % (2 * half)) >= half) == fwd
            w = jnp.exp(jnp.where(past_mid, b - ref, ref - b))
            ql, kl = q * w, k * w
        elif half == 1:
            w = jnp.exp(jnp.where((t % 2 == 1) == fwd, g, 0.0))
            ql, kl = q * w, k * w
        else:
            ql, kl = q, k
        p = _dot_nt(_stack_heads(ql).astype(BF16), kl.astype(BF16))
        att = att + p * masks[lvl * 2 * c:(lvl + 1) * 2 * c]
    return att.astype(BF16)


def _gla_outputs(q, k, v, b, att, st, fwd):
    c = GLA_CHUNK
    edge = b[c - 1:c] if fwd else b[0:1]
    q_dec = q * jnp.exp(b)
    o_st = _dot_nt(_stack_heads(q_dec).astype(BF16), st.astype(BF16))
    r = _dot(att, v)
    o = jnp.concatenate([o_st[0:c] + r[0:c, 0:GLA_DV], o_st[c:] + r[c:, GLA_DV:]], axis=1)
    k_dec = (k * jnp.exp(edge - b)).astype(BF16)
    u = _dot_tn(v, k_dec)
    st_new = jnp.exp(edge) * st + jnp.where(_lane_lo(), u[0:GLA_DV], u[GLA_DV:])
    return o, st_new


GLA_GROUP = 8


def _gla_kernel(q_ref, k_ref, v_ref, rg_ref, gf_ref, gb_ref, s0f_ref, s0b_ref, gain_ref,
                tri_f_ref, masks_f_ref, tri_b_ref, masks_b_ref, y_ref, sf_ref, sb_ref,
                of_scr, ob_scr, st_scr):
    t = q_ref.shape[0]
    c = GLA_CHUNK
    nc = t // c
    group = min(GLA_GROUP, nc)
    assert nc % group == 0
    for d, s0_ref in enumerate((s0f_ref, s0b_ref)):
        st_scr[d] = jnp.concatenate([s0_ref[0], s0_ref[1]], axis=0).T
    per_dir = ((gf_ref, of_scr, tri_f_ref, masks_f_ref), (gb_ref, ob_scr, tri_b_ref, masks_b_ref))

    def body(gi, carry):
        streams = []
        for u in range(group):
            for d in range(2):
                ci = gi * group + u
                cc = ci if d == 0 else nc - 1 - ci
                streams.append((d, pl.ds(pl.multiple_of(cc * c, c), c)))
        qkg = [(q_ref[rows, :].astype(F32), k_ref[rows, :].astype(F32), per_dir[d][0][rows, :])
               for d, rows in streams]
        cums = [_gla_cumulative(g, per_dir[d][2][...]) for (d, _), (_, _, g) in zip(streams, qkg)]
        atts = [_gla_intra(q, k, g, b, per_dir[d][3][...], d == 0)
                for (d, _), (q, k, g), b in zip(streams, qkg, cums)]
        for (d, rows), (q, k, _), b, att in zip(streams, qkg, cums, atts):
            o, st_new = _gla_outputs(q, k, v_ref[rows, :], b, att, st_scr[d], d == 0)
            per_dir[d][1][rows, :] = o
            st_scr[d] = st_new
        return carry

    lax.fori_loop(0, nc // group, body, 0)

    for d, s_ref in enumerate((sf_ref, sb_ref)):
        s_pair = st_scr[d].T
        s_ref[0] = s_pair[0:GLA_DK]
        s_ref[1] = s_pair[GLA_DK:]

    rt = 256
    for r in range(t // rt):
        og = of_scr[r * rt:(r + 1) * rt, :] + ob_scr[r * rt:(r + 1) * rt, :]
        halves = []
        for hh in range(2):
            x = og[:, hh * GLA_DV:(hh + 1) * GLA_DV]
            halves.append(x * lax.rsqrt(jnp.mean(x * x, axis=-1, keepdims=True) + EPS))
        y = jnp.concatenate(halves, axis=1) * gain_ref[...] * _silu(rg_ref[r * rt:(r + 1) * rt, :].astype(F32))
        y_ref[r * rt:(r + 1) * rt, :] = y.astype(BF16)


def _gla_call(proj, gates, s0f, s0b, gain, consts, t, nb, row0, name):
    npairs = GLA_HEADS // 2
    qcol = 3 * NA_W // LANES
    kcol = qcol + GLA_KW // LANES
    vcol = (3 * NA_W + 2 * GLA_KW) // (2 * GLA_DV)
    rcol = vcol + GLA_VW // (2 * GLA_DV)
    tri_f, masks_f = consts[True]
    tri_b, masks_b = consts[False]
    st_spec = pl.BlockSpec((None, 2, GLA_DK, GLA_DV), lambda b, p: (b, p, 0, 0))
    return pl.pallas_call(
        _gla_kernel,
        out_shape=(jax.ShapeDtypeStruct((nb * t, GLA_VW), BF16),
                   jax.ShapeDtypeStruct((nb, GLA_HEADS, GLA_DK, GLA_DV), F32),
                   jax.ShapeDtypeStruct((nb, GLA_HEADS, GLA_DK, GLA_DV), F32)),
        grid=(nb, npairs),
        in_specs=[pl.BlockSpec((t, LANES), lambda b, p: (row0 + b, qcol + p)),
                  pl.BlockSpec((t, LANES), lambda b, p: (row0 + b, kcol + p)),
                  pl.BlockSpec((t, 2 * GLA_DV), lambda b, p: (row0 + b, vcol + p)),
                  pl.BlockSpec((t, 2 * GLA_DV), lambda b, p: (row0 + b, rcol + p)),
                  pl.BlockSpec((t, LANES), lambda b, p: (row0 + b, p)),
                  pl.BlockSpec((t, LANES), lambda b, p: (row0 + b, npairs + p)),
                  st_spec, st_spec,
                  pl.BlockSpec((1, 2 * GLA_DV), lambda b, p: (0, p)),
                  _const_spec(tri_f.shape), _const_spec(masks_f.shape),
                  _const_spec(tri_b.shape), _const_spec(masks_b.shape)],
        out_specs=(pl.BlockSpec((t, 2 * GLA_DV), lambda b, p: (b, p)), st_spec, st_spec),
        scratch_shapes=[pltpu.VMEM((t, 2 * GLA_DV), F32), pltpu.VMEM((t, 2 * GLA_DV), F32),
                        pltpu.VMEM((2, GLA_DV, LANES), F32)],
        compiler_params=_params(("parallel", "parallel")),
        name=name,
    )(proj, proj, proj, proj, gates, gates, s0f, s0b, gain, tri_f, masks_f, tri_b, masks_b)


C_OUT = C_QW + 4 * C_KW


def _c_in_kernel(x_ref, g_ref, sc_ref, sh_ref, w_ref, gain_ref, cos_ref, sin_ref, grp_ref, o_ref):
    h = _norm_mod(x_ref[...], g_ref[...], sc_ref[...], sh_ref[...]).astype(BF16)
    cw = 512
    nch = C_OUT // cw
    n_norm = (C_QW + 2 * C_KW) // LANES
    quarter = HD // 4
    first_half = lax.broadcasted_iota(jnp.int32, (1, LANES), 1) % (2 * quarter) < quarter
    y_ahead = _dot(h, w_ref[:, 0:cw])
    for j in range(nch):
        y = y_ahead
        if j + 1 < nch:
            y_ahead = _dot(h, w_ref[:, (j + 1) * cw:(j + 2) * cw])
        for s in range(cw // LANES):
            blk = j * (cw // LANES) + s
            yb = y[:, s * LANES:(s + 1) * LANES]
            if blk < n_norm:
                ms = _dot((yb * yb).astype(BF16), grp_ref[...])
                yn = yb * lax.rsqrt(ms + EPS) * gain_ref[:, blk * LANES:(blk + 1) * LANES]
                partner = jnp.where(first_half, pltpu.roll(yn, LANES - quarter, 1), pltpu.roll(yn, quarter, 1))
                yb = yn * cos_ref[...] + partner * sin_ref[...]
            o_ref[:, blk * LANES:(blk + 1) * LANES] = yb.astype(BF16)


def _rope_tables(geom):
    tpos = np.arange(geom.ds)
    inv = ROPE_THETA ** (-np.arange(0, HD // 2, 2, dtype=np.float32) / (HD // 2))
    ang_r = (tpos // GRID_W).astype(np.float32)[:, None] * inv
    ang_c = (tpos % GRID_W).astype(np.float32)[:, None] * inv
    ang_r, ang_c = jnp.asarray(ang_r, F32), jnp.asarray(ang_c, F32)
    cos = jnp.concatenate([jnp.cos(ang_r)] * 2 + [jnp.cos(ang_c)] * 2, axis=1)
    sin = jnp.concatenate([-jnp.sin(ang_r), jnp.sin(ang_r), -jnp.sin(ang_c), jnp.sin(ang_c)], axis=1)
    cos = jnp.concatenate([jnp.tile(cos, (1, 2)), jnp.ones((TP, LANES), F32)], axis=0)
    sin = jnp.concatenate([jnp.tile(sin, (1, 2)), jnp.zeros((TP, LANES), F32)], axis=0)
    lane = np.arange(LANES)
    grp = (lane[:, None] // HD == lane[None, :] // HD).astype(np.float32) / HD
    return cos, sin, jnp.asarray(grp, BF16)


def _c_in_proj(geom, x, mod_l, g, w, gain, tables):
    cos, sin, grp = tables
    npt, tps = geom.np_rows // TP, geom.ds // TP

    def pos_map(i):
        return (jnp.where(i < npt, tps, (i - npt) % tps), 0)
    return pl.pallas_call(
        _c_in_kernel,
        out_shape=jax.ShapeDtypeStruct((geom.n, C_OUT), BF16),
        grid=(geom.n // TP,),
        in_specs=[pl.BlockSpec((TP, D), lambda i: (i, 0)),
                  _const_spec((1, D)),
                  geom.mod_spec(1, TP), geom.mod_spec(0, TP),
                  _const_spec(w.shape), _const_spec(gain.shape),
                  pl.BlockSpec((TP, LANES), pos_map), pl.BlockSpec((TP, LANES), pos_map),
                  _const_spec(grp.shape)],
        out_specs=pl.BlockSpec((TP, C_OUT), lambda i: (i, 0)),
        compiler_params=_params(("parallel",)),
        name="gqa_in_proj",
    )(x, g, mod_l, mod_l, w, gain, cos, sin, grp)


def _gqa_attention(geom, proj, k_ctx, v_ctx):
    nqp = GQA_HEADS // 2
    kcol = C_QW // LANES
    vcol = kcol + 2 * C_KW // LANES
    s, ds = geom.s, geom.ds
    kvw = 2 * C_KW
    o_p = pl.pallas_call(
        functools.partial(_pair_attn_kernel, has_ctx=False, row_split=1, kv_share=2),
        out_shape=jax.ShapeDtypeStruct((geom.np_rows, C_QW), BF16),
        grid=(geom.b,),
        in_specs=[pl.BlockSpec((s, C_QW), lambda b: (b, 0)),
                  pl.BlockSpec((s, kvw), lambda b: (b, C_QW // kvw)),
                  pl.BlockSpec((s, kvw), lambda b: (b, C_QW // kvw + 1))],
        out_specs=pl.BlockSpec((s, C_QW), lambda b: (b, 0)),
        compiler_params=_params(("parallel",)),
        name="gqa_prompt_attn",
    )(proj, proj, proj)

    tq = ds
    nqt = ds // tq
    q0 = geom.np_rows // tq
    s0 = geom.np_rows // ds
    kc = jnp.concatenate([k_ctx, k_ctx], axis=-1).astype(BF16)
    vc = jnp.concatenate([v_ctx, v_ctx], axis=-1).astype(BF16)
    nctx = kc.shape[2]
    o_s = pl.pallas_call(
        functools.partial(_shared_kv_attn_kernel, has_ctx=True, row_split=tq // 256, kv_share=1),
        out_shape=jax.ShapeDtypeStruct((geom.ns_rows, C_QW), BF16),
        grid=(geom.db, nqp, nqt),
        in_specs=[pl.BlockSpec((tq, LANES), lambda b, m, t: (q0 + nqt * b + t, m)),
                  pl.BlockSpec((ds, LANES), lambda b, m, t: (s0 + b, kcol + m // 2)),
                  pl.BlockSpec((ds, LANES), lambda b, m, t: (s0 + b, vcol + m // 2)),
                  pl.BlockSpec((None, None, nctx, LANES), lambda b, m, t: (b, m // 2, 0, 0)),
                  pl.BlockSpec((None, None, nctx, LANES), lambda b, m, t: (b, m // 2, 0, 0))],
        out_specs=pl.BlockSpec((tq, LANES), lambda b, m, t: (nqt * b + t, m)),
        compiler_params=_params(("parallel", "parallel", "parallel")),
        name="gqa_sample_attn",
    )(proj, proj, proj, kc, vc)
    return o_p, o_s


def _ffn_kernel(*refs, npt, tps, seq, x_split, n_act):
    i = pl.program_id(0)
    j = jnp.maximum(i - npt, 0) % tps
    is_prompt = i < npt
    has_prev = jnp.logical_and(jnp.logical_not(is_prompt), j > 0).astype(F32)
    has_next = jnp.logical_and(jnp.logical_not(is_prompt), j < tps - 1).astype(F32)
    nx = 4 if x_split else 3
    x_refs = refs[:nx]
    act_refs = refs[nx:nx + 4 * n_act]
    (wout_ref, gate1_ref, g_ref, sc_ref, sh_ref, gate_ref, wup_ref, cw_ref, cb_ref, wdn_ref,
     o_ref, acc_ref, h_ref, xm_ref) = refs[nx + 4 * n_act:]
    x_main = _read_split(x_refs[0], x_refs[1], npt) if x_split else x_refs[0][...]
    mix = None
    k0 = 0
    for a in range(n_act):
        a_p, a_s, a_prev, a_next = act_refs[4 * a:4 * a + 4]
        act = jnp.concatenate([a_prev[...], _read_split(a_p, a_s, npt), a_next[...]], axis=0)
        part = _dot(act, wout_ref[k0:k0 + act.shape[1], :])
        mix = part if mix is None else mix + part
        k0 += act.shape[1]
    xm_ref[...] = jnp.concatenate([x_refs[-2][...], x_main, x_refs[-1][...]], axis=0) + gate1_ref[...] * mix
    g, sc, sh = g_ref[...], sc_ref[...], sh_ref[...]
    x = xm_ref[HALO:HALO + TF, :]
    h_ref[0:HALO, :] = (_norm_mod(xm_ref[0:HALO, :], g, sc, sh) * has_prev).astype(BF16)
    h_ref[HALO:HALO + TF, :] = _norm_mod(x, g, sc, sh).astype(BF16)
    h_ref[HALO + TF:, :] = (_norm_mod(xm_ref[HALO + TF:, :], g, sc, sh) * has_next).astype(BF16)
    rows = TF + 2 * HALO
    nch = D_FF // FF_CHUNK
    inner = list(range(seq, TF, seq))
    sub = lax.broadcasted_iota(jnp.int32, (SUBLANES, FF_CHUNK), 0)
    no_prev = jnp.where(jnp.logical_and(is_prompt, sub == 0), 0.0, 1.0)
    no_next = jnp.where(jnp.logical_and(is_prompt, sub == SUBLANES - 1), 0.0, 1.0)

    def up(c):
        return [_dot(h_ref[...], wup_ref[:, off + c * FF_CHUNK:off + (c + 1) * FF_CHUNK]) for off in (0, D_FF)]

    def conv(u, off, c):
        cols = slice(off + c * FF_CHUNK, off + (c + 1) * FF_CHUNK)
        u_prev = pltpu.roll(u, 1, 0)[HALO:HALO + TF]
        u_next = pltpu.roll(u, rows - 1, 0)[HALO:HALO + TF]
        for b in inner:
            u_prev = jnp.concatenate([u_prev[:b], u_prev[b:b + SUBLANES] * no_prev, u_prev[b + SUBLANES:]], axis=0)
            u_next = jnp.concatenate([u_next[:b - SUBLANES], u_next[b - SUBLANES:b] * no_next, u_next[b:]], axis=0)
        return (cw_ref[0:1, cols] * u_prev + cw_ref[1:2, cols] * u[HALO:HALO + TF]
                + cw_ref[2:3, cols] * u_next + cb_ref[:, cols])

    def down(c, a):
        part = _dot(a, wdn_ref[c * FF_CHUNK:(c + 1) * FF_CHUNK, :])
        if c == 0:
            acc_ref[...] = part
        else:
            acc_ref[...] += part

    u_ahead = up(0)
    a_prev = None
    for c in range(nch):
        u_val, u_gate = u_ahead
        if c + 1 < nch:
            u_ahead = up(c + 1)
        if a_prev is not None:
            down(c - 1, a_prev)
        a_prev = (_silu(conv(u_gate, D_FF, c)) * conv(u_val, 0, c)).astype(BF16)
    down(nch - 1, a_prev)
    o_ref[...] = x + gate_ref[...] * acc_ref[...]


def _mixer_out_ffn(geom, xs, act_pairs, w_out, mod_l, g, wup, cw, cb, wdn):
    assert TF % geom.s == 0 and geom.np_rows % TF == 0 and geom.ds % TF == 0
    per = TF // HALO
    npt = geom.np_rows // TF

    def halo_specs(width, nblk, first_tile):
        def prev(i):
            return (jnp.maximum((jnp.maximum(i - first_tile, 0)) * per - 1, 0), 0)

        def nxt(i):
            return (jnp.minimum((jnp.maximum(i - first_tile, 0) + 1) * per, nblk - 1), 0)
        return [pl.BlockSpec((HALO, width), prev), pl.BlockSpec((HALO, width), nxt)]

    x_split = len(xs) == 2
    if x_split:
        in_specs = geom.split_specs(TF, D) + halo_specs(D, geom.ns_rows // HALO, npt)
        args = [xs[0], xs[1], xs[1], xs[1]]
    else:
        in_specs = [pl.BlockSpec((TF, D), lambda i: (i, 0))] + halo_specs(D, geom.n // HALO, 0)
        args = [xs[0], xs[0], xs[0]]
    for a_p, a_s in act_pairs:
        in_specs += geom.split_specs(TF, a_p.shape[1]) + halo_specs(a_p.shape[1], geom.ns_rows // HALO, npt)
        args += [a_p, a_s, a_s, a_s]
    in_specs += [_const_spec(w_out.shape), geom.mod_spec(2, TF), _const_spec((1, D)),
                 geom.mod_spec(4, TF), geom.mod_spec(3, TF), geom.mod_spec(5, TF),
                 _const_spec(wup.shape), _const_spec(cw.shape), _const_spec(cb.shape), _const_spec(wdn.shape)]
    args += [w_out, mod_l, g, mod_l, mod_l, mod_l, wup, cw, cb, wdn]
    return pl.pallas_call(
        functools.partial(_ffn_kernel, npt=npt, tps=geom.ds // TF, seq=geom.s, x_split=x_split,
                          n_act=len(act_pairs)),
        out_shape=jax.ShapeDtypeStruct((geom.n, D), F32),
        grid=(geom.n // TF,),
        in_specs=in_specs,
        out_specs=pl.BlockSpec((TF, D), lambda i: (i, 0)),
        scratch_shapes=[pltpu.VMEM((TF, D), F32), pltpu.VMEM((TF + 2 * HALO, D), BF16),
                        pltpu.VMEM((TF + 2 * HALO, D), F32)],
        compiler_params=_params(("parallel",)),
        name="mixer_out_ffn",
    )(*args)


def _final_norm_kernel(x_ref, g_ref, o_ref):
    x = x_ref[...]
    o_ref[...] = x * lax.rsqrt(jnp.mean(x * x, axis=-1, keepdims=True) + EPS) * g_ref[...]


def _final_norm(x, g, row0, nrows):
    tn = 1024
    assert row0 % tn == 0 and nrows % tn == 0
    t0 = row0 // tn
    return pl.pallas_call(
        _final_norm_kernel,
        out_shape=jax.ShapeDtypeStruct((nrows, D), F32),
        grid=(nrows // tn,),
        in_specs=[pl.BlockSpec((tn, D), lambda i: (t0 + i, 0)), _const_spec((1, D))],
        out_specs=pl.BlockSpec((tn, D), lambda i: (i, 0)),
        compiler_params=_params(("parallel",)),
        name="final_norm",
    )(x, g)


def _prep_ab(w_in, w2_f, b2_f, w2_b, b2_b):
    scale = np.ones((w_in.shape[1],), np.float32)
    scale[0:NA_W] = HD ** -0.5 * LOG2E
    scale[3 * NA_W:3 * NA_W + GLA_KW] = GLA_DK ** -0.5
    w = jnp.pad(w_in * scale, ((0, 0), (0, AB_MAIN + LANES - w_in.shape[1]))).astype(BF16)
    w2 = jnp.zeros((LANES, 2 * GLA_KW), F32)
    w2 = w2.at[0:GLA_RANK, 0:GLA_KW].set(w2_f).at[GLA_RANK:2 * GLA_RANK, GLA_KW:].set(w2_b)
    w2_hi, w2_lo = _split(w2)
    b2 = jnp.concatenate([b2_f, b2_b])[None, :]
    return w, w2_hi, w2_lo, b2


def _prep_c(w_in, qn_g, kn_g):
    wq, wk, wv = w_in[:, :C_QW], w_in[:, C_QW:C_QW + C_KW], w_in[:, C_QW + C_KW:]

    def dup(w):
        return jnp.broadcast_to(w.reshape(D, GQA_KVH, 1, HD), (D, GQA_KVH, 2, HD)).reshape(D, 2 * C_KW)
    w = jnp.concatenate([wq, dup(wk), dup(wv)], axis=1).astype(BF16)
    gain = jnp.concatenate([jnp.tile(qn_g * (HD ** -0.5 * LOG2E), GQA_HEADS), jnp.tile(kn_g, 2 * GQA_KVH)])[None, :]
    return w, gain


def kernel(x_prompt, x_sample, cache_na_k, cache_na_v, state_gla_fwd, state_gla_bwd, cache_gqa_k, cache_gqa_v, c, c_ctx, ada_w, ada_b, norm_mix_g, norm_ffn_g, ab_w_in, ab_w_out, na_rpb, gla_w2_fwd, gla_b2_fwd, gla_w2_bwd, gla_b2_bwd, gla_norm_g, gqa_w_in, gqa_w_out, gqa_q_norm_g, gqa_k_norm_g, ffn_w_up, ffn_conv_w, ffn_conv_b, ffn_w_down, final_norm_g):
    b, s, _ = x_prompt.shape
    db, ds, _ = x_sample.shape
    geom = _Geom(b, s, db, ds)
    assert ds % GRID_W == 0 and geom.np_rows % ds == 0 and geom.np_rows % TP == 0

    xs = [x_prompt.reshape(-1, D), x_sample.reshape(-1, D)]
    nrow = -(-(db + 1) // SUBLANES) * SUBLANES
    c_rows = jnp.zeros((nrow, D), F32).at[:db].set(c).at[db].set(c_ctx)
    mod = _modulation(c_rows, ada_w, ada_b).reshape(DEPTH, nrow, 1, 6 * D)

    gla_consts = _gla_constants()
    rope = _rope_tables(geom)
    zeros_state = jnp.zeros((b, GLA_HEADS, GLA_DK, GLA_DV), F32)
    na_k, na_v, gla_f, gla_b, gq_k, gq_v = [], [], [], [], [], []

    for i in range(DEPTH):
        j = i // 2
        mod_l = mod[i]
        if i % 2 == 0:
            w, w2_hi, w2_lo, b2 = _prep_ab(ab_w_in[j], gla_w2_fwd[j], gla_b2_fwd[j], gla_w2_bwd[j], gla_b2_bwd[j])
            proj, gates = _ab_in_proj(geom, xs, mod_l, norm_mix_g[i][None, :], w, w2_hi, w2_lo, b2)
            oa = _na_attention(geom, proj, cache_na_k[:, j], cache_na_v[:, j], na_rpb[j])
            gain = gla_norm_g[j][None, :]
            yg_p, sf, sb = _gla_call(proj, gates, zeros_state, zeros_state, gain, gla_consts, s, b, 0, "gla_prompt")
            yg_s, _, _ = _gla_call(proj, gates, state_gla_fwd[:, j], state_gla_bwd[:, j], gain, gla_consts,
                                   ds, db, geom.np_rows // ds, "gla_sample")
            acts, w_out = [oa, (yg_p, yg_s)], ab_w_out[j]
            pk = proj[:geom.np_rows].reshape(b, s, -1)
            na_k.append(pk[..., NA_W:2 * NA_W].reshape(b, s, NA_HEADS, HD).transpose(0, 2, 1, 3).astype(F32))
            na_v.append(pk[..., 2 * NA_W:3 * NA_W].reshape(b, s, NA_HEADS, HD).transpose(0, 2, 1, 3).astype(F32))
            gla_f.append(sf)
            gla_b.append(sb)
        else:
            w, gain = _prep_c(gqa_w_in[j], gqa_q_norm_g[j], gqa_k_norm_g[j])
            proj = _c_in_proj(geom, xs[0], mod_l, norm_mix_g[i][None, :], w, gain, rope)
            o = _gqa_attention(geom, proj, cache_gqa_k[:, j], cache_gqa_v[:, j])
            acts, w_out = [o], gqa_w_out[j]
            pk = proj[:geom.np_rows].reshape(b, s, -1)
            kd = pk[..., C_QW:C_QW + 2 * C_KW].reshape(b, s, GQA_KVH, 2, HD)[:, :, :, 0]
            vd = pk[..., C_QW + 2 * C_KW:].reshape(b, s, GQA_KVH, 2, HD)[:, :, :, 0]
            gq_k.append(kd.transpose(0, 2, 1, 3).astype(F32))
            gq_v.append(vd.transpose(0, 2, 1, 3).astype(F32))
        x = _mixer_out_ffn(geom, xs, acts, w_out.astype(BF16), mod_l, norm_ffn_g[i][None, :],
                           ffn_w_up[i].astype(BF16), ffn_conv_w[i], ffn_conv_b[i][None, :], ffn_w_down[i].astype(BF16))
        xs = [x]

    y_prompt = _final_norm(x, final_norm_g[None, :], 0, geom.np_rows).reshape(b, s, D)
    y_sample = _final_norm(x, final_norm_g[None, :], geom.np_rows, geom.ns_rows).reshape(db, ds, D)
    return (y_prompt, y_sample,
            jnp.stack(na_k, axis=1), jnp.stack(na_v, axis=1),
            jnp.stack(gla_f, axis=1), jnp.stack(gla_b, axis=1),
            jnp.stack(gq_k, axis=1), jnp.stack(gq_v, axis=1))
```

```python
import functools

import numpy as np
import jax
import jax.numpy as jnp
from jax import lax
from jax.experimental import pallas as pl
from jax.experimental.pallas import tpu as pltpu

F32 = jnp.float32
BF16 = jnp.bfloat16

D = 1024
DEPTH = 4
HD = 64
GRID_W = 64
EPS = 1e-6
NA_HEADS = 8
NA_W = NA_HEADS * HD
NA_KR = 8
NA_KC = 16
GLA_HEADS = 4
GLA_DK = 64
GLA_DV = 128
GLA_KW = GLA_HEADS * GLA_DK
GLA_VW = GLA_HEADS * GLA_DV
GLA_RANK = 16
GLA_CHUNK = 64
GQA_HEADS = 16
GQA_KVH = 4
C_QW = GQA_HEADS * HD
C_KW = GQA_KVH * HD
ROPE_THETA = 10000.0
D_FF = 2816
AB_MAIN = 3 * NA_W + 2 * GLA_KW + 2 * GLA_VW

LANES = 128
SUBLANES = 8
TM = 256
TP = 1024
TF = 256
LOG2E = 1.4426950408889634
HALO = 16
FF_CHUNK = 256
NA_QROWS = 8
NA_KROWS = NA_QROWS + NA_KR - 1
VMEM_LIMIT = 56 * 1024 * 1024


def _dot(a, b):
    return jnp.dot(a, b, preferred_element_type=F32)


def _dot_nt(a, b):
    return lax.dot_general(a, b, (((1,), (1,)), ((), ())), preferred_element_type=F32)


def _dot_tn(a, b):
    return lax.dot_general(a, b, (((0,), (0,)), ((), ())), preferred_element_type=F32)


def _split(a):
    hi = a.astype(BF16)
    lo = (a - hi.astype(F32)).astype(BF16)
    return hi, lo


def _dot_split(a, w_hi, w_lo):
    a_hi, a_lo = _split(a)
    return _dot(a_hi, w_hi) + _dot(a_lo, w_hi) + _dot(a_hi, w_lo)


def _norm_mod(x, g, sc, sh):
    xn = x * lax.rsqrt(jnp.mean(x * x, axis=-1, keepdims=True) + EPS)
    return (xn * g) * (1.0 + sc) + sh


def _silu(x):
    return x / (1.0 + jnp.exp(-x))


def _lane_lo(shape=(1, LANES)):
    return lax.broadcasted_iota(jnp.int32, shape, len(shape) - 1) < HD


def _stack_heads(x):
    lo = _lane_lo()
    zero = jnp.zeros_like(x)
    return jnp.concatenate([jnp.where(lo, x, zero), jnp.where(lo, zero, x)], axis=0)


def _params(sem):
    return pltpu.CompilerParams(dimension_semantics=sem, vmem_limit_bytes=VMEM_LIMIT)


def _mod_kernel(c_ref, w_ref, b_ref, o_ref):
    a = _silu(c_ref[...])
    w = w_ref[...]
    w_hi, w_lo = _split(w)
    o_ref[...] = _dot_split(a, w_hi, w_lo) + b_ref[...]


def _modulation(c_rows, ada_w, ada_b):
    nrow = c_rows.shape[0]
    tn = 1024
    return pl.pallas_call(
        _mod_kernel,
        out_shape=jax.ShapeDtypeStruct((DEPTH, nrow, 6 * D), F32),
        grid=(DEPTH, 6 * D // tn),
        in_specs=[pl.BlockSpec((nrow, D), lambda l, n: (0, 0)),
                  pl.BlockSpec((None, D, tn), lambda l, n: (l, 0, n)),
                  pl.BlockSpec((None, 1, tn), lambda l, n: (l, 0, n))],
        out_specs=pl.BlockSpec((None, nrow, tn), lambda l, n: (l, 0, n)),
        compiler_params=_params(("parallel", "parallel")),
        name="modulation",
    )(c_rows, ada_w, ada_b.reshape(DEPTH, 1, 6 * D))


class _Geom:
    def __init__(self, b, s, db, ds):
        assert s == TM and ds % TM == 0
        self.b, self.s, self.db, self.ds = b, s, db, ds
        self.np_rows = b * s
        self.ns_rows = db * ds
        self.n = self.np_rows + self.ns_rows
        self.npt = self.np_rows // TM
        self.tps = ds // TM
        self.ctx_row = db

    def mod_spec(self, k, tm=TM):
        assert self.np_rows % tm == 0 and self.ds % tm == 0
        npt, tps, ctx = self.np_rows // tm, self.ds // tm, self.ctx_row

        def imap(i):
            return (jnp.where(i < npt, ctx, (i - npt) // tps), 0, k)
        return pl.BlockSpec((None, 1, D), imap)

    def split_specs(self, tm, width):
        npt = self.np_rows // tm
        return [pl.BlockSpec((tm, width), lambda i: (jnp.minimum(i, npt - 1), 0)),
                pl.BlockSpec((tm, width), lambda i: (jnp.maximum(i - npt, 0), 0))]


def _read_split(p_ref, s_ref, npt):
    return jnp.where(pl.program_id(0) < npt, p_ref[...], s_ref[...])


def _const_spec(shape):
    nd = len(shape)
    return pl.BlockSpec(shape, lambda *_: (0,) * nd)


def _ab_in_kernel(*refs, split_npt):
    nx = 2 if split_npt else 1
    g_ref, sc_ref, sh_ref, w_ref, w2hi_ref, w2lo_ref, b2_ref, o_ref, gate_ref = refs[nx:]
    x = _read_split(refs[0], refs[1], split_npt) if split_npt else refs[0][...]
    h = _norm_mod(x, g_ref[...], sc_ref[...], sh_ref[...]).astype(BF16)
    cw = 512
    lr = _dot(h, w_ref[:, AB_MAIN:AB_MAIN + LANES])
    o_ref[:, 0:cw] = _dot(h, w_ref[:, 0:cw]).astype(BF16)
    z = _dot_split(lr, w2hi_ref[...], w2lo_ref[...]) + b2_ref[...]
    for j in range(1, AB_MAIN // cw):
        o_ref[:, j * cw:(j + 1) * cw] = _dot(h, w_ref[:, j * cw:(j + 1) * cw]).astype(BF16)
    gate_ref[...] = (jnp.minimum(z, 0.0) - jnp.log(1.0 + jnp.exp(-jnp.abs(z)))) * (1.0 / 16.0)


def _ab_in_proj(geom, xs, mod_l, g, w, w2hi, w2lo, b2):
    split = len(xs) == 2
    x_specs = geom.split_specs(TP, D) if split else [pl.BlockSpec((TP, D), lambda i: (i, 0))]
    return pl.pallas_call(
        functools.partial(_ab_in_kernel, split_npt=geom.np_rows // TP if split else 0),
        out_shape=(jax.ShapeDtypeStruct((geom.n, AB_MAIN), BF16),
                   jax.ShapeDtypeStruct((geom.n, 2 * GLA_KW), F32)),
        grid=(geom.n // TP,),
        in_specs=x_specs + [_const_spec((1, D)),
                            geom.mod_spec(1, TP), geom.mod_spec(0, TP),
                            _const_spec(w.shape), _const_spec(w2hi.shape), _const_spec(w2lo.shape),
                            _const_spec(b2.shape)],
        out_specs=(pl.BlockSpec((TP, AB_MAIN), lambda i: (i, 0)),
                   pl.BlockSpec((TP, 2 * GLA_KW), lambda i: (i, 0))),
        compiler_params=_params(("parallel",)),
        name="ab_in_proj",
    )(*xs, g, mod_l, mod_l, w, w2hi, w2lo, b2)


def _softmax_units(n_units, scores, values, emit):
    def finish(u, p_parts, l):
        emit(u, sum(_dot(p, v) for p, v in zip(p_parts, values(u))) / l)

    s_ahead = scores(0)
    pending = None
    for u in range(n_units):
        s_parts = s_ahead
        if u + 1 < n_units:
            s_ahead = scores(u + 1)
        if pending is not None:
            finish(u - 1, *pending)
        m = functools.reduce(jnp.maximum, [jnp.max(s, axis=-1, keepdims=True) for s in s_parts])
        p_parts = [jnp.exp2(s - m) for s in s_parts]
        l = sum(jnp.sum(p, axis=-1, keepdims=True) for p in p_parts)
        pending = ([p.astype(BF16) for p in p_parts], l)
    finish(n_units - 1, *pending)


def _head_of_pair(x, hh):
    lo = _lane_lo()
    zero = jnp.zeros_like(x)
    return jnp.where(lo, x, zero) if hh == 0 else jnp.where(lo, zero, x)


def _pair_attn_kernel(*refs, has_ctx, row_split, kv_share):
    if has_ctx:
        q_ref, k_ref, v_ref, kc_ref, vc_ref, o_ref = refs
    else:
        q_ref, k_ref, v_ref, o_ref = refs
    npairs = q_ref.shape[1] // LANES
    rows = q_ref.shape[0] // row_split
    units = [(m, r, hh) for m in range(npairs) for r in range(row_split) for hh in range(2)]
    outs = {}

    def kv_cols(m):
        c0 = (m // kv_share) * LANES
        return slice(c0, c0 + LANES)

    def scores(u):
        m, r, hh = units[u]
        qb = _head_of_pair(q_ref[r * rows:(r + 1) * rows, m * LANES:(m + 1) * LANES], hh)
        parts = [_dot_nt(qb, k_ref[:, kv_cols(m)])]
        if has_ctx:
            parts.append(_dot_nt(qb, kc_ref[...]))
        return parts

    def values(u):
        m = units[u][0]
        return [v_ref[:, kv_cols(m)]] + ([vc_ref[...]] if has_ctx else [])

    def emit(u, out):
        outs[units[u]] = out

    _softmax_units(len(units), scores, values, emit)
    lo = _lane_lo()
    for m in range(npairs):
        for r in range(row_split):
            o_ref[r * rows:(r + 1) * rows, m * LANES:(m + 1) * LANES] = jnp.where(
                lo, outs[(m, r, 0)], outs[(m, r, 1)]).astype(BF16)


def _shared_kv_attn_kernel(*refs, has_ctx, row_split, kv_share):
    if has_ctx:
        q_ref, k_ref, v_ref, kc_ref, vc_ref, o_ref = refs
    else:
        q_ref, k_ref, v_ref, o_ref = refs
    npairs = q_ref.shape[1] // LANES
    rows = q_ref.shape[0] // row_split
    units = [(m, r, hh) for m in range(npairs) for r in range(row_split) for hh in range(2)]
    outs = {}

    def kv_cols(m):
        c0 = (m // kv_share) * LANES
        return slice(c0, c0 + LANES)

    def values_t(ref, cols):
        vt = ref[:, cols].astype(F32).T
        return jnp.where(lax.broadcasted_iota(jnp.int32, vt.shape, 0) < HD, vt, 1.0).astype(BF16)

    vt_cache = {}

    def values(u):
        m = units[u][0]
        key = m // kv_share
        if key not in vt_cache:
            vt_cache[key] = [values_t(v_ref, kv_cols(m))] + ([values_t(vc_ref, slice(None))] if has_ctx else [])
        return vt_cache[key]

    def scores(u):
        m, r, hh = units[u]
        qb = _head_of_pair(q_ref[r * rows:(r + 1) * rows, m * LANES:(m + 1) * LANES], hh)
        parts = [_dot_nt(k_ref[:, kv_cols(m)], qb)]
        if has_ctx:
            parts.append(_dot_nt(kc_ref[...], qb))
        return parts

    def finish(u, p_parts):
        ext = sum(_dot(vt, p) for vt, p in zip(values(u), p_parts))
        outs[units[u]] = ext[0:HD] / ext[HD:HD + 1]

    s_ahead = scores(0)
    pending = None
    for u in range(len(units)):
        s_parts = s_ahead
        if u + 1 < len(units):
            s_ahead = scores(u + 1)
        if pending is not None:
            finish(u - 1, pending)
        m_col = functools.reduce(jnp.maximum, [jnp.max(s, axis=0, keepdims=True) for s in s_parts])
        pending = [jnp.exp2(s - m_col).astype(BF16) for s in s_parts]
    finish(len(units) - 1, pending)
    for m in range(npairs):
        for r in range(row_split):
            both = jnp.concatenate([outs[(m, r, 0)], outs[(m, r, 1)]], axis=0)
            o_ref[r * rows:(r + 1) * rows, m * LANES:(m + 1) * LANES] = both.T.astype(BF16)


def _na_block_rows(rb, rows):
    r0 = rb * NA_QROWS
    return r0, int(np.clip(r0 - NA_KR // 2, 0, rows - NA_KROWS))


def _na_kernel(q_ref, k_ref, v_ref, kc_ref, vc_ref, tab_ref, rowmask_ref, o_ref):
    rows = q_ref.shape[0] // GRID_W
    nrb = rows // NA_QROWS
    nq = NA_QROWS * GRID_W
    nk = NA_KROWS * GRID_W
    units = [(rb, hh) for rb in range(nrb) for hh in range(2)]
    outs = {}

    def probabilities(u, s_loc, s_ctx):
        rb, hh = units[u]
        r0, kstart = _na_block_rows(rb, rows)
        loc_cols, ctx_cols = [], []
        for jp in range(NA_QROWS // 2):
            starts = [int(np.clip(r0 + 2 * jp + e - NA_KR // 2, 0, rows - NA_KR)) - kstart for e in (0, 1)]
            lo, hi = min(starts), max(starts) + NA_KR
            first = kstart - (r0 + 2 * jp) + 2 * (NA_KR - 1)
            lanes = slice(jp * LANES, (jp + 1) * LANES)
            s = (s_loc[lo * GRID_W:hi * GRID_W, lanes]
                 + tab_ref[hh, (first + lo) * GRID_W:(first + hi) * GRID_W, :]
                 + rowmask_ref[rb, jp, lo * GRID_W:hi * GRID_W, :])
            sc = s_ctx[:, lanes]
            m_col = jnp.maximum(jnp.max(s, axis=0, keepdims=True), jnp.max(sc, axis=0, keepdims=True))
            pieces = [jnp.exp2(s - m_col).astype(BF16)]
            if lo:
                pieces.insert(0, jnp.zeros((lo * GRID_W, LANES), BF16))
            if hi < NA_KROWS:
                pieces.append(jnp.zeros(((NA_KROWS - hi) * GRID_W, LANES), BF16))
            loc_cols.append(jnp.concatenate(pieces, axis=0))
            ctx_cols.append(jnp.exp2(sc - m_col).astype(BF16))
        return [jnp.concatenate(loc_cols, axis=1), jnp.concatenate(ctx_cols, axis=1)]

    def keys_at(rb):
        k0 = _na_block_rows(rb, rows)[1] * GRID_W
        return slice(k0, k0 + nk)

    def scores(u):
        rb, hh = units[u]
        qb = _head_of_pair(q_ref[rb * nq:(rb + 1) * nq, :], hh)
        return [_dot_nt(k_ref[keys_at(rb), :], qb), _dot_nt(kc_ref[...], qb)]

    vt_cache = {}

    def with_ones(v):
        vt = v.astype(F32).T.astype(BF16)
        return jnp.concatenate([vt, jnp.ones((HALO, vt.shape[1]), BF16)], axis=0)

    def values_t(rb):
        if "ctx" not in vt_cache:
            vt_cache["ctx"] = with_ones(vc_ref[...])
        if rb not in vt_cache:
            vt_cache[rb] = with_ones(v_ref[keys_at(rb), :])
        return [vt_cache[rb], vt_cache["ctx"]]

    def finish(u, p_parts):
        rb, hh = units[u]
        ext = sum(_dot(vt, p) for vt, p in zip(values_t(rb), p_parts))
        outs[units[u]] = ext[hh * HD:(hh + 1) * HD] / ext[LANES:LANES + 1]

    s_ahead = scores(0)
    pending = None
    for u in range(len(units)):
        s_parts = s_ahead
        if u + 1 < len(units):
            s_ahead = scores(u + 1)
        if pending is not None:
            finish(u - 1, pending)
        pending = probabilities(u, *s_parts)
    finish(len(units) - 1, pending)
    for rb in range(nrb):
        both = jnp.concatenate([outs[(rb, 0)], outs[(rb, 1)]], axis=0)
        o_ref[rb * nq:(rb + 1) * nq, :] = both.T.astype(BF16)


NA_TAB_BLOCKS = 30


def _na_bias_tables(rpb, rows):
    ndr = 2 * NA_KR - 1
    col = np.arange(GRID_W)
    cs = np.clip(col - NA_KC // 2, 0, GRID_W - NA_KC)
    col_ok = (col[None, :] >= cs[:, None]) & (col[None, :] < cs[:, None] + NA_KC)
    dc = np.clip(col[None, :] - col[:, None] + NA_KC - 1, 0, 2 * NA_KC - 2)
    pick_dc = (np.arange(2 * NA_KC - 1)[:, None, None] == dc[None]).astype(np.float32)
    by_col = jnp.einsum('hdj,jqc->hdcq', rpb.astype(F32) * LOG2E, pick_dc, precision=lax.Precision.HIGHEST)
    by_col = jnp.where(col_ok.T[None, None], by_col, -jnp.inf)
    flat = by_col.reshape(NA_HEADS // 2, 2, ndr * GRID_W, GRID_W)
    lead = NA_KR - 1
    tail = NA_TAB_BLOCKS - ndr - lead

    def shifted(shift):
        return jnp.pad(flat, ((0, 0), (0, 0), ((lead + shift) * GRID_W, (tail - shift) * GRID_W), (0, 0)))
    table = jnp.concatenate([shifted(0), shifted(1)], axis=-1)
    nrb = rows // NA_QROWS
    row_mask = np.full((nrb, NA_QROWS // 2, NA_KROWS, GRID_W, 2, GRID_W), -np.inf, np.float32)
    for rb in range(nrb):
        r0, kstart = _na_block_rows(rb, rows)
        for qr in range(NA_QROWS):
            rs = int(np.clip(r0 + qr - NA_KR // 2, 0, rows - NA_KR))
            for kk in range(NA_KROWS):
                if rs <= kstart + kk < rs + NA_KR:
                    row_mask[rb, qr // 2, kk, :, qr % 2, :] = 0.0
    return table, jnp.asarray(row_mask.reshape(nrb, NA_QROWS // 2, NA_KROWS * GRID_W, 2 * GRID_W))


def _na_attention(geom, proj, k_ctx, v_ctx, rpb):
    npairs = NA_HEADS // 2
    s, ds = geom.s, geom.ds
    oa_p = pl.pallas_call(
        functools.partial(_pair_attn_kernel, has_ctx=False, row_split=1, kv_share=1),
        out_shape=jax.ShapeDtypeStruct((geom.np_rows, NA_W), BF16),
        grid=(geom.b,),
        in_specs=[pl.BlockSpec((s, NA_W), lambda b: (b, 0)),
                  pl.BlockSpec((s, NA_W), lambda b: (b, 1)),
                  pl.BlockSpec((s, NA_W), lambda b: (b, 2))],
        out_specs=pl.BlockSpec((s, NA_W), lambda b: (b, 0)),
        compiler_params=_params(("parallel",)),
        name="na_prompt_attn",
    )(proj, proj, proj)

    rows = ds // GRID_W
    assert rows % NA_QROWS == 0 and rows >= NA_KROWS
    table, row_mask = _na_bias_tables(rpb, rows)
    s0 = geom.np_rows // ds
    kc = k_ctx.reshape(geom.db, npairs, 2, -1, HD).transpose(0, 1, 3, 2, 4).reshape(geom.db, npairs, -1, LANES).astype(BF16)
    vc = v_ctx.reshape(geom.db, npairs, 2, -1, HD).transpose(0, 1, 3, 2, 4).reshape(geom.db, npairs, -1, LANES).astype(BF16)
    nctx = kc.shape[2]
    tab_spec = pl.BlockSpec((None, 2, NA_TAB_BLOCKS * GRID_W, LANES), lambda p, b: (p, 0, 0, 0))
    oa_s = pl.pallas_call(
        _na_kernel,
        out_shape=jax.ShapeDtypeStruct((geom.ns_rows, NA_W), BF16),
        grid=(npairs, geom.db),
        in_specs=[pl.BlockSpec((ds, LANES), lambda p, b: (s0 + b, p)),
                  pl.BlockSpec((ds, LANES), lambda p, b: (s0 + b, npairs + p)),
                  pl.BlockSpec((ds, LANES), lambda p, b: (s0 + b, 2 * npairs + p)),
                  pl.BlockSpec((None, None, nctx, LANES), lambda p, b: (b, p, 0, 0)),
                  pl.BlockSpec((None, None, nctx, LANES), lambda p, b: (b, p, 0, 0)),
                  tab_spec, _const_spec(row_mask.shape)],
        out_specs=pl.BlockSpec((ds, LANES), lambda p, b: (b, p)),
        compiler_params=_params(("parallel", "parallel")),
        name="na_sample_attn",
    )(proj, proj, proj, kc, vc, table, row_mask)
    return oa_p, oa_s


GLA_LEVELS = 6


def _gla_constants():
    c = GLA_CHUNK
    t = np.arange(c)
    i, j = t[:, None], t[None, :]
    out = {}
    for fwd in (True, False):
        tri = (j <= i) if fwd else (j >= i)
        masks = np.zeros((GLA_LEVELS + 1, c, c), np.float32)
        for lvl in range(GLA_LEVELS):
            half = c >> (lvl + 1)
            upper = (t % (2 * half)) >= half
            same = (i // (2 * half)) == (j // (2 * half))
            masks[lvl] = (same & upper[:, None] & ~upper[None, :]) if fwd else (same & ~upper[:, None] & upper[None, :])
        masks[GLA_LEVELS] = np.eye(c)
        out[fwd] = (jnp.asarray(tri.astype(np.float32), BF16),
                    jnp.asarray(np.concatenate([masks, masks], axis=1).reshape(-1, c), F32))
    return out


def _row_of_block(x, block, row):
    c = x.shape[0]
    if block >= SUBLANES:
        return jnp.concatenate(
            [jnp.broadcast_to(x[b * block + row:b * block + row + 1, :], (block, x.shape[1])) for b in range(c // block)],
            axis=0)
    per = SUBLANES // block
    sub = lax.broadcasted_iota(jnp.int32, x.shape, 0) % SUBLANES
    out = _row_of_block(x, SUBLANES, row)
    for p in range(1, per):
        out = jnp.where(sub >= p * block, _row_of_block(x, SUBLANES, p * block + row), out)
    return out


def _gla_cumulative(g, tri):
    g_hi, g_lo = _split(g)
    return _dot(tri, g_hi) + _dot(tri, g_lo)


def _gla_intra(q, k, g, b, masks, fwd):
    c = GLA_CHUNK
    t = lax.broadcasted_iota(jnp.int32, b.shape, 0)
    att = jnp.zeros((2 * c, c), F32)
    for lvl in range(GLA_LEVELS + 1):
        half = c >> (lvl + 1)
        if half >= 2:
            ref = _row_of_block(b, 2 * half, half - 1 if fwd else half)
            past_mid = ((t % (2 * half)) >= half) == fwd
            w = jnp.exp(jnp.where(past_mid, b - ref, ref - b))
            ql, kl = q * w, k * w
        elif half == 1:
            w = jnp.exp(jnp.where((t % 2 == 1) == fwd, g, 0.0))
            ql, kl = q * w, k * w
        else:
            ql, kl = q, k
        p = _dot_nt(_stack_heads(ql).astype(BF16), kl.astype(BF16))
        att = att + p * masks[lvl * 2 * c:(lvl + 1) * 2 * c]
    return att.astype(BF16)


def _gla_outputs(q, k, v, b, att, st, fwd):
    c = GLA_CHUNK
    edge = b[c - 1:c] if fwd else b[0:1]
    q_dec = q * jnp.exp(b)
    o_st = _dot_nt(_stack_heads(q_dec).astype(BF16), st.astype(BF16))
    r = _dot(att, v)
    o = jnp.concatenate([o_st[0:c] + r[0:c, 0:GLA_DV], o_st[c:] + r[c:, GLA_DV:]], axis=1)
    k_dec = (k * jnp.exp(edge - b)).astype(BF16)
    u = _dot_tn(v, k_dec)
    st_new = jnp.exp(edge) * st + jnp.where(_lane_lo(), u[0:GLA_DV], u[GLA_DV:])
    return o, st_new


GLA_GROUP = 8


def _gla_kernel(q_ref, k_ref, v_ref, rg_ref, gf_ref, gb_ref, s0f_ref, s0b_ref, gain_ref,
                tri_f_ref, masks_f_ref, tri_b_ref, masks_b_ref, y_ref, sf_ref, sb_ref,
                of_scr, ob_scr, st_scr):
    t = q_ref.shape[0]
    c = GLA_CHUNK
    nc = t // c
    group = min(GLA_GROUP, nc)
    assert nc % group == 0
    for d, s0_ref in enumerate((s0f_ref, s0b_ref)):
        st_scr[d] = jnp.concatenate([s0_ref[0], s0_ref[1]], axis=0).T
    per_dir = ((gf_ref, of_scr, tri_f_ref, masks_f_ref), (gb_ref, ob_scr, tri_b_ref, masks_b_ref))

    def body(gi, carry):
        streams = []
        for u in range(group):
            for d in range(2):
                ci = gi * group + u
                cc = ci if d == 0 else nc - 1 - ci
                streams.append((d, pl.ds(pl.multiple_of(cc * c, c), c)))
        qkg = [(q_ref[rows, :].astype(F32), k_ref[rows, :].astype(F32), per_dir[d][0][rows, :])
               for d, rows in streams]
        cums = [_gla_cumulative(g, per_dir[d][2][...]) for (d, _), (_, _, g) in zip(streams, qkg)]
        atts = [_gla_intra(q, k, g, b, per_dir[d][3][...], d == 0)
                for (d, _), (q, k, g), b in zip(streams, qkg, cums)]
        for (d, rows), (q, k, _), b, att in zip(streams, qkg, cums, atts):
            o, st_new = _gla_outputs(q, k, v_ref[rows, :], b, att, st_scr[d], d == 0)
            per_dir[d][1][rows, :] = o
            st_scr[d] = st_new
        return carry

    lax.fori_loop(0, nc // group, body, 0)

    for d, s_ref in enumerate((sf_ref, sb_ref)):
        s_pair = st_scr[d].T
        s_ref[0] = s_pair[0:GLA_DK]
        s_ref[1] = s_pair[GLA_DK:]

    rt = 256
    for r in range(t // rt):
        og = of_scr[r * rt:(r + 1) * rt, :] + ob_scr[r * rt:(r + 1) * rt, :]
        halves = []
        for hh in range(2):
            x = og[:, hh * GLA_DV:(hh + 1) * GLA_DV]
            halves.append(x * lax.rsqrt(jnp.mean(x * x, axis=-1, keepdims=True) + EPS))
        y = jnp.concatenate(halves, axis=1) * gain_ref[...] * _silu(rg_ref[r * rt:(r + 1) * rt, :].astype(F32))
        y_ref[r * rt:(r + 1) * rt, :] = y.astype(BF16)


def _gla_call(proj, gates, s0f, s0b, gain, consts, t, nb, row0, name):
    npairs = GLA_HEADS // 2
    qcol = 3 * NA_W // LANES
    kcol = qcol + GLA_KW // LANES
    vcol = (3 * NA_W + 2 * GLA_KW) // (2 * GLA_DV)
    rcol = vcol + GLA_VW // (2 * GLA_DV)
    tri_f, masks_f = consts[True]
    tri_b, masks_b = consts[False]
    st_spec = pl.BlockSpec((None, 2, GLA_DK, GLA_DV), lambda b, p: (b, p, 0, 0))
    return pl.pallas_call(
        _gla_kernel,
        out_shape=(jax.ShapeDtypeStruct((nb * t, GLA_VW), BF16),
                   jax.ShapeDtypeStruct((nb, GLA_HEADS, GLA_DK, GLA_DV), F32),
                   jax.ShapeDtypeStruct((nb, GLA_HEADS, GLA_DK, GLA_DV), F32)),
        grid=(nb, npairs),
        in_specs=[pl.BlockSpec((t, LANES), lambda b, p: (row0 + b, qcol + p)),
                  pl.BlockSpec((t, LANES), lambda b, p: (row0 + b, kcol + p)),
                  pl.BlockSpec((t, 2 * GLA_DV), lambda b, p: (row0 + b, vcol + p)),
                  pl.BlockSpec((t, 2 * GLA_DV), lambda b, p: (row0 + b, rcol + p)),
                  pl.BlockSpec((t, LANES), lambda b, p: (row0 + b, p)),
                  pl.BlockSpec((t, LANES), lambda b, p: (row0 + b, npairs + p)),
                  st_spec, st_spec,
                  pl.BlockSpec((1, 2 * GLA_DV), lambda b, p: (0, p)),
                  _const_spec(tri_f.shape), _const_spec(masks_f.shape),
                  _const_spec(tri_b.shape), _const_spec(masks_b.shape)],
        out_specs=(pl.BlockSpec((t, 2 * GLA_DV), lambda b, p: (b, p)), st_spec, st_spec),
        scratch_shapes=[pltpu.VMEM((t, 2 * GLA_DV), F32), pltpu.VMEM((t, 2 * GLA_DV), F32),
                        pltpu.VMEM((2, GLA_DV, LANES), F32)],
        compiler_params=_params(("parallel", "parallel")),
        name=name,
    )(proj, proj, proj, proj, gates, gates, s0f, s0b, gain, tri_f, masks_f, tri_b, masks_b)


C_OUT = C_QW + 4 * C_KW


def _c_in_kernel(x_ref, g_ref, sc_ref, sh_ref, w_ref, gain_ref, cos_ref, sin_ref, grp_ref, o_ref):
    h = _norm_mod(x_ref[...], g_ref[...], sc_ref[...], sh_ref[...]).astype(BF16)
    cw = 512
    nch = C_OUT // cw
    n_norm = (C_QW + 2 * C_KW) // LANES
    quarter = HD // 4
    first_half = lax.broadcasted_iota(jnp.int32, (1, LANES), 1) % (2 * quarter) < quarter
    y_ahead = _dot(h, w_ref[:, 0:cw])
    for j in range(nch):
        y = y_ahead
        if j + 1 < nch:
            y_ahead = _dot(h, w_ref[:, (j + 1) * cw:(j + 2) * cw])
        for s in range(cw // LANES):
            blk = j * (cw // LANES) + s
            yb = y[:, s * LANES:(s + 1) * LANES]
            if blk < n_norm:
                ms = _dot((yb * yb).astype(BF16), grp_ref[...])
                yn = yb * lax.rsqrt(ms + EPS) * gain_ref[:, blk * LANES:(blk + 1) * LANES]
                partner = jnp.where(first_half, pltpu.roll(yn, LANES - quarter, 1), pltpu.roll(yn, quarter, 1))
                yb = yn * cos_ref[...] + partner * sin_ref[...]
            o_ref[:, blk * LANES:(blk + 1) * LANES] = yb.astype(BF16)


def _rope_tables(geom):
    tpos = np.arange(geom.ds)
    inv = ROPE_THETA ** (-np.arange(0, HD // 2, 2, dtype=np.float32) / (HD // 2))
    ang_r = (tpos // GRID_W).astype(np.float32)[:, None] * inv
    ang_c = (tpos % GRID_W).astype(np.float32)[:, None] * inv
    ang_r, ang_c = jnp.asarray(ang_r, F32), jnp.asarray(ang_c, F32)
    cos = jnp.concatenate([jnp.cos(ang_r)] * 2 + [jnp.cos(ang_c)] * 2, axis=1)
    sin = jnp.concatenate([-jnp.sin(ang_r), jnp.sin(ang_r), -jnp.sin(ang_c), jnp.sin(ang_c)], axis=1)
    cos = jnp.concatenate([jnp.tile(cos, (1, 2)), jnp.ones((TP, LANES), F32)], axis=0)
    sin = jnp.concatenate([jnp.tile(sin, (1, 2)), jnp.zeros((TP, LANES), F32)], axis=0)
    lane = np.arange(LANES)
    grp = (lane[:, None] // HD == lane[None, :] // HD).astype(np.float32) / HD
    return cos, sin, jnp.asarray(grp, BF16)


def _c_in_proj(geom, x, mod_l, g, w, gain, tables):
    cos, sin, grp = tables
    npt, tps = geom.np_rows // TP, geom.ds // TP

    def pos_map(i):
        return (jnp.where(i < npt, tps, (i - npt) % tps), 0)
    return pl.pallas_call(
        _c_in_kernel,
        out_shape=jax.ShapeDtypeStruct((geom.n, C_OUT), BF16),
        grid=(geom.n // TP,),
        in_specs=[pl.BlockSpec((TP, D), lambda i: (i, 0)),
                  _const_spec((1, D)),
                  geom.mod_spec(1, TP), geom.mod_spec(0, TP),
                  _const_spec(w.shape), _const_spec(gain.shape),
                  pl.BlockSpec((TP, LANES), pos_map), pl.BlockSpec((TP, LANES), pos_map),
                  _const_spec(grp.shape)],
        out_specs=pl.BlockSpec((TP, C_OUT), lambda i: (i, 0)),
        compiler_params=_params(("parallel",)),
        name="gqa_in_proj",
    )(x, g, mod_l, mod_l, w, gain, cos, sin, grp)


def _gqa_attention(geom, proj, k_ctx, v_ctx):
    nqp = GQA_HEADS // 2
    kcol = C_QW // LANES
    vcol = kcol + 2 * C_KW // LANES
    s, ds = geom.s, geom.ds
    kvw = 2 * C_KW
    o_p = pl.pallas_call(
        functools.partial(_pair_attn_kernel, has_ctx=False, row_split=1, kv_share=2),
        out_shape=jax.ShapeDtypeStruct((geom.np_rows, C_QW), BF16),
        grid=(geom.b,),
        in_specs=[pl.BlockSpec((s, C_QW), lambda b: (b, 0)),
                  pl.BlockSpec((s, kvw), lambda b: (b, C_QW // kvw)),
                  pl.BlockSpec((s, kvw), lambda b: (b, C_QW // kvw + 1))],
        out_specs=pl.BlockSpec((s, C_QW), lambda b: (b, 0)),
        compiler_params=_params(("parallel",)),
        name="gqa_prompt_attn",
    )(proj, proj, proj)

    tq = ds
    nqt = ds // tq
    q0 = geom.np_rows // tq
    s0 = geom.np_rows // ds
    kc = jnp.concatenate([k_ctx, k_ctx], axis=-1).astype(BF16)
    vc = jnp.concatenate([v_ctx, v_ctx], axis=-1).astype(BF16)
    nctx = kc.shape[2]
    o_s = pl.pallas_call(
        functools.partial(_shared_kv_attn_kernel, has_ctx=True, row_split=tq // 512, kv_share=1),
        out_shape=jax.ShapeDtypeStruct((geom.ns_rows, C_QW), BF16),
        grid=(geom.db, nqp, nqt),
        in_specs=[pl.BlockSpec((tq, LANES), lambda b, m, t: (q0 + nqt * b + t, m)),
                  pl.BlockSpec((ds, LANES), lambda b, m, t: (s0 + b, kcol + m // 2)),
                  pl.BlockSpec((ds, LANES), lambda b, m, t: (s0 + b, vcol + m // 2)),
                  pl.BlockSpec((None, None, nctx, LANES), lambda b, m, t: (b, m // 2, 0, 0)),
                  pl.BlockSpec((None, None, nctx, LANES), lambda b, m, t: (b, m // 2, 0, 0))],
        out_specs=pl.BlockSpec((tq, LANES), lambda b, m, t: (nqt * b + t, m)),
        compiler_params=_params(("parallel", "parallel", "parallel")),
        name="gqa_sample_attn",
    )(proj, proj, proj, kc, vc)
    return o_p, o_s


def _ffn_kernel(*refs, npt, tps, seq, x_split, n_act):
    i = pl.program_id(0)
    j = jnp.maximum(i - npt, 0) % tps
    is_prompt = i < npt
    has_prev = jnp.logical_and(jnp.logical_not(is_prompt), j > 0).astype(F32)
    has_next = jnp.logical_and(jnp.logical_not(is_prompt), j < tps - 1).astype(F32)
    nx = 4 if x_split else 3
    x_refs = refs[:nx]
    act_refs = refs[nx:nx + 4 * n_act]
    (wout_ref, gate1_ref, g_ref, sc_ref, sh_ref, gate_ref, wup_ref, cw_ref, cb_ref, wdn_ref,
     o_ref, acc_ref, h_ref, xm_ref) = refs[nx + 4 * n_act:]
    x_main = _read_split(x_refs[0], x_refs[1], npt) if x_split else x_refs[0][...]
    mix = None
    k0 = 0
    for a in range(n_act):
        a_p, a_s, a_prev, a_next = act_refs[4 * a:4 * a + 4]
        act = jnp.concatenate([a_prev[...], _read_split(a_p, a_s, npt), a_next[...]], axis=0)
        part = _dot(act, wout_ref[k0:k0 + act.shape[1], :])
        mix = part if mix is None else mix + part
        k0 += act.shape[1]
    xm_ref[...] = jnp.concatenate([x_refs[-2][...], x_main, x_refs[-1][...]], axis=0) + gate1_ref[...] * mix
    g, sc, sh = g_ref[...], sc_ref[...], sh_ref[...]
    x = xm_ref[HALO:HALO + TF, :]
    h_ref[0:HALO, :] = (_norm_mod(xm_ref[0:HALO, :], g, sc, sh) * has_prev).astype(BF16)
    h_ref[HALO:HALO + TF, :] = _norm_mod(x, g, sc, sh).astype(BF16)
    h_ref[HALO + TF:, :] = (_norm_mod(xm_ref[HALO + TF:, :], g, sc, sh) * has_next).astype(BF16)
    rows = TF + 2 * HALO
    nch = D_FF // FF_CHUNK
    inner = list(range(seq, TF, seq))
    sub = lax.broadcasted_iota(jnp.int32, (SUBLANES, FF_CHUNK), 0)
    no_prev = jnp.where(jnp.logical_and(is_prompt, sub == 0), 0.0, 1.0)
    no_next = jnp.where(jnp.logical_and(is_prompt, sub == SUBLANES - 1), 0.0, 1.0)

    def up(c):
        return [_dot(h_ref[...], wup_ref[:, off + c * FF_CHUNK:off + (c + 1) * FF_CHUNK]) for off in (0, D_FF)]

    def conv(u, off, c):
        cols = slice(off + c * FF_CHUNK, off + (c + 1) * FF_CHUNK)
        u_prev = pltpu.roll(u, 1, 0)[HALO:HALO + TF]
        u_next = pltpu.roll(u, rows - 1, 0)[HALO:HALO + TF]
        for b in inner:
            u_prev = jnp.concatenate([u_prev[:b], u_prev[b:b + SUBLANES] * no_prev, u_prev[b + SUBLANES:]], axis=0)
            u_next = jnp.concatenate([u_next[:b - SUBLANES], u_next[b - SUBLANES:b] * no_next, u_next[b:]], axis=0)
        return (cw_ref[0:1, cols] * u_prev + cw_ref[1:2, cols] * u[HALO:HALO + TF]
                + cw_ref[2:3, cols] * u_next + cb_ref[:, cols])

    def down(c, a):
        part = _dot(a, wdn_ref[c * FF_CHUNK:(c + 1) * FF_CHUNK, :])
        if c == 0:
            acc_ref[...] = part
        else:
            acc_ref[...] += part

    u_ahead = up(0)
    a_prev = None
    for c in range(nch):
        u_val, u_gate = u_ahead
        if c + 1 < nch:
            u_ahead = up(c + 1)
        if a_prev is not None:
            down(c - 1, a_prev)
        a_prev = (_silu(conv(u_gate, D_FF, c)) * conv(u_val, 0, c)).astype(BF16)
    down(nch - 1, a_prev)
    o_ref[...] = x + gate_ref[...] * acc_ref[...]


def _mixer_out_ffn(geom, xs, act_pairs, w_out, mod_l, g, wup, cw, cb, wdn):
    assert TF % geom.s == 0 and geom.np_rows % TF == 0 and geom.ds % TF == 0
    per = TF // HALO
    npt = geom.np_rows // TF

    def halo_specs(width, nblk, first_tile):
        def prev(i):
            return (jnp.maximum((jnp.maximum(i - first_tile, 0)) * per - 1, 0), 0)

        def nxt(i):
            return (jnp.minimum((jnp.maximum(i - first_tile, 0) + 1) * per, nblk - 1), 0)
        return [pl.BlockSpec((HALO, width), prev), pl.BlockSpec((HALO, width), nxt)]

    x_split = len(xs) == 2
    if x_split:
        in_specs = geom.split_specs(TF, D) + halo_specs(D, geom.ns_rows // HALO, npt)
        args = [xs[0], xs[1], xs[1], xs[1]]
    else:
        in_specs = [pl.BlockSpec((TF, D), lambda i: (i, 0))] + halo_specs(D, geom.n // HALO, 0)
        args = [xs[0], xs[0], xs[0]]
    for a_p, a_s in act_pairs:
        in_specs += geom.split_specs(TF, a_p.shape[1]) + halo_specs(a_p.shape[1], geom.ns_rows // HALO, npt)
        args += [a_p, a_s, a_s, a_s]
    in_specs += [_const_spec(w_out.shape), geom.mod_spec(2, TF), _const_spec((1, D)),
                 geom.mod_spec(4, TF), geom.mod_spec(3, TF), geom.mod_spec(5, TF),
                 _const_spec(wup.shape), _const_spec(cw.shape), _const_spec(cb.shape), _const_spec(wdn.shape)]
    args += [w_out, mod_l, g, mod_l, mod_l, mod_l, wup, cw, cb, wdn]
    return pl.pallas_call(
        functools.partial(_ffn_kernel, npt=npt, tps=geom.ds // TF, seq=geom.s, x_split=x_split,
                          n_act=len(act_pairs)),
        out_shape=jax.ShapeDtypeStruct((geom.n, D), F32),
        grid=(geom.n // TF,),
        in_specs=in_specs,
        out_specs=pl.BlockSpec((TF, D), lambda i: (i, 0)),
        scratch_shapes=[pltpu.VMEM((TF, D), F32), pltpu.VMEM((TF + 2 * HALO, D), BF16),
                        pltpu.VMEM((TF + 2 * HALO, D), F32)],
        compiler_params=_params(("parallel",)),
        name="mixer_out_ffn",
    )(*args)


def _final_norm_kernel(x_ref, g_ref, o_ref):
    x = x_ref[...]
    o_ref[...] = x * lax.rsqrt(jnp.mean(x * x, axis=-1, keepdims=True) + EPS) * g_ref[...]


def _final_norm(x, g, row0, nrows):
    tn = 1024
    assert row0 % tn == 0 and nrows % tn == 0
    t0 = row0 // tn
    return pl.pallas_call(
        _final_norm_kernel,
        out_shape=jax.ShapeDtypeStruct((nrows, D), F32),
        grid=(nrows // tn,),
        in_specs=[pl.BlockSpec((tn, D), lambda i: (t0 + i, 0)), _const_spec((1, D))],
        out_specs=pl.BlockSpec((tn, D), lambda i: (i, 0)),
        compiler_params=_params(("parallel",)),
        name="final_norm",
    )(x, g)


def _prep_ab(w_in, w2_f, b2_f, w2_b, b2_b):
    scale = np.ones((w_in.shape[1],), np.float32)
    scale[0:NA_W] = HD ** -0.5 * LOG2E
    scale[3 * NA_W:3 * NA_W + GLA_KW] = GLA_DK ** -0.5
    w = jnp.pad(w_in * scale, ((0, 0), (0, AB_MAIN + LANES - w_in.shape[1]))).astype(BF16)
    w2 = jnp.zeros((LANES, 2 * GLA_KW), F32)
    w2 = w2.at[0:GLA_RANK, 0:GLA_KW].set(w2_f).at[GLA_RANK:2 * GLA_RANK, GLA_KW:].set(w2_b)
    w2_hi, w2_lo = _split(w2)
    b2 = jnp.concatenate([b2_f, b2_b])[None, :]
    return w, w2_hi, w2_lo, b2


def _prep_c(w_in, qn_g, kn_g):
    wq, wk, wv = w_in[:, :C_QW], w_in[:, C_QW:C_QW + C_KW], w_in[:, C_QW + C_KW:]

    def dup(w):
        return jnp.broadcast_to(w.reshape(D, GQA_KVH, 1, HD), (D, GQA_KVH, 2, HD)).reshape(D, 2 * C_KW)
    w = jnp.concatenate([wq, dup(wk), dup(wv)], axis=1).astype(BF16)
    gain = jnp.concatenate([jnp.tile(qn_g * (HD ** -0.5 * LOG2E), GQA_HEADS), jnp.tile(kn_g, 2 * GQA_KVH)])[None, :]
    return w, gain


def kernel(x_prompt, x_sample, cache_na_k, cache_na_v, state_gla_fwd, state_gla_bwd, cache_gqa_k, cache_gqa_v, c, c_ctx, ada_w, ada_b, norm_mix_g, norm_ffn_g, ab_w_in, ab_w_out, na_rpb, gla_w2_fwd, gla_b2_fwd, gla_w2_bwd, gla_b2_bwd, gla_norm_g, gqa_w_in, gqa_w_out, gqa_q_norm_g, gqa_k_norm_g, ffn_w_up, ffn_conv_w, ffn_conv_b, ffn_w_down, final_norm_g):
    b, s, _ = x_prompt.shape
    db, ds, _ = x_sample.shape
    geom = _Geom(b, s, db, ds)
    assert ds % GRID_W == 0 and geom.np_rows % ds == 0 and geom.np_rows % TP == 0

    xs = [x_prompt.reshape(-1, D), x_sample.reshape(-1, D)]
    nrow = -(-(db + 1) // SUBLANES) * SUBLANES
    c_rows = jnp.zeros((nrow, D), F32).at[:db].set(c).at[db].set(c_ctx)
    mod = _modulation(c_rows, ada_w, ada_b).reshape(DEPTH, nrow, 1, 6 * D)

    gla_consts = _gla_constants()
    rope = _rope_tables(geom)
    zeros_state = jnp.zeros((b, GLA_HEADS, GLA_DK, GLA_DV), F32)
    na_k, na_v, gla_f, gla_b, gq_k, gq_v = [], [], [], [], [], []

    for i in range(DEPTH):
        j = i // 2
        mod_l = mod[i]
        if i % 2 == 0:
            w, w2_hi, w2_lo, b2 = _prep_ab(ab_w_in[j], gla_w2_fwd[j], gla_b2_fwd[j], gla_w2_bwd[j], gla_b2_bwd[j])
            proj, gates = _ab_in_proj(geom, xs, mod_l, norm_mix_g[i][None, :], w, w2_hi, w2_lo, b2)
            oa = _na_attention(geom, proj, cache_na_k[:, j], cache_na_v[:, j], na_rpb[j])
            gain = gla_norm_g[j][None, :]
            yg_p, sf, sb = _gla_call(proj, gates, zeros_state, zeros_state, gain, gla_consts, s, b, 0, "gla_prompt")
            yg_s, _, _ = _gla_call(proj, gates, state_gla_fwd[:, j], state_gla_bwd[:, j], gain, gla_consts,
                                   ds, db, geom.np_rows // ds, "gla_sample")
            acts, w_out = [oa, (yg_p, yg_s)], ab_w_out[j]
            pk = proj[:geom.np_rows].reshape(b, s, -1)
            na_k.append(pk[..., NA_W:2 * NA_W].reshape(b, s, NA_HEADS, HD).transpose(0, 2, 1, 3).astype(F32))
            na_v.append(pk[..., 2 * NA_W:3 * NA_W].reshape(b, s, NA_HEADS, HD).transpose(0, 2, 1, 3).astype(F32))
            gla_f.append(sf)
            gla_b.append(sb)
        else:
            w, gain = _prep_c(gqa_w_in[j], gqa_q_norm_g[j], gqa_k_norm_g[j])
            proj = _c_in_proj(geom, xs[0], mod_l, norm_mix_g[i][None, :], w, gain, rope)
            o = _gqa_attention(geom, proj, cache_gqa_k[:, j], cache_gqa_v[:, j])
            acts, w_out = [o], gqa_w_out[j]
            pk = proj[:geom.np_rows].reshape(b, s, -1)
            kd = pk[..., C_QW:C_QW + 2 * C_KW].reshape(b, s, GQA_KVH, 2, HD)[:, :, :, 0]
            vd = pk[..., C_QW + 2 * C_KW:].reshape(b, s, GQA_KVH, 2, HD)[:, :, :, 0]
            gq_k.append(kd.transpose(0, 2, 1, 3).astype(F32))
            gq_v.append(vd.transpose(0, 2, 1, 3).astype(F32))
        x = _mixer_out_ffn(geom, xs, acts, w_out.astype(BF16), mod_l, norm_ffn_g[i][None, :],
                           ffn_w_up[i].astype(BF16), ffn_conv_w[i], ffn_conv_b[i][None, :], ffn_w_down[i].astype(BF16))
        xs = [x]

    y_prompt = _final_norm(x, final_norm_g[None, :], 0, geom.np_rows).reshape(b, s, D)
    y_sample = _final_norm(x, final_norm_g[None, :], geom.np_rows, geom.ns_rows).reshape(db, ds, D)
    return (y_prompt, y_sample,
            jnp.stack(na_k, axis=1), jnp.stack(na_v, axis=1),
            jnp.stack(gla_f, axis=1), jnp.stack(gla_b, axis=1),
            jnp.stack(gq_k, axis=1), jnp.stack(gq_v, axis=1))
```
